```python
import jax, jax.numpy as jnp
from jax import lax
import numpy as np

D_MODEL = 2048
BATCH = 4
SEQ = 4096
DEPTH = 2

NORM_EPS = 1e-6
MIX_WIDTH = D_MODEL
CONV_CH = MIX_WIDTH // 2
CONV_K = 3
GLA_HEADS = 4
GLA_DV = (MIX_WIDTH // 2) // GLA_HEADS
GLA_DK = GLA_DV // 2
GLA_GATE_RANK = 16
GLA_GATE_TEMP = 16.0
GLA_CHUNK = 64
ATTN_HEADS = 16
ATTN_HEAD_DIM = D_MODEL // ATTN_HEADS
DILATED_GROUPS = ((128, 1), (512, 4), (2048, 16))
ATTN_BLOCK = 128
D_FF = 4 * D_MODEL
N_EVEN = (DEPTH + 1) // 2
N_ODD = DEPTH // 2
IN_SIZES = (CONV_CH, CONV_CH, CONV_CH, GLA_HEADS * GLA_DK, GLA_HEADS * GLA_DK,
            GLA_HEADS * GLA_DV, GLA_HEADS * GLA_DV, GLA_GATE_RANK)
IN_OFFSETS = tuple(int(v) for v in np.cumsum(IN_SIZES)[:-1])
IN_COLS = sum(IN_SIZES)

kernel_name = "hybrid_conv_gla_dilated_trunk"


def rmsnorm(x, g):
    xf = x.astype(jnp.float32)
    xf = xf * lax.rsqrt(jnp.mean(xf * xf, axis=-1, keepdims=True) + NORM_EPS)
    return (xf * g.astype(jnp.float32)).astype(x.dtype)


def causal_short_conv(u, w):
    return lax.conv_general_dilated(
        u, w[:, None, :], window_strides=(1,), padding=[(CONV_K - 1, 0)],
        dimension_numbers=('NWC', 'WIO', 'NWC'), feature_group_count=u.shape[-1])


def gla_chunked(q, k, v, log_a):
    b, t, h, dk = q.shape
    dv = v.shape[-1]
    n = t // GLA_CHUNK

    def chunks(a):
        return a.astype(jnp.float32).reshape(b, n, GLA_CHUNK, h, a.shape[-1]).transpose(1, 0, 3, 2, 4)

    qc, kc, vc = chunks(q), chunks(k), chunks(v)
    gc = jnp.cumsum(chunks(log_a), axis=3)
    causal = jnp.tril(jnp.ones((GLA_CHUNK, GLA_CHUNK), dtype=bool))

    def step(state, inp):
        qi, ki, vi, gi = inp
        o_inter = jnp.einsum('bhck,bhkv->bhcv', qi * jnp.exp(gi), state)
        rel = gi[:, :, :, None, :] - gi[:, :, None, :, :]
        decay = jnp.exp(jnp.where(causal[:, :, None], rel, -jnp.inf))
        scores = jnp.einsum('bhik,bhijk,bhjk->bhij', qi, decay, ki)
        o_intra = jnp.einsum('bhij,bhjv->bhiv', scores, vi)
        g_last = gi[:, :, -1]
        k_dec = ki * jnp.exp(g_last[:, :, None] - gi)
        state = jnp.exp(g_last)[..., None] * state + jnp.einsum('bhjk,bhjv->bhkv', k_dec, vi)
        return state, o_inter + o_intra

    s0 = jnp.zeros((b, h, dk, dv), jnp.float32)
    _, o = lax.scan(step, s0, (qc, kc, vc, gc))
    return o.transpose(1, 0, 3, 2, 4).reshape(b, t, h, dv).astype(q.dtype)


def conv_gla_mixer(h, w_in, conv_w, w_gate2, b_gate, gla_norm_g, w_out):
    b, t, _ = h.shape
    a_b, a_c, a_x, q, k, v, r, g_low = jnp.split(h @ w_in, IN_OFFSETS, axis=-1)
    y_conv = a_b * causal_short_conv(a_c * a_x, conv_w)
    q = q.reshape(b, t, GLA_HEADS, GLA_DK) * (GLA_DK ** -0.5)
    k = k.reshape(b, t, GLA_HEADS, GLA_DK)
    v = v.reshape(b, t, GLA_HEADS, GLA_DV)
    log_a = jax.nn.log_sigmoid((g_low @ w_gate2 + b_gate).astype(jnp.float32)) / GLA_GATE_TEMP
    o = gla_chunked(q, k, v, log_a.reshape(b, t, GLA_HEADS, GLA_DK))
    o = rmsnorm(o, gla_norm_g) * jax.nn.silu(r.reshape(b, t, GLA_HEADS, GLA_DV))
    y = jnp.concatenate([y_conv, o.reshape(b, t, GLA_HEADS * GLA_DV)], axis=-1)
    return y @ w_out


def dilated_branch(q, k, v, slopes, window, dilation):
    b, t, h, dh = q.shape
    steps = window // dilation
    assert steps <= ATTN_BLOCK
    span = dilation * ATTN_BLOCK
    tp = -(-t // span) * span
    n_sub = tp // dilation
    nb = n_sub // ATTN_BLOCK

    def blocks(a):
        a = jnp.pad(a, ((0, 0), (0, tp - t), (0, 0), (0, 0)))
        a = a.reshape(b, n_sub, dilation, h, dh).transpose(0, 2, 3, 1, 4)
        return a.reshape(b, dilation, h, nb, ATTN_BLOCK, dh)

    def with_prev(a):
        prev = jnp.pad(a, ((0, 0), (0, 0), (0, 0), (1, 0), (0, 0), (0, 0)))[:, :, :, :-1]
        return jnp.concatenate([prev, a], axis=4)

    qb = blocks(q)
    kw, vw = with_prev(blocks(k)), with_prev(blocks(v))
    s = jnp.einsum('brhnqc,brhnkc->brhnqk', qb, kw).astype(jnp.float32) * (dh ** -0.5)
    qi = jnp.arange(ATTN_BLOCK)[:, None]
    ki = jnp.arange(2 * ATTN_BLOCK)[None, :]
    diff = ATTN_BLOCK + qi - ki
    band = (diff >= 0) & (diff <= steps)
    after_start = (jnp.arange(nb)[:, None, None] > 0) | (ki[None] >= ATTN_BLOCK)
    mask = band[None] & after_start
    alibi = -slopes[:, None, None] * (dilation * diff).astype(jnp.float32)[None]
    s = jnp.where(mask[None, None, None], s + alibi[None, None, :, None], -jnp.inf)
    m = jnp.max(s, axis=-1, keepdims=True)
    p = jnp.exp(s - m)
    z = jnp.sum(p, axis=-1, keepdims=True)
    o = jnp.einsum('brhnqk,brhnkc->brhnqc', p, vw.astype(jnp.float32)) / z
    lse = (m + jnp.log(z))[..., 0]
    o = o.reshape(b, dilation, h, n_sub, dh).transpose(0, 3, 1, 2, 4).reshape(b, tp, h, dh)[:, :t]
    lse = lse.reshape(b, dilation, h, n_sub).transpose(0, 3, 1, 2).reshape(b, tp, h)[:, :t]
    return o, lse


def dilated_attention(h, w_qkv, w_o):
    b, t, _ = h.shape
    qkv = (h @ w_qkv).reshape(b, t, 3, ATTN_HEADS, ATTN_HEAD_DIM)
    q, k, v = qkv[:, :, 0], qkv[:, :, 1], qkv[:, :, 2]
    slopes = jnp.exp2(-8.0 * jnp.arange(1, ATTN_HEADS + 1, dtype=jnp.float32) / ATTN_HEADS)
    outs, lses = zip(*[dilated_branch(q, k, v, slopes, win, dil) for win, dil in DILATED_GROUPS])
    wts = jax.nn.softmax(jnp.stack(lses), axis=0)
    o = jnp.sum(wts[..., None] * jnp.stack(outs), axis=0)
    return o.reshape(b, t, D_MODEL).astype(h.dtype) @ w_o


def squared_relu_mlp(h, w1, w2):
    return jnp.square(jax.nn.relu(h @ w1)) @ w2


def setup_inputs(seed: int = 0) -> dict:
    key = jax.random.key(seed)
    ks = jax.random.split(key, 14)
    nrm = jax.random.normal
    f32 = jnp.float32
    return {
        'x': nrm(ks[0], (BATCH, SEQ, D_MODEL), f32),
        'norm_mix_g': 1.0 + 0.01 * nrm(ks[1], (DEPTH, D_MODEL), f32),
        'norm_mlp_g': 1.0 + 0.01 * nrm(ks[2], (DEPTH, D_MODEL), f32),
        'final_norm_g': 1.0 + 0.01 * nrm(ks[3], (D_MODEL,), f32),
        'hyb_w_in': nrm(ks[4], (N_EVEN, D_MODEL, IN_COLS), f32) * D_MODEL ** -0.5,
        'conv_w': nrm(ks[5], (N_EVEN, CONV_K, CONV_CH), f32) * CONV_K ** -0.5,
        'gla_w_gate2': nrm(ks[6], (N_EVEN, GLA_GATE_RANK, GLA_HEADS * GLA_DK), f32) * GLA_GATE_RANK ** -0.5,
        'gla_b_gate': 0.01 * nrm(ks[7], (N_EVEN, GLA_HEADS * GLA_DK), f32),
        'gla_norm_g': 1.0 + 0.01 * nrm(ks[8], (N_EVEN, GLA_DV), f32),
        'hyb_w_out': nrm(ks[9], (N_EVEN, MIX_WIDTH, D_MODEL), f32) * MIX_WIDTH ** -0.5,
        'attn_w_qkv': nrm(ks[10], (N_ODD, D_MODEL, 3 * D_MODEL), f32) * D_MODEL ** -0.5,
        'attn_w_o': nrm(ks[11], (N_ODD, D_MODEL, D_MODEL), f32) * D_MODEL ** -0.5,
        'mlp_w1': nrm(ks[12], (DEPTH, D_MODEL, D_FF), f32) * D_MODEL ** -0.5,
        'mlp_w2': nrm(ks[13], (DEPTH, D_FF, D_MODEL), f32) * D_FF ** -0.5,
    }


def reference(x, norm_mix_g, norm_mlp_g, final_norm_g, hyb_w_in, conv_w, gla_w_gate2,
              gla_b_gate, gla_norm_g, hyb_w_out, attn_w_qkv, attn_w_o, mlp_w1, mlp_w2):
    for layer in range(DEPTH):
        h = rmsnorm(x, norm_mix_g[layer])
        if layer % 2 == 0:
            e = layer // 2
            mix = conv_gla_mixer(h, hyb_w_in[e], conv_w[e], gla_w_gate2[e], gla_b_gate[e],
                                 gla_norm_g[e], hyb_w_out[e])
        else:
            o = layer // 2
            mix = dilated_attention(h, attn_w_qkv[o], attn_w_o[o])
        x = x + mix
        x = x + squared_relu_mlp(rmsnorm(x, norm_mlp_g[layer]), mlp_w1[layer], mlp_w2[layer])
    return rmsnorm(x, final_norm_g)
```

```python
import functools

import jax
import jax.numpy as jnp
import numpy as np
from jax import lax
from jax.experimental import pallas as pl
from jax.experimental.pallas import tpu as pltpu

F32 = jnp.float32
BF16 = jnp.bfloat16

NORM_EPS = 1e-6
CONV_K = 3
GLA_HEADS = 4
GLA_GATE_RANK = 16
GLA_GATE_TEMP = 16.0
ATTN_HEADS = 16
DILATED_GROUPS = ((128, 1), (512, 4), (2048, 16))
DIL_MAX = 16
ATTN_STEPS = 128

LANES = 128
GLA_CHUNK = 256
GLA_SUB = 16
MASK_VALUE = -1e30
VMEM_LIMIT_BYTES = 56 * 1024 * 1024


def _params(*semantics):
    return pltpu.CompilerParams(dimension_semantics=semantics, vmem_limit_bytes=VMEM_LIMIT_BYTES)


def _dot(a, b):
    return jnp.dot(a, b, preferred_element_type=F32)


def _dot_nt(a, b):
    return lax.dot_general(a, b, (((1,), (1,)), ((), ())), preferred_element_type=F32)


def _dot_tn(a, b):
    return lax.dot_general(a, b, (((0,), (0,)), ((), ())), preferred_element_type=F32)


def _rmsnorm(x, g):
    return x * lax.rsqrt(jnp.mean(x * x, axis=-1, keepdims=True) + NORM_EPS) * g


def _norm_rows_into(x_ref, g_ref, xn_ref, row_groups, d_model):
    g = g_ref[...]
    if row_groups == 1:
        xn_ref[...] = _rmsnorm(x_ref[...], g).astype(BF16)
    else:
        sub = xn_ref.shape[0] // row_groups
        for r in range(row_groups):
            xr = x_ref[:, r * d_model:(r + 1) * d_model]
            xn_ref[r * sub:(r + 1) * sub, :] = _rmsnorm(xr, g).astype(BF16)


def _in_proj_gla_kernel(x_ref, g_ref, w_ref, wlow_ref, wgate_ref, bgate_ref, o_ref, loga_ref, xn_ref):
    j = pl.program_id(1)

    @pl.when(j == 0)
    def _():
        _norm_rows_into(x_ref, g_ref, xn_ref, 1, x_ref.shape[1])
        g_low = _dot(xn_ref[...], wlow_ref[...]).astype(BF16)
        gate = _dot(g_low, wgate_ref[...]) + bgate_ref[...]
        log_sig = jnp.minimum(gate, 0.0) - jnp.log(1.0 + jnp.exp(-jnp.abs(gate)))
        loga_ref[...] = log_sig * (1.0 / GLA_GATE_TEMP)

    o_ref[...] = _dot(xn_ref[...], w_ref[...]).astype(o_ref.dtype)


def _in_proj_gla(x, gain, w, w_low, w_gate, b_gate, *, tm=512, tn=1024):
    n, d = x.shape
    cols = w.shape[1]
    n_gate = w_gate.shape[1]
    return pl.pallas_call(
        _in_proj_gla_kernel,
        grid=(n // tm, cols // tn),
        in_specs=[
            pl.BlockSpec((tm, d), lambda i, j: (i, 0)),
            pl.BlockSpec((1, d), lambda i, j: (0, 0)),
            pl.BlockSpec((d, tn), lambda i, j: (0, j)),
            pl.BlockSpec((d, LANES), lambda i, j: (0, 0)),
            pl.BlockSpec((LANES, n_gate), lambda i, j: (0, 0)),
            pl.BlockSpec((1, n_gate), lambda i, j: (0, 0)),
        ],
        out_specs=[
            pl.BlockSpec((tm, tn), lambda i, j: (i, j)),
            pl.BlockSpec((tm, n_gate), lambda i, j: (i, 0)),
        ],
        out_shape=[
            jax.ShapeDtypeStruct((n, cols), BF16),
            jax.ShapeDtypeStruct((n, n_gate), F32),
        ],
        scratch_shapes=[pltpu.VMEM((tm, d), BF16)],
        compiler_params=_params("parallel", "arbitrary"),
        name="in_proj_gla",
    )(x, gain, w, w_low, w_gate, b_gate)


def _qkv_proj_kernel(x_ref, g_ref, w_ref, o_ref, xn_ref, *, row_groups, d_model, q_tiles, q_scale):
    j = pl.program_id(1)

    @pl.when(j == 0)
    def _():
        _norm_rows_into(x_ref, g_ref, xn_ref, row_groups, d_model)

    acc = _dot(xn_ref[...], w_ref[...])
    scale = jnp.where(j < q_tiles, q_scale, 1.0).astype(F32)
    o_ref[...] = (acc * scale).astype(o_ref.dtype)


def _qkv_proj_residue_major(x, gain, w, *, batch, seq, q_scale, row_groups=2, tn=1024):
    n, d = x.shape
    cols = w.shape[1]
    sub = seq // DIL_MAX
    tm = row_groups * sub
    groups_per_seq = DIL_MAX // row_groups
    x_view = x.reshape(batch, sub, DIL_MAX * d)
    kern = functools.partial(_qkv_proj_kernel, row_groups=row_groups, d_model=d,
                             q_tiles=(cols // 3) // tn, q_scale=q_scale)
    return pl.pallas_call(
        kern,
        grid=(n // tm, cols // tn),
        in_specs=[
            pl.BlockSpec((None, sub, row_groups * d),
                         lambda i, j: (i // groups_per_seq, 0, i % groups_per_seq)),
            pl.BlockSpec((1, d), lambda i, j: (0, 0)),
            pl.BlockSpec((d, tn), lambda i, j: (0, j)),
        ],
        out_specs=pl.BlockSpec((tm, tn), lambda i, j: (i, j)),
        out_shape=jax.ShapeDtypeStruct((n, cols), BF16),
        scratch_shapes=[pltpu.VMEM((tm, d), BF16)],
        compiler_params=_params("parallel", "arbitrary"),
        name="qkv_proj",
    )(x_view, gain, w)


def _conv_kernel(ab_ref, ac_ref, ax_ref, w_ref, o_ref, ubuf_ref):
    tb = ab_ref.shape[0]
    pad = 8

    @pl.when(pl.program_id(1) == 0)
    def _():
        ubuf_ref[0:pad, :] = jnp.zeros((pad, ubuf_ref.shape[1]), F32)

    u = ac_ref[...].astype(F32) * ax_ref[...].astype(F32)
    ubuf_ref[pad:pad + tb, :] = u
    u1 = ubuf_ref[pad - 1:pad - 1 + tb, :]
    u2 = ubuf_ref[pad - 2:pad - 2 + tb, :]
    w = w_ref[...]
    y = ab_ref[...].astype(F32) * (w[0:1, :] * u2 + w[1:2, :] * u1 + w[2:3, :] * u)
    o_ref[...] = y.astype(o_ref.dtype)
    ubuf_ref[0:pad, :] = ubuf_ref[tb:tb + pad, :]


def _gated_conv(proj, conv_w, *, batch, seq, ch, tb=512):
    n = proj.shape[0]
    tpb = seq // tb
    return pl.pallas_call(
        _conv_kernel,
        grid=(batch, tpb),
        in_specs=[
            pl.BlockSpec((tb, ch), lambda b, t: (b * tpb + t, 0)),
            pl.BlockSpec((tb, ch), lambda b, t: (b * tpb + t, 1)),
            pl.BlockSpec((tb, ch), lambda b, t: (b * tpb + t, 2)),
            pl.BlockSpec((CONV_K, ch), lambda b, t: (0, 0)),
        ],
        out_specs=pl.BlockSpec((tb, ch), lambda b, t: (b * tpb + t, 0)),
        out_shape=jax.ShapeDtypeStruct((n, ch), BF16),
        scratch_shapes=[pltpu.VMEM((tb + 8, ch), F32)],
        compiler_params=_params("parallel", "arbitrary"),
        name="gated_conv",
    )(proj, proj, proj, conv_w)


def _gla_kernel(q_ref, k_ref, v_ref, r_ref, la_ref, tri_ref, gn_ref, o_ref, st_ref, *, q_scale):
    c_len, dk = q_ref.shape
    n_sub = c_len // GLA_SUB

    @pl.when(pl.program_id(2) == 0)
    def _():
        st_ref[...] = jnp.zeros_like(st_ref)

    la = la_ref[...]
    tri = tri_ref[...]
    la_hi = la.astype(BF16)
    rem = la - la_hi.astype(F32)
    la_mid = rem.astype(BF16)
    la_lo = (rem - la_mid.astype(F32)).astype(BF16)
    g = _dot(tri, la_hi) + _dot(tri, la_mid) + _dot(tri, la_lo)
    g_last = g[c_len - 1:c_len, :]

    q = q_ref[...].astype(F32) * q_scale
    k = k_ref[...].astype(F32)
    v = v_ref[...]

    st = st_ref[...]
    o = _dot_nt((q * jnp.exp(g)).astype(BF16), st.astype(BF16))
    k_dec = (k * jnp.exp(g_last - g)).astype(BF16)
    st_ref[...] = st * jnp.exp(g_last) + _dot_tn(v, k_dec)

    score_rows = [jnp.zeros((GLA_SUB, c_len), F32)]
    for a in range(1, n_sub):
        lo = a * GLA_SUB
        n_keys = LANES * -(-lo // LANES)
        g_ref_row = g[lo:lo + 1, :]
        q_a = (q[lo:lo + GLA_SUB] * jnp.exp(g[lo:lo + GLA_SUB] - g_ref_row)).astype(BF16)
        k_a = (k[:n_keys] * jnp.exp(jnp.minimum(g_ref_row - g[:n_keys], 0.0))).astype(BF16)
        s_a = _dot_nt(q_a, k_a)
        col = lax.broadcasted_iota(jnp.int32, s_a.shape, 1)
        s_a = jnp.where(col < lo, s_a, 0.0)
        if n_keys < c_len:
            s_a = jnp.concatenate([s_a, jnp.zeros((GLA_SUB, c_len - n_keys), F32)], axis=1)
        score_rows.append(s_a)
    scores = jnp.concatenate(score_rows, axis=0)

    row = lax.broadcasted_iota(jnp.int32, (c_len, LANES), 0)
    lane = lax.broadcasted_iota(jnp.int32, (c_len, LANES), 1)
    delta = row % LANES - lane
    band_id = jnp.where((delta >= 0) & (delta <= row % GLA_SUB), delta, -1)
    band = jnp.zeros((c_len, LANES), F32)
    for d in range(GLA_SUB):
        k_d = k if d == 0 else pltpu.roll(k, d, 0)
        g_d = g if d == 0 else pltpu.roll(g, d, 0)
        e = jnp.exp(jnp.minimum(g - g_d, 0.0))
        diag = jnp.sum(q * k_d * e, axis=-1, keepdims=True)
        band = jnp.where(band_id == d, diag, band)
    zeros = jnp.zeros((LANES, LANES), F32)
    band_rows = []
    for t in range(c_len // LANES):
        tiles = [zeros] * (c_len // LANES)
        tiles[t] = band[t * LANES:(t + 1) * LANES]
        band_rows.append(jnp.concatenate(tiles, axis=1))
    scores = scores + jnp.concatenate(band_rows, axis=0)

    o = o + _dot(scores.astype(BF16), v)
    o = _rmsnorm(o, gn_ref[...])
    r = r_ref[...].astype(F32)
    o_ref[...] = (o * (r / (1.0 + jnp.exp(-r)))).astype(o_ref.dtype)


def _gla(proj, log_a, tri, norm_g, *, batch, seq, q_off, k_off, v_off, r_off, dk, dv):
    n = proj.shape[0]
    cpb = seq // GLA_CHUNK
    row = lambda b, h, c: b * cpb + c
    kern = functools.partial(_gla_kernel, q_scale=float(dk) ** -0.5)
    return pl.pallas_call(
        kern,
        grid=(batch, GLA_HEADS, cpb),
        in_specs=[
            pl.BlockSpec((GLA_CHUNK, dk), lambda b, h, c: (row(b, h, c), q_off // dk + h)),
            pl.BlockSpec((GLA_CHUNK, dk), lambda b, h, c: (row(b, h, c), k_off // dk + h)),
            pl.BlockSpec((GLA_CHUNK, dv), lambda b, h, c: (row(b, h, c), v_off // dv + h)),
            pl.BlockSpec((GLA_CHUNK, dv), lambda b, h, c: (row(b, h, c), r_off // dv + h)),
            pl.BlockSpec((GLA_CHUNK, dk), lambda b, h, c: (row(b, h, c), h)),
            pl.BlockSpec((GLA_CHUNK, GLA_CHUNK), lambda b, h, c: (0, 0)),
            pl.BlockSpec((1, dv), lambda b, h, c: (0, 0)),
        ],
        out_specs=pl.BlockSpec((GLA_CHUNK, dv), lambda b, h, c: (row(b, h, c), h)),
        out_shape=jax.ShapeDtypeStruct((n, GLA_HEADS * dv), BF16),
        scratch_shapes=[pltpu.VMEM((dv, dk), F32)],
        compiler_params=_params("parallel", "parallel", "arbitrary"),
        name="gla",
    )(proj, proj, proj, proj, log_a, tri, norm_g)


def _out_proj_kernel(*refs, n_lhs, row_groups, d_model):
    lhs_refs, w_refs = refs[:n_lhs], refs[n_lhs:2 * n_lhs]
    x_ref, o_ref = refs[2 * n_lhs], refs[2 * n_lhs + 1]
    acc = _dot(lhs_refs[0][...], w_refs[0][...])
    for l_ref, w_ref in zip(lhs_refs[1:], w_refs[1:]):
        acc = acc + _dot(l_ref[...], w_ref[...])
    if row_groups == 1:
        o_ref[...] = x_ref[...] + acc
    else:
        sub = o_ref.shape[0] // row_groups
        for r in range(row_groups):
            o_ref[r * sub:(r + 1) * sub, :] = (
                x_ref[:, r * d_model:(r + 1) * d_model] + acc[r * sub:(r + 1) * sub])


def _out_proj_residual(lhs_list, w_list, x, *, tm=512, residue_major=None):
    n, d = x.shape
    n_lhs = len(lhs_list)
    if residue_major is None:
        row_groups = 1
        x_in = x
        x_spec = pl.BlockSpec((tm, d), lambda i: (i, 0))
    else:
        batch, seq = residue_major
        sub = seq // DIL_MAX
        row_groups = tm // sub
        gps = DIL_MAX // row_groups
        x_in = x.reshape(batch, sub, DIL_MAX * d)
        x_spec = pl.BlockSpec((None, sub, row_groups * d), lambda i: (i // gps, 0, i % gps))
    kern = functools.partial(_out_proj_kernel, n_lhs=n_lhs, row_groups=row_groups, d_model=d)
    in_specs = [pl.BlockSpec((tm, l.shape[1]), lambda i: (i, 0)) for l in lhs_list]
    in_specs += [pl.BlockSpec(w.shape, lambda i: (0, 0)) for w in w_list]
    in_specs.append(x_spec)
    return pl.pallas_call(
        kern,
        grid=(n // tm,),
        in_specs=in_specs,
        out_specs=pl.BlockSpec((tm, d), lambda i: (i, 0)),
        out_shape=jax.ShapeDtypeStruct((n, d), F32),
        compiler_params=_params("parallel"),
        name="out_proj",
    )(*lhs_list, *w_list, x_in)


def _mlp_kernel(x_ref, g_ref, w1_ref, w2_ref, gf_ref, o_ref, xn_ref, acc_ref, *,
                final_norm, row_groups, d_model):
    f = pl.program_id(1)

    @pl.when(f == 0)
    def _():
        _norm_rows_into(x_ref, g_ref, xn_ref, 1, d_model)

    h = _dot(xn_ref[...], w1_ref[...])
    h = jnp.square(jnp.maximum(h, 0.0)).astype(BF16)
    part = _dot(h, w2_ref[...])

    @pl.when(f == 0)
    def _():
        acc_ref[...] = part

    @pl.when(f > 0)
    def _():
        acc_ref[...] += part

    @pl.when(f == pl.num_programs(1) - 1)
    def _():
        y = x_ref[...] + acc_ref[...]
        if final_norm:
            y = _rmsnorm(y, gf_ref[...])
        if row_groups == 1:
            o_ref[...] = y
        else:
            sub = y.shape[0] // row_groups
            for r in range(row_groups):
                o_ref[:, r * d_model:(r + 1) * d_model] = y[r * sub:(r + 1) * sub]


def _mlp_residual(x, gain, w1, w2, final_gain, *, final_norm, residue_major=None, tm=512, tf=512):
    n, d = x.shape
    d_ff = w1.shape[1]
    if residue_major is None:
        row_groups = 1
        out_shape = jax.ShapeDtypeStruct((n, d), F32)
        out_spec = pl.BlockSpec((tm, d), lambda i, f: (i, 0))
    else:
        batch, seq = residue_major
        sub = seq // DIL_MAX
        row_groups = tm // sub
        gps = DIL_MAX // row_groups
        out_shape = jax.ShapeDtypeStruct((batch, sub, DIL_MAX * d), F32)
        out_spec = pl.BlockSpec((None, sub, row_groups * d), lambda i, f: (i // gps, 0, i % gps))
    kern = functools.partial(_mlp_kernel, final_norm=final_norm, row_groups=row_groups, d_model=d)
    out = pl.pallas_call(
        kern,
        grid=(n // tm, d_ff // tf),
        in_specs=[
            pl.BlockSpec((tm, d), lambda i, f: (i, 0)),
            pl.BlockSpec((1, d), lambda i, f: (0, 0)),
            pl.BlockSpec((d, tf), lambda i, f: (0, f)),
            pl.BlockSpec((tf, d), lambda i, f: (f, 0)),
            pl.BlockSpec((1, d), lambda i, f: (0, 0)),
        ],
        out_specs=out_spec,
        out_shape=out_shape,
        scratch_shapes=[pltpu.VMEM((tm, d), BF16), pltpu.VMEM((tm, d), F32)],
        compiler_params=_params("parallel", "arbitrary"),
        name="mlp",
    )(x, gain, w1, w2, final_gain)
    return out.reshape(n, d)


def _attn_bias_tables(slopes, sub):
    def table(q_pos, k_pos, dilation):
        diff = q_pos[:, None] - k_pos[None, :]
        valid = jnp.asarray((diff >= 0) & (diff <= ATTN_STEPS))
        bias = -slopes[:, None, None] * jnp.asarray((dilation * diff).astype(np.float32))[None]
        return jnp.where(valid[None], bias, MASK_VALUE).astype(F32)

    n = np.arange(sub)
    b16 = table(n, n, 16)[:, None]

    q = np.arange(128)
    q4 = 4 * (q % 32) + q // 32
    k = np.arange(256)
    k4 = 4 * (k % 64) + k // 64
    b4 = jnp.stack([table(q4, k4, 4), table(q4 + 128, k4, 4)], axis=1)

    q = np.arange(256)
    q1 = 16 * (q % 16) + q // 16
    k = np.arange(512)
    k1 = 16 * (k % 32) + k // 32
    b1 = jnp.stack([table(q1, k1, 1), table(q1 + 256, k1, 1)], axis=1)
    return b1, b4, b16


def _attn_kernel(q_ref, k_ref, v_ref, b1_ref, b4_ref, b16_ref, o_ref, m_ref, l_ref, acc_ref, *, sub):
    dh = q_ref.shape[1]

    def softmax_tile(s):
        m = jnp.max(s, axis=-1, keepdims=True)
        p = jnp.exp(s - m)
        return m, jnp.sum(p, axis=-1, keepdims=True), p.astype(BF16)

    def gather(ref, starts, size):
        return jnp.concatenate([ref[pl.ds(s, size), :] for s in starts], axis=0)

    def scatter(ref, starts, size, val):
        for c, s in enumerate(starts):
            ref[pl.ds(s, size), :] = val[c * size:(c + 1) * size]

    def lanes(x):
        return jnp.broadcast_to(x, (x.shape[0], LANES))

    def group16(r, carry):
        rows = pl.ds(pl.multiple_of(r * sub, sub), sub)
        s = _dot_nt(q_ref[rows, :], k_ref[rows, :]) + b16_ref[0]
        m, l, p = softmax_tile(s)
        m_ref[rows, :] = lanes(m)
        l_ref[rows, :] = lanes(l)
        acc_ref[rows, :] = _dot(p, v_ref[rows, :])
        return carry

    lax.fori_loop(0, DIL_MAX, group16, 0)

    def online_update(s, v, starts, size, final):
        m_old = gather(m_ref, starts, size)[:, :1]
        l_old = gather(l_ref, starts, size)[:, :1]
        acc_old = gather(acc_ref, starts, size)
        m_new = jnp.maximum(m_old, jnp.max(s, axis=-1, keepdims=True))
        alpha = jnp.exp(m_old - m_new)
        p = jnp.exp(s - m_new)
        l_new = alpha * l_old + jnp.sum(p, axis=-1, keepdims=True)
        acc_new = alpha * acc_old + _dot(p.astype(BF16), v)
        if final:
            scatter(o_ref, starts, size, (acc_new / l_new).astype(o_ref.dtype))
        else:
            scatter(m_ref, starts, size, lanes(m_new))
            scatter(l_ref, starts, size, lanes(l_new))
            scatter(acc_ref, starts, size, acc_new)

    def group4(i, carry):
        r4, blk = i // (sub // 32), i % (sub // 32)
        first = blk == 0
        q_starts = [pl.multiple_of((r4 + 4 * c) * sub + 32 * blk, 32) for c in range(4)]
        k_starts = [pl.multiple_of((r4 + 4 * c) * sub + 32 * jnp.maximum(blk - 1, 0), 32)
                    for c in range(4)]
        s = _dot_nt(gather(q_ref, q_starts, 32), gather(k_ref, k_starts, 64))
        s = s + b4_ref[jnp.where(first, 0, 1)]
        online_update(s, gather(v_ref, k_starts, 64), q_starts, 32, False)
        return carry

    lax.fori_loop(0, 4 * (sub // 32), group4, 0)

    def group1(blk, carry):
        first = blk == 0
        q_starts = [pl.multiple_of(r * sub + 16 * blk, 16) for r in range(DIL_MAX)]
        k_starts = [pl.multiple_of(r * sub + 16 * jnp.maximum(blk - 1, 0), 16)
                    for r in range(DIL_MAX)]
        s = _dot_nt(gather(q_ref, q_starts, 16), gather(k_ref, k_starts, 32))
        s = s + b1_ref[jnp.where(first, 0, 1)]
        online_update(s, gather(v_ref, k_starts, 32), q_starts, 16, True)
        return carry

    lax.fori_loop(0, sub // 16, group1, 0)


def _dilated_attention(qkv, slopes, *, batch, seq, dh):
    n = qkv.shape[0]
    sub = seq // DIL_MAX
    b1, b4, b16 = _attn_bias_tables(slopes, sub)
    kern = functools.partial(_attn_kernel, sub=sub)
    return pl.pallas_call(
        kern,
        grid=(batch, ATTN_HEADS),
        in_specs=[
            pl.BlockSpec((seq, dh), lambda b, h: (b, h)),
            pl.BlockSpec((seq, dh), lambda b, h: (b, ATTN_HEADS + h)),
            pl.BlockSpec((seq, dh), lambda b, h: (b, 2 * ATTN_HEADS + h)),
            pl.BlockSpec((None,) + b1.shape[1:], lambda b, h: (h, 0, 0, 0)),
            pl.BlockSpec((None,) + b4.shape[1:], lambda b, h: (h, 0, 0, 0)),
            pl.BlockSpec((None,) + b16.shape[1:], lambda b, h: (h, 0, 0, 0)),
        ],
        out_specs=pl.BlockSpec((seq, dh), lambda b, h: (b, h)),
        out_shape=jax.ShapeDtypeStruct((n, ATTN_HEADS * dh), BF16),
        scratch_shapes=[pltpu.VMEM((seq, LANES), F32), pltpu.VMEM((seq, LANES), F32),
                        pltpu.VMEM((seq, dh), F32)],
        compiler_params=_params("parallel", "parallel"),
        name="dilated_attn",
    )(qkv, qkv, qkv, b1, b4, b16)


def kernel(x, norm_mix_g, norm_mlp_g, final_norm_g, hyb_w_in, conv_w, gla_w_gate2, gla_b_gate,
           gla_norm_g, hyb_w_out, attn_w_qkv, attn_w_o, mlp_w1, mlp_w2):
    batch, seq, d = x.shape
    n = batch * seq
    depth = norm_mix_g.shape[0]
    conv_ch = conv_w.shape[-1]
    dv = gla_norm_g.shape[-1]
    dk = gla_w_gate2.shape[-1] // GLA_HEADS
    dh = d // ATTN_HEADS
    assert depth == 2 and seq % (DIL_MAX * ATTN_STEPS) == 0 and seq % GLA_CHUNK == 0
    assert dk == LANES and dh == LANES and dv % LANES == 0

    row = lambda v: v.reshape(1, -1).astype(F32)
    xf = x.reshape(n, d)

    main_cols = 3 * conv_ch + 2 * GLA_HEADS * dk + 2 * GLA_HEADS * dv
    w_in = hyb_w_in[0]
    w_main = w_in[:, :main_cols].astype(BF16)
    w_low = jnp.pad(w_in[:, main_cols:], ((0, 0), (0, LANES - GLA_GATE_RANK))).astype(BF16)
    w_gate = jnp.pad(gla_w_gate2[0], ((0, LANES - GLA_GATE_RANK), (0, 0))).astype(BF16)
    proj, log_a = _in_proj_gla(xf, row(norm_mix_g[0]), w_main, w_low, w_gate, row(gla_b_gate[0]))

    y_conv = _gated_conv(proj, conv_w[0].astype(F32), batch=batch, seq=seq, ch=conv_ch)
    q_off = 3 * conv_ch
    k_off = q_off + GLA_HEADS * dk
    v_off = k_off + GLA_HEADS * dk
    r_off = v_off + GLA_HEADS * dv
    tri = jnp.asarray(np.tril(np.ones((GLA_CHUNK, GLA_CHUNK), np.float32)), BF16)
    o_gla = _gla(proj, log_a, tri, row(gla_norm_g[0]), batch=batch, seq=seq,
                 q_off=q_off, k_off=k_off, v_off=v_off, r_off=r_off, dk=dk, dv=dv)
    w_out = hyb_w_out[0].astype(BF16)
    xf = _out_proj_residual([y_conv, o_gla], [w_out[:conv_ch], w_out[conv_ch:]], xf)
    xf = _mlp_residual(xf, row(norm_mlp_g[0]), mlp_w1[0].astype(BF16), mlp_w2[0].astype(BF16),
                       row(final_norm_g), final_norm=False)

    qkv = _qkv_proj_residue_major(xf, row(norm_mix_g[1]), attn_w_qkv[0].astype(BF16),
                                  batch=batch, seq=seq, q_scale=float(dh) ** -0.5)
    slopes = jnp.exp2(-8.0 * jnp.arange(1, ATTN_HEADS + 1, dtype=F32) / ATTN_HEADS)
    attn = _dilated_attention(qkv, slopes, batch=batch, seq=seq, dh=dh)
    xf = _out_proj_residual([attn], [attn_w_o[0].astype(BF16)], xf, residue_major=(batch, seq))
    out = _mlp_residual(xf, row(norm_mlp_g[1]), mlp_w1[1].astype(BF16), mlp_w2[1].astype(BF16),
                        row(final_norm_g), final_norm=True, residue_major=(batch, seq))
    return out.reshape(batch, seq, d)
```

```python
import functools

import jax
import jax.numpy as jnp
import numpy as np
from jax import lax
from jax.experimental import pallas as pl
from jax.experimental.pallas import tpu as pltpu

F32 = jnp.float32
BF16 = jnp.bfloat16

NORM_EPS = 1e-6
CONV_K = 3
GLA_HEADS = 4
GLA_GATE_RANK = 16
GLA_GATE_TEMP = 16.0
ATTN_HEADS = 16
DILATED_GROUPS = ((128, 1), (512, 4), (2048, 16))
DIL_MAX = 16
ATTN_STEPS = 128

LANES = 128
GLA_CHUNK = 256
GLA_SUB = 16
MASK_VALUE = -1e30
LOG2_E = 1.4426950408889634
TILES_16 = 8
BLOCKS_4 = 4
BLOCKS_1 = 8
VMEM_LIMIT_BYTES = 56 * 1024 * 1024


def _params(*semantics):
    return pltpu.CompilerParams(dimension_semantics=semantics, vmem_limit_bytes=VMEM_LIMIT_BYTES)


def _dot(a, b):
    return jnp.dot(a, b, preferred_element_type=F32)


def _dot_nt(a, b):
    return lax.dot_general(a, b, (((1,), (1,)), ((), ())), preferred_element_type=F32)


def _dot_tn(a, b):
    return lax.dot_general(a, b, (((0,), (0,)), ((), ())), preferred_element_type=F32)


def _rmsnorm(x, g):
    return x * lax.rsqrt(jnp.mean(x * x, axis=-1, keepdims=True) + NORM_EPS) * g


def _norm_rows_into(x_ref, g_ref, xn_ref, residue_major=False):
    x = x_ref[...]
    if residue_major:
        rows, d = x.shape
        x = jnp.swapaxes(x.reshape(rows // DIL_MAX, DIL_MAX, d), 0, 1).reshape(rows, d)
    xn_ref[...] = _rmsnorm(x, g_ref[...]).astype(BF16)


def _in_proj_gla_kernel(x_ref, g_ref, w_ref, wlow_ref, wgate_ref, bgate_ref, o_ref, loga_ref, xn_ref):
    j = pl.program_id(1)

    @pl.when(j == 0)
    def _():
        _norm_rows_into(x_ref, g_ref, xn_ref)
        g_low = _dot(xn_ref[...], wlow_ref[...]).astype(BF16)
        gate = _dot(g_low, wgate_ref[...]) + bgate_ref[...]
        log_sig = jnp.minimum(gate, 0.0) - jnp.log(1.0 + jnp.exp(-jnp.abs(gate)))
        loga_ref[...] = log_sig * (1.0 / GLA_GATE_TEMP)

    o_ref[...] = _dot(xn_ref[...], w_ref[...]).astype(o_ref.dtype)


def _in_proj_gla(x, gain, w, w_low, w_gate, b_gate, *, tm=1024, tn=1024):
    n, d = x.shape
    cols = w.shape[1]
    n_gate = w_gate.shape[1]
    return pl.pallas_call(
        _in_proj_gla_kernel,
        grid=(n // tm, cols // tn),
        in_specs=[
            pl.BlockSpec((tm, d), lambda i, j: (i, 0)),
            pl.BlockSpec((1, d), lambda i, j: (0, 0)),
            pl.BlockSpec((d, tn), lambda i, j: (0, j)),
            pl.BlockSpec((d, LANES), lambda i, j: (0, 0)),
            pl.BlockSpec((LANES, n_gate), lambda i, j: (0, 0)),
            pl.BlockSpec((1, n_gate), lambda i, j: (0, 0)),
        ],
        out_specs=[
            pl.BlockSpec((tm, tn), lambda i, j: (i, j)),
            pl.BlockSpec((tm, n_gate), lambda i, j: (i, 0)),
        ],
        out_shape=[
            jax.ShapeDtypeStruct((n, cols), BF16),
            jax.ShapeDtypeStruct((n, n_gate), F32),
        ],
        scratch_shapes=[pltpu.VMEM((tm, d), BF16)],
        compiler_params=_params("parallel", "arbitrary"),
        name="in_proj_gla",
    )(x, gain, w, w_low, w_gate, b_gate)


def _qkv_proj_kernel(x_ref, g_ref, w_ref, o_ref, xn_ref, *, q_tiles, q_scale):
    j = pl.program_id(1)

    @pl.when(j == 0)
    def _():
        _norm_rows_into(x_ref, g_ref, xn_ref, residue_major=True)

    acc = _dot(xn_ref[...], w_ref[...])
    scale = jnp.where(j < q_tiles, q_scale, 1.0).astype(F32)
    o_ref[...] = (acc * scale).astype(o_ref.dtype).reshape(o_ref.shape)


def _qkv_proj_residue_major(x, gain, w, *, batch, seq, q_scale, tm=1024, tn=1024):
    n, d = x.shape
    cols = w.shape[1]
    sub = seq // DIL_MAX
    per = tm // DIL_MAX
    tiles_per_seq = seq // tm
    kern = functools.partial(_qkv_proj_kernel, q_tiles=(cols // 3) // tn, q_scale=q_scale)
    out = pl.pallas_call(
        kern,
        grid=(n // tm, cols // tn),
        in_specs=[
            pl.BlockSpec((tm, d), lambda i, j: (i, 0)),
            pl.BlockSpec((1, d), lambda i, j: (0, 0)),
            pl.BlockSpec((d, tn), lambda i, j: (0, j)),
        ],
        out_specs=pl.BlockSpec((None, DIL_MAX, per, tn),
                               lambda i, j: (i // tiles_per_seq, 0, i % tiles_per_seq, j)),
        out_shape=jax.ShapeDtypeStruct((batch, DIL_MAX, sub, cols), BF16),
        scratch_shapes=[pltpu.VMEM((tm, d), BF16)],
        compiler_params=_params("parallel", "arbitrary"),
        name="qkv_proj",
    )(x, gain, w)
    return out.reshape(n, cols)


def _conv_kernel(ab_ref, ac_ref, ax_ref, w_ref, o_ref, ubuf_ref):
    tb = ab_ref.shape[0]
    pad = 8

    @pl.when(pl.program_id(1) == 0)
    def _():
        ubuf_ref[0:pad, :] = jnp.zeros((pad, ubuf_ref.shape[1]), F32)

    u = ac_ref[...].astype(F32) * ax_ref[...].astype(F32)
    ubuf_ref[pad:pad + tb, :] = u
    u1 = ubuf_ref[pad - 1:pad - 1 + tb, :]
    u2 = ubuf_ref[pad - 2:pad - 2 + tb, :]
    w = w_ref[...]
    y = ab_ref[...].astype(F32) * (w[0:1, :] * u2 + w[1:2, :] * u1 + w[2:3, :] * u)
    o_ref[...] = y.astype(o_ref.dtype)
    ubuf_ref[0:pad, :] = ubuf_ref[tb:tb + pad, :]


def _gated_conv(proj, conv_w, *, batch, seq, ch, tb=512):
    n = proj.shape[0]
    tpb = seq // tb
    return pl.pallas_call(
        _conv_kernel,
        grid=(batch, tpb),
        in_specs=[
            pl.BlockSpec((tb, ch), lambda b, t: (b * tpb + t, 0)),
            pl.BlockSpec((tb, ch), lambda b, t: (b * tpb + t, 1)),
            pl.BlockSpec((tb, ch), lambda b, t: (b * tpb + t, 2)),
            pl.BlockSpec((CONV_K, ch), lambda b, t: (0, 0)),
        ],
        out_specs=pl.BlockSpec((tb, ch), lambda b, t: (b * tpb + t, 0)),
        out_shape=jax.ShapeDtypeStruct((n, ch), BF16),
        scratch_shapes=[pltpu.VMEM((tb + 8, ch), F32)],
        compiler_params=_params("parallel", "arbitrary"),
        name="gated_conv",
    )(proj, proj, proj, conv_w)


def _gla_kernel(q_ref, k_ref, v_ref, r_ref, la_ref, tri_ref, gn_ref, o_ref, st_ref, *, q_scale):
    c_len, dk = q_ref.shape
    n_sub = c_len // GLA_SUB

    @pl.when(pl.program_id(2) == 0)
    def _():
        st_ref[...] = jnp.zeros_like(st_ref)

    la = la_ref[...]
    tri = tri_ref[...]
    la_hi = la.astype(BF16)
    rem = la - la_hi.astype(F32)
    la_mid = rem.astype(BF16)
    la_lo = (rem - la_mid.astype(F32)).astype(BF16)
    g = _dot(tri, la_hi) + _dot(tri, la_mid) + _dot(tri, la_lo)
    g_last = g[c_len - 1:c_len, :]

    q = q_ref[...].astype(F32) * q_scale
    k = k_ref[...].astype(F32)
    v = v_ref[...]

    st = st_ref[...]
    o = _dot_nt((q * jnp.exp(g)).astype(BF16), st.astype(BF16))
    k_dec = (k * jnp.exp(g_last - g)).astype(BF16)
    st_ref[...] = st * jnp.exp(g_last) + _dot_tn(v, k_dec)

    score_rows = [jnp.zeros((GLA_SUB, c_len), F32)]
    for a in range(1, n_sub):
        lo = a * GLA_SUB
        n_keys = LANES * -(-lo // LANES)
        g_ref_row = g[lo:lo + 1, :]
        q_a = (q[lo:lo + GLA_SUB] * jnp.exp(g[lo:lo + GLA_SUB] - g_ref_row)).astype(BF16)
        k_a = (k[:n_keys] * jnp.exp(jnp.minimum(g_ref_row - g[:n_keys], 0.0))).astype(BF16)
        s_a = _dot_nt(q_a, k_a)
        col = lax.broadcasted_iota(jnp.int32, s_a.shape, 1)
        s_a = jnp.where(col < lo, s_a, 0.0)
        if n_keys < c_len:
            s_a = jnp.concatenate([s_a, jnp.zeros((GLA_SUB, c_len - n_keys), F32)], axis=1)
        score_rows.append(s_a)
    scores = jnp.concatenate(score_rows, axis=0)

    row = lax.broadcasted_iota(jnp.int32, (c_len, LANES), 0)
    lane = lax.broadcasted_iota(jnp.int32, (c_len, LANES), 1)
    delta = row % LANES - lane
    band_id = jnp.where((delta >= 0) & (delta <= row % GLA_SUB), delta, -1)
    band = jnp.zeros((c_len, LANES), F32)
    for d in range(GLA_SUB):
        k_d = k if d == 0 else pltpu.roll(k, d, 0)
        g_d = g if d == 0 else pltpu.roll(g, d, 0)
        e = jnp.exp(jnp.minimum(g - g_d, 0.0))
        diag = jnp.sum(q * k_d * e, axis=-1, keepdims=True)
        band = jnp.where(band_id == d, diag, band)
    zeros = jnp.zeros((LANES, LANES), F32)
    band_rows = []
    for t in range(c_len // LANES):
        tiles = [zeros] * (c_len // LANES)
        tiles[t] = band[t * LANES:(t + 1) * LANES]
        band_rows.append(jnp.concatenate(tiles, axis=1))
    scores = scores + jnp.concatenate(band_rows, axis=0)

    o = o + _dot(scores.astype(BF16), v)
    o = _rmsnorm(o, gn_ref[...])
    r = r_ref[...].astype(F32)
    o_ref[...] = (o * (r / (1.0 + jnp.exp(-r)))).astype(o_ref.dtype)


def _gla(proj, log_a, tri, norm_g, *, batch, seq, q_off, k_off, v_off, r_off, dk, dv):
    n = proj.shape[0]
    cpb = seq // GLA_CHUNK
    row = lambda b, h, c: b * cpb + c
    kern = functools.partial(_gla_kernel, q_scale=float(dk) ** -0.5)
    return pl.pallas_call(
        kern,
        grid=(batch, GLA_HEADS, cpb),
        in_specs=[
            pl.BlockSpec((GLA_CHUNK, dk), lambda b, h, c: (row(b, h, c), q_off // dk + h)),
            pl.BlockSpec((GLA_CHUNK, dk), lambda b, h, c: (row(b, h, c), k_off // dk + h)),
            pl.BlockSpec((GLA_CHUNK, dv), lambda b, h, c: (row(b, h, c), v_off // dv + h)),
            pl.BlockSpec((GLA_CHUNK, dv), lambda b, h, c: (row(b, h, c), r_off // dv + h)),
            pl.BlockSpec((GLA_CHUNK, dk), lambda b, h, c: (row(b, h, c), h)),
            pl.BlockSpec((GLA_CHUNK, GLA_CHUNK), lambda b, h, c: (0, 0)),
            pl.BlockSpec((1, dv), lambda b, h, c: (0, 0)),
        ],
        out_specs=pl.BlockSpec((GLA_CHUNK, dv), lambda b, h, c: (row(b, h, c), h)),
        out_shape=jax.ShapeDtypeStruct((n, GLA_HEADS * dv), BF16),
        scratch_shapes=[pltpu.VMEM((dv, dk), F32)],
        compiler_params=_params("parallel", "parallel", "arbitrary"),
        name="gla",
    )(proj, proj, proj, proj, log_a, tri, norm_g)


def _out_proj_kernel(*refs, n_lhs, residue_major):
    lhs_refs, w_refs = refs[:n_lhs], refs[n_lhs:2 * n_lhs]
    x_ref, o_ref = refs[2 * n_lhs], refs[2 * n_lhs + 1]
    tm = x_ref.shape[0]

    def lhs(l_ref):
        return l_ref[...].reshape(tm, l_ref.shape[-1])

    acc = _dot(lhs(lhs_refs[0]), w_refs[0][...])
    for l_ref, w_ref in zip(lhs_refs[1:], w_refs[1:]):
        acc = acc + _dot(lhs(l_ref), w_ref[...])
    if residue_major:
        d = acc.shape[1]
        acc = jnp.swapaxes(acc.reshape(DIL_MAX, tm // DIL_MAX, d), 0, 1).reshape(tm, d)
    o_ref[...] = x_ref[...] + acc


def _out_proj_residual(lhs_list, w_list, x, *, tm=512, residue_major=None):
    n, d = x.shape
    n_lhs = len(lhs_list)
    if residue_major is None:
        in_specs = [pl.BlockSpec((tm, l.shape[1]), lambda i: (i, 0)) for l in lhs_list]
    else:
        batch, seq = residue_major
        per = tm // DIL_MAX
        tps = seq // tm
        lhs_list = [l.reshape(batch, DIL_MAX, seq // DIL_MAX, l.shape[1]) for l in lhs_list]
        in_specs = [pl.BlockSpec((None, DIL_MAX, per, l.shape[-1]),
                                 lambda i: (i // tps, 0, i % tps, 0)) for l in lhs_list]
    kern = functools.partial(_out_proj_kernel, n_lhs=n_lhs, residue_major=residue_major is not None)
    in_specs += [pl.BlockSpec(w.shape, lambda i: (0, 0)) for w in w_list]
    in_specs.append(pl.BlockSpec((tm, d), lambda i: (i, 0)))
    return pl.pallas_call(
        kern,
        grid=(n // tm,),
        in_specs=in_specs,
        out_specs=pl.BlockSpec((tm, d), lambda i: (i, 0)),
        out_shape=jax.ShapeDtypeStruct((n, d), F32),
        compiler_params=_params("parallel"),
        name="out_proj",
    )(*lhs_list, *w_list, x)


def _mlp_kernel(x_ref, g_ref, w1_ref, w2_ref, gf_ref, o_ref, xn_ref, *, final_norm):
    f = pl.program_id(1)

    @pl.when(f == 0)
    def _():
        _norm_rows_into(x_ref, g_ref, xn_ref)
        o_ref[...] = x_ref[...]

    h = _dot(xn_ref[...], w1_ref[...])
    h = jnp.square(jnp.maximum(h, 0.0)).astype(BF16)
    o_ref[...] += _dot(h, w2_ref[...])

    if final_norm:
        @pl.when(f == pl.num_programs(1) - 1)
        def _():
            o_ref[...] = _rmsnorm(o_ref[...], gf_ref[...])


def _mlp_residual(x, gain, w1, w2, final_gain, *, final_norm, tm=512, tf=1024):
    n, d = x.shape
    d_ff = w1.shape[1]
    kern = functools.partial(_mlp_kernel, final_norm=final_norm)
    return pl.pallas_call(
        kern,
        grid=(n // tm, d_ff // tf),
        in_specs=[
            pl.BlockSpec((tm, d), lambda i, f: (i, 0)),
            pl.BlockSpec((1, d), lambda i, f: (0, 0)),
            pl.BlockSpec((d, tf), lambda i, f: (0, f)),
            pl.BlockSpec((tf, d), lambda i, f: (f, 0)),
            pl.BlockSpec((1, d), lambda i, f: (0, 0)),
        ],
        out_specs=pl.BlockSpec((tm, d), lambda i, f: (i, 0)),
        out_shape=jax.ShapeDtypeStruct((n, d), F32),
        scratch_shapes=[pltpu.VMEM((tm, d), BF16)],
        compiler_params=_params("parallel", "arbitrary"),
        name="mlp",
    )(x, gain, w1, w2, final_gain)


def _attn_bias_tables(slopes, sub):
    def table(q_pos, k_pos, dilation):
        diff = q_pos[:, None] - k_pos[None, :]
        valid = jnp.asarray((diff >= 0) & (diff <= ATTN_STEPS))
        dist = jnp.asarray((dilation * diff).astype(np.float32))
        bias = -(slopes * LOG2_E)[:, None, None] * dist[None]
        return jnp.where(valid[None], bias, MASK_VALUE).astype(F32)

    q = np.arange(128)
    b16_first = table(q, q, 16)
    b16 = table(q + 128, np.arange(256), 16)

    q4 = 4 * (q % 32) + q // 32
    k = np.arange(256)
    k4 = 4 * (k % 64) + k // 64
    b4 = jnp.stack([table(q4, k4, 4), table(q4 + 128, k4, 4)], axis=1)

    q = np.arange(256)
    q1 = 16 * (q % 16) + q // 16
    k = np.arange(384)
    k1 = 16 * (k % 24) + k // 24
    b1 = jnp.stack([table(q1, k1, 1), table(q1 + 128, k1, 1)], axis=1)
    return b1, b4, b16_first, b16


def _attn_kernel(q_ref, k_ref, v_ref, b1_ref, b4_ref, b16f_ref, b16_ref, o_ref,
                 m_ref, l_ref, acc_ref, *, sub):
    def gather(ref, starts, size):
        return jnp.concatenate([ref[pl.ds(s, size), :] for s in starts], axis=0)

    def scatter(ref, starts, size, val):
        for c, s in enumerate(starts):
            ref[pl.ds(s, size), :] = val[c * size:(c + 1) * size]

    def lanes(x):
        return jnp.broadcast_to(x, (x.shape[0], LANES))

    def across_keys(x, n_keys):
        return jnp.concatenate([x] * (n_keys // LANES), axis=1)

    def weighted_values_and_sum(p, v):
        pv = _dot(p.astype(BF16), jnp.concatenate([v, jnp.ones_like(v)], axis=1))
        return pv[:, :LANES], pv[:, LANES:]

    def first_tile(q, k, v, bias):
        s = _dot_nt(q, k) + bias
        m = jnp.max(s, axis=-1, keepdims=True)
        acc, l = weighted_values_and_sum(jnp.exp2(s - m), v)
        return lanes(m), l, acc

    def next_tile(q, k, v, bias, m_old, l_old, acc_old):
        s = _dot_nt(q, k) + bias
        m_new = jnp.maximum(m_old, lanes(jnp.max(s, axis=-1, keepdims=True)))
        alpha = jnp.exp2(m_old - m_new)
        acc, l = weighted_values_and_sum(jnp.exp2(s - across_keys(m_new, s.shape[1])), v)
        return m_new, alpha * l_old + l, alpha * acc_old + acc

    def group16(i, carry):
        bias_first, bias = b16f_ref[...], b16_ref[...]
        rows, res = [], []
        for u in range(TILES_16):
            base = (TILES_16 * i + u) * sub
            for j in range(sub // ATTN_STEPS):
                q_rows = pl.ds(pl.multiple_of(base + j * ATTN_STEPS, ATTN_STEPS), ATTN_STEPS)
                if j == 0:
                    k_rows, b = q_rows, bias_first
                else:
                    k_rows = pl.ds(pl.multiple_of(base + (j - 1) * ATTN_STEPS, ATTN_STEPS),
                                   2 * ATTN_STEPS)
                    b = bias
                rows.append(q_rows)
                res.append(first_tile(q_ref[q_rows, :], k_ref[k_rows, :], v_ref[k_rows, :], b))
        for r, (m, l, acc) in zip(rows, res):
            m_ref[r, :] = m
            l_ref[r, :] = l
            acc_ref[r, :] = acc
        return carry

    lax.fori_loop(0, DIL_MAX // TILES_16, group16, 0)

    def group4(i, carry):
        starts, res = [], []
        for u in range(BLOCKS_4):
            blk = BLOCKS_4 * i + u
            bias = b4_ref[jnp.where(blk == 0, 0, 1)]
            k_blk = jnp.maximum(blk - 1, 0)
            for r4 in range(4):
                qs = [pl.multiple_of((r4 + 4 * c) * sub + 32 * blk, 32) for c in range(4)]
                ks = [pl.multiple_of((r4 + 4 * c) * sub + 32 * k_blk, 32) for c in range(4)]
                starts.append(qs)
                res.append(next_tile(
                    gather(q_ref, qs, 32), gather(k_ref, ks, 64), gather(v_ref, ks, 64), bias,
                    gather(m_ref, qs, 32), gather(l_ref, qs, 32), gather(acc_ref, qs, 32)))
        for qs, (m, l, acc) in zip(starts, res):
            scatter(m_ref, qs, 32, m)
            scatter(l_ref, qs, 32, l)
            scatter(acc_ref, qs, 32, acc)
        return carry

    lax.fori_loop(0, sub // (32 * BLOCKS_4), group4, 0)

    def group1(i, carry):
        res = []
        for u in range(BLOCKS_1):
            blk = BLOCKS_1 * i + u
            bias = b1_ref[jnp.where(blk == 0, 0, 1)]
            k_lo = jnp.maximum(16 * blk - 8, 0)
            qs = [pl.multiple_of(r * sub + 16 * blk, 16) for r in range(DIL_MAX)]
            ks = [pl.multiple_of(r * sub + k_lo, 8) for r in range(DIL_MAX)]
            _, l, acc = next_tile(
                gather(q_ref, qs, 16), gather(k_ref, ks, 24), gather(v_ref, ks, 24), bias,
                gather(m_ref, qs, 16), gather(l_ref, qs, 16), gather(acc_ref, qs, 16))
            res.append((qs, (acc / l).astype(o_ref.dtype)))
        for qs, out in res:
            scatter(o_ref, qs, 16, out)
        return carry

    lax.fori_loop(0, sub // (16 * BLOCKS_1), group1, 0)


def _dilated_attention(qkv, slopes, *, batch, seq, dh):
    n = qkv.shape[0]
    sub = seq // DIL_MAX
    b1, b4, b16_first, b16 = _attn_bias_tables(slopes, sub)
    kern = functools.partial(_attn_kernel, sub=sub)
    return pl.pallas_call(
        kern,
        grid=(batch, ATTN_HEADS),
        in_specs=[
            pl.BlockSpec((seq, dh), lambda b, h: (b, h)),
            pl.BlockSpec((seq, dh), lambda b, h: (b, ATTN_HEADS + h)),
            pl.BlockSpec((seq, dh), lambda b, h: (b, 2 * ATTN_HEADS + h)),
            pl.BlockSpec((None,) + b1.shape[1:], lambda b, h: (h, 0, 0, 0)),
            pl.BlockSpec((None,) + b4.shape[1:], lambda b, h: (h, 0, 0, 0)),
            pl.BlockSpec((None,) + b16_first.shape[1:], lambda b, h: (h, 0, 0)),
            pl.BlockSpec((None,) + b16.shape[1:], lambda b, h: (h, 0, 0)),
        ],
        out_specs=pl.BlockSpec((seq, dh), lambda b, h: (b, h)),
        out_shape=jax.ShapeDtypeStruct((n, ATTN_HEADS * dh), BF16),
        scratch_shapes=[pltpu.VMEM((seq, LANES), F32), pltpu.VMEM((seq, LANES), F32),
                        pltpu.VMEM((seq, dh), F32)],
        compiler_params=_params("parallel", "parallel"),
        name="dilated_attn",
    )(qkv, qkv, qkv, b1, b4, b16_first, b16)


def kernel(x, norm_mix_g, norm_mlp_g, final_norm_g, hyb_w_in, conv_w, gla_w_gate2, gla_b_gate,
           gla_norm_g, hyb_w_out, attn_w_qkv, attn_w_o, mlp_w1, mlp_w2):
    batch, seq, d = x.shape
    n = batch * seq
    depth = norm_mix_g.shape[0]
    conv_ch = conv_w.shape[-1]
    dv = gla_norm_g.shape[-1]
    dk = gla_w_gate2.shape[-1] // GLA_HEADS
    dh = d // ATTN_HEADS
    assert depth == 2 and seq % (DIL_MAX * ATTN_STEPS) == 0 and seq % GLA_CHUNK == 0
    assert dk == LANES and dh == LANES and dv % LANES == 0

    row = lambda v: v.reshape(1, -1).astype(F32)
    xf = x.reshape(n, d)

    main_cols = 3 * conv_ch + 2 * GLA_HEADS * dk + 2 * GLA_HEADS * dv
    w_in = hyb_w_in[0]
    w_main = w_in[:, :main_cols].astype(BF16)
    w_low = jnp.pad(w_in[:, main_cols:], ((0, 0), (0, LANES - GLA_GATE_RANK))).astype(BF16)
    w_gate = jnp.pad(gla_w_gate2[0], ((0, LANES - GLA_GATE_RANK), (0, 0))).astype(BF16)
    proj, log_a = _in_proj_gla(xf, row(norm_mix_g[0]), w_main, w_low, w_gate, row(gla_b_gate[0]))

    y_conv = _gated_conv(proj, conv_w[0].astype(F32), batch=batch, seq=seq, ch=conv_ch)
    q_off = 3 * conv_ch
    k_off = q_off + GLA_HEADS * dk
    v_off = k_off + GLA_HEADS * dk
    r_off = v_off + GLA_HEADS * dv
    tri = jnp.asarray(np.tril(np.ones((GLA_CHUNK, GLA_CHUNK), np.float32)), BF16)
    o_gla = _gla(proj, log_a, tri, row(gla_norm_g[0]), batch=batch, seq=seq,
                 q_off=q_off, k_off=k_off, v_off=v_off, r_off=r_off, dk=dk, dv=dv)
    w_out = hyb_w_out[0].astype(BF16)
    xf = _out_proj_residual([y_conv, o_gla], [w_out[:conv_ch], w_out[conv_ch:]], xf)
    xf = _mlp_residual(xf, row(norm_mlp_g[0]), mlp_w1[0].astype(BF16), mlp_w2[0].astype(BF16),
                       row(final_norm_g), final_norm=False)

    qkv = _qkv_proj_residue_major(xf, row(norm_mix_g[1]), attn_w_qkv[0].astype(BF16),
                                  batch=batch, seq=seq, q_scale=float(dh) ** -0.5 * LOG2_E)
    slopes = jnp.exp2(-8.0 * jnp.arange(1, ATTN_HEADS + 1, dtype=F32) / ATTN_HEADS)
    attn = _dilated_attention(qkv, slopes, batch=batch, seq=seq, dh=dh)
    xf = _out_proj_residual([attn], [attn_w_o[0].astype(BF16)], xf, residue_major=(batch, seq))
    out = _mlp_residual(xf, row(norm_mlp_g[1]), mlp_w1[1].astype(BF16), mlp_w2[1].astype(BF16),
                        row(final_norm_g), final_norm=True)
    return out.reshape(batch, seq, d)
```

```python
import functools

import jax
import jax.numpy as jnp
import numpy as np
from jax import lax
from jax.experimental import pallas as pl
from jax.experimental.pallas import tpu as pltpu

F32 = jnp.float32
BF16 = jnp.bfloat16

NORM_EPS = 1e-6
CONV_K = 3
GLA_HEADS = 4
GLA_GATE_RANK = 16
GLA_GATE_TEMP = 16.0
ATTN_HEADS = 16
DILATED_GROUPS = ((128, 1), (512, 4), (2048, 16))
DIL_MAX = 16
ATTN_STEPS = 128

LANES = 128
GLA_CHUNK = 256
GLA_SUB = 16
GLA_BLOCK = 64
GLA_MAX_BLOCK_DECAY = 60.0
MASK_VALUE = -1e30
LOG2_E = 1.4426950408889634
TILES_16 = 16
BLOCKS_4 = 8
BLOCKS_1 = 16
VMEM_LIMIT_BYTES = 56 * 1024 * 1024


def _params(*semantics):
    return pltpu.CompilerParams(dimension_semantics=semantics, vmem_limit_bytes=VMEM_LIMIT_BYTES)


def _dot(a, b):
    return jnp.dot(a, b, preferred_element_type=F32)


def _dot_nt(a, b):
    return lax.dot_general(a, b, (((1,), (1,)), ((), ())), preferred_element_type=F32)


def _dot_tn(a, b):
    return lax.dot_general(a, b, (((0,), (0,)), ((), ())), preferred_element_type=F32)


def _rmsnorm(x, g):
    return x * lax.rsqrt(jnp.mean(x * x, axis=-1, keepdims=True) + NORM_EPS) * g


def _norm_rows_into(x_ref, g_ref, xn_ref, residue_major=False):
    x = x_ref[...]
    if residue_major:
        rows, d = x.shape
        x = jnp.swapaxes(x.reshape(rows // DIL_MAX, DIL_MAX, d), 0, 1).reshape(rows, d)
    xn_ref[...] = _rmsnorm(x, g_ref[...]).astype(BF16)


def _in_proj_gla_kernel(x_ref, g_ref, w_ref, wlow_ref, wgate_ref, bgate_ref, o_ref, loga_ref, xn_ref):
    j = pl.program_id(1)

    @pl.when(j == 0)
    def _():
        _norm_rows_into(x_ref, g_ref, xn_ref)
        g_low = _dot(xn_ref[...], wlow_ref[...]).astype(BF16)
        gate = _dot(g_low, wgate_ref[...]) + bgate_ref[...]
        log_sig = jnp.minimum(gate, 0.0) - jnp.log(1.0 + jnp.exp(-jnp.abs(gate)))
        loga_ref[...] = log_sig * (1.0 / GLA_GATE_TEMP)

    o_ref[...] = _dot(xn_ref[...], w_ref[...]).astype(o_ref.dtype)


def _in_proj_gla(x, gain, w, w_low, w_gate, b_gate, *, tm=1024, tn=1024):
    n, d = x.shape
    cols = w.shape[1]
    n_gate = w_gate.shape[1]
    return pl.pallas_call(
        _in_proj_gla_kernel,
        grid=(n // tm, cols // tn),
        in_specs=[
            pl.BlockSpec((tm, d), lambda i, j: (i, 0)),
            pl.BlockSpec((1, d), lambda i, j: (0, 0)),
            pl.BlockSpec((d, tn), lambda i, j: (0, j)),
            pl.BlockSpec((d, LANES), lambda i, j: (0, 0)),
            pl.BlockSpec((LANES, n_gate), lambda i, j: (0, 0)),
            pl.BlockSpec((1, n_gate), lambda i, j: (0, 0)),
        ],
        out_specs=[
            pl.BlockSpec((tm, tn), lambda i, j: (i, j)),
            pl.BlockSpec((tm, n_gate), lambda i, j: (i, 0)),
        ],
        out_shape=[
            jax.ShapeDtypeStruct((n, cols), BF16),
            jax.ShapeDtypeStruct((n, n_gate), F32),
        ],
        scratch_shapes=[pltpu.VMEM((tm, d), BF16)],
        compiler_params=_params("parallel", "arbitrary"),
        name="in_proj_gla",
    )(x, gain, w, w_low, w_gate, b_gate)


def _qkv_proj_kernel(x_ref, g_ref, w_ref, o_ref, xn_ref, *, q_tiles, q_scale):
    j = pl.program_id(1)

    @pl.when(j == 0)
    def _():
        _norm_rows_into(x_ref, g_ref, xn_ref, residue_major=True)

    acc = _dot(xn_ref[...], w_ref[...])
    scale = jnp.where(j < q_tiles, q_scale, 1.0).astype(F32)
    o_ref[...] = (acc * scale).astype(o_ref.dtype).reshape(o_ref.shape)


def _qkv_proj_residue_major(x, gain, w, *, batch, seq, q_scale, tm=1024, tn=1024):
    n, d = x.shape
    cols = w.shape[1]
    sub = seq // DIL_MAX
    per = tm // DIL_MAX
    tiles_per_seq = seq // tm
    kern = functools.partial(_qkv_proj_kernel, q_tiles=(cols // 3) // tn, q_scale=q_scale)
    out = pl.pallas_call(
        kern,
        grid=(n // tm, cols // tn),
        in_specs=[
            pl.BlockSpec((tm, d), lambda i, j: (i, 0)),
            pl.BlockSpec((1, d), lambda i, j: (0, 0)),
            pl.BlockSpec((d, tn), lambda i, j: (0, j)),
        ],
        out_specs=pl.BlockSpec((None, DIL_MAX, per, tn),
                               lambda i, j: (i // tiles_per_seq, 0, i % tiles_per_seq, j)),
        out_shape=jax.ShapeDtypeStruct((batch, DIL_MAX, sub, cols), BF16),
        scratch_shapes=[pltpu.VMEM((tm, d), BF16)],
        compiler_params=_params("parallel", "arbitrary"),
        name="qkv_proj",
    )(x, gain, w)
    return out.reshape(n, cols)


def _conv_kernel(ab_ref, ac_ref, ax_ref, w_ref, o_ref, ubuf_ref):
    tb = ab_ref.shape[0]
    pad = 8

    @pl.when(pl.program_id(1) == 0)
    def _():
        ubuf_ref[0:pad, :] = jnp.zeros((pad, ubuf_ref.shape[1]), F32)

    u = ac_ref[...].astype(F32) * ax_ref[...].astype(F32)
    ubuf_ref[pad:pad + tb, :] = u
    u1 = ubuf_ref[pad - 1:pad - 1 + tb, :]
    u2 = ubuf_ref[pad - 2:pad - 2 + tb, :]
    w = w_ref[...]
    y = ab_ref[...].astype(F32) * (w[0:1, :] * u2 + w[1:2, :] * u1 + w[2:3, :] * u)
    o_ref[...] = y.astype(o_ref.dtype)
    ubuf_ref[0:pad, :] = ubuf_ref[tb:tb + pad, :]


def _gated_conv(proj, conv_w, *, batch, seq, ch, tb=512):
    n = proj.shape[0]
    tpb = seq // tb
    return pl.pallas_call(
        _conv_kernel,
        grid=(batch, tpb),
        in_specs=[
            pl.BlockSpec((tb, ch), lambda b, t: (b * tpb + t, 0)),
            pl.BlockSpec((tb, ch), lambda b, t: (b * tpb + t, 1)),
            pl.BlockSpec((tb, ch), lambda b, t: (b * tpb + t, 2)),
            pl.BlockSpec((CONV_K, ch), lambda b, t: (0, 0)),
        ],
        out_specs=pl.BlockSpec((tb, ch), lambda b, t: (b * tpb + t, 0)),
        out_shape=jax.ShapeDtypeStruct((n, ch), BF16),
        scratch_shapes=[pltpu.VMEM((tb + 8, ch), F32)],
        compiler_params=_params("parallel", "arbitrary"),
        name="gated_conv",
    )(proj, proj, proj, conv_w)


def _gla_scores_factored(q, k, g):
    c_len = q.shape[0]
    score_rows = []
    for a in range(c_len // GLA_BLOCK):
        lo, hi = a * GLA_BLOCK, (a + 1) * GLA_BLOCK
        n_keys = LANES * -(-hi // LANES)
        g_first = g[lo:lo + 1, :]
        q_a = (q[lo:hi] * jnp.exp(g[lo:hi] - g_first)).astype(BF16)
        k_a = (k[:n_keys] * jnp.exp(jnp.minimum(g_first - g[:n_keys], GLA_MAX_BLOCK_DECAY)))
        s_a = _dot_nt(q_a, k_a.astype(BF16))
        row = lax.broadcasted_iota(jnp.int32, s_a.shape, 0) + lo
        col = lax.broadcasted_iota(jnp.int32, s_a.shape, 1)
        s_a = jnp.where(col <= row, s_a, 0.0)
        if n_keys < c_len:
            s_a = jnp.concatenate([s_a, jnp.zeros((GLA_BLOCK, c_len - n_keys), F32)], axis=1)
        score_rows.append(s_a)
    return jnp.concatenate(score_rows, axis=0)


def _gla_scores_pairwise(q, k, g):
    c_len = q.shape[0]
    n_sub = c_len // GLA_SUB
    score_rows = [jnp.zeros((GLA_SUB, c_len), F32)]
    for a in range(1, n_sub):
        lo = a * GLA_SUB
        n_keys = LANES * -(-lo // LANES)
        g_ref_row = g[lo:lo + 1, :]
        q_a = (q[lo:lo + GLA_SUB] * jnp.exp(g[lo:lo + GLA_SUB] - g_ref_row)).astype(BF16)
        k_a = (k[:n_keys] * jnp.exp(jnp.minimum(g_ref_row - g[:n_keys], 0.0))).astype(BF16)
        s_a = _dot_nt(q_a, k_a)
        col = lax.broadcasted_iota(jnp.int32, s_a.shape, 1)
        s_a = jnp.where(col < lo, s_a, 0.0)
        if n_keys < c_len:
            s_a = jnp.concatenate([s_a, jnp.zeros((GLA_SUB, c_len - n_keys), F32)], axis=1)
        score_rows.append(s_a)
    scores = jnp.concatenate(score_rows, axis=0)

    row = lax.broadcasted_iota(jnp.int32, (c_len, LANES), 0)
    lane = lax.broadcasted_iota(jnp.int32, (c_len, LANES), 1)
    delta = row % LANES - lane
    band_id = jnp.where((delta >= 0) & (delta <= row % GLA_SUB), delta, -1)
    band = jnp.zeros((c_len, LANES), F32)
    for d in range(GLA_SUB):
        k_d = k if d == 0 else pltpu.roll(k, d, 0)
        g_d = g if d == 0 else pltpu.roll(g, d, 0)
        e = jnp.exp(jnp.minimum(g - g_d, 0.0))
        diag = jnp.sum(q * k_d * e, axis=-1, keepdims=True)
        band = jnp.where(band_id == d, diag, band)
    zeros = jnp.zeros((LANES, LANES), F32)
    band_rows = []
    for t in range(c_len // LANES):
        tiles = [zeros] * (c_len // LANES)
        tiles[t] = band[t * LANES:(t + 1) * LANES]
        band_rows.append(jnp.concatenate(tiles, axis=1))
    return scores + jnp.concatenate(band_rows, axis=0)


def _gla_kernel(q_ref, k_ref, v_ref, r_ref, la_ref, tri_ref, gn_ref, o_ref, st_ref, g_ref, *,
                q_scale, dk, dv):
    c_len = q_ref.shape[0]

    @pl.when(pl.program_id(1) == 0)
    def _():
        st_ref[...] = jnp.zeros_like(st_ref)

    la = la_ref[...]
    tri = tri_ref[...]
    la_hi = la.astype(BF16)
    rem = la - la_hi.astype(F32)
    la_mid = rem.astype(BF16)
    la_lo = (rem - la_mid.astype(F32)).astype(BF16)
    g_all = _dot(tri, la_hi) + _dot(tri, la_mid) + _dot(tri, la_lo)
    g_ref[...] = g_all
    block_decay = [g_all[lo:lo + 1, :] - g_all[lo + GLA_BLOCK - 1:lo + GLA_BLOCK, :]
                   for lo in range(0, c_len, GLA_BLOCK)]
    worst_decay = jnp.max(jnp.concatenate(block_decay, axis=0))

    def all_heads(scores_fn):
        for h in range(GLA_HEADS):
            g = g_ref[:, h * dk:(h + 1) * dk]
            g_last = g[c_len - 1:c_len, :]
            q = q_ref[:, h * dk:(h + 1) * dk].astype(F32) * q_scale
            k = k_ref[:, h * dk:(h + 1) * dk].astype(F32)
            v = v_ref[:, h * dv:(h + 1) * dv]

            st = st_ref[h]
            o = _dot_nt((q * jnp.exp(g)).astype(BF16), st.astype(BF16))
            k_dec = (k * jnp.exp(g_last - g)).astype(BF16)
            st_ref[h] = st * jnp.exp(g_last) + _dot_tn(v, k_dec)

            o = o + _dot(scores_fn(q, k, g).astype(BF16), v)
            o = _rmsnorm(o, gn_ref[...])
            r = r_ref[:, h * dv:(h + 1) * dv].astype(F32)
            o_ref[:, h * dv:(h + 1) * dv] = (o * (r / (1.0 + jnp.exp(-r)))).astype(o_ref.dtype)

    @pl.when(worst_decay <= GLA_MAX_BLOCK_DECAY)
    def _():
        all_heads(_gla_scores_factored)

    @pl.when(worst_decay > GLA_MAX_BLOCK_DECAY)
    def _():
        all_heads(_gla_scores_pairwise)


def _gla(proj, log_a, tri, norm_g, *, batch, seq, q_off, k_off, v_off, r_off, dk, dv):
    n = proj.shape[0]
    cpb = seq // GLA_CHUNK
    qk_w, vr_w = GLA_HEADS * dk, GLA_HEADS * dv
    row = lambda b, c: b * cpb + c
    kern = functools.partial(_gla_kernel, q_scale=float(dk) ** -0.5, dk=dk, dv=dv)
    return pl.pallas_call(
        kern,
        grid=(batch, cpb),
        in_specs=[
            pl.BlockSpec((GLA_CHUNK, qk_w), lambda b, c: (row(b, c), q_off // qk_w)),
            pl.BlockSpec((GLA_CHUNK, qk_w), lambda b, c: (row(b, c), k_off // qk_w)),
            pl.BlockSpec((GLA_CHUNK, vr_w), lambda b, c: (row(b, c), v_off // vr_w)),
            pl.BlockSpec((GLA_CHUNK, vr_w), lambda b, c: (row(b, c), r_off // vr_w)),
            pl.BlockSpec((GLA_CHUNK, qk_w), lambda b, c: (row(b, c), 0)),
            pl.BlockSpec((GLA_CHUNK, GLA_CHUNK), lambda b, c: (0, 0)),
            pl.BlockSpec((1, dv), lambda b, c: (0, 0)),
        ],
        out_specs=pl.BlockSpec((GLA_CHUNK, vr_w), lambda b, c: (row(b, c), 0)),
        out_shape=jax.ShapeDtypeStruct((n, vr_w), BF16),
        scratch_shapes=[pltpu.VMEM((GLA_HEADS, dv, dk), F32), pltpu.VMEM((GLA_CHUNK, qk_w), F32)],
        compiler_params=_params("parallel", "arbitrary"),
        name="gla",
    )(proj, proj, proj, proj, log_a, tri, norm_g)


def _out_proj_kernel(*refs, n_lhs, residue_major):
    lhs_refs, w_refs = refs[:n_lhs], refs[n_lhs:2 * n_lhs]
    x_ref, o_ref = refs[2 * n_lhs], refs[2 * n_lhs + 1]
    tm = x_ref.shape[0]

    def lhs(l_ref):
        return l_ref[...].reshape(tm, l_ref.shape[-1])

    acc = _dot(lhs(lhs_refs[0]), w_refs[0][...])
    for l_ref, w_ref in zip(lhs_refs[1:], w_refs[1:]):
        acc = acc + _dot(lhs(l_ref), w_ref[...])
    if residue_major:
        d = acc.shape[1]
        acc = jnp.swapaxes(acc.reshape(DIL_MAX, tm // DIL_MAX, d), 0, 1).reshape(tm, d)
    o_ref[...] = x_ref[...] + acc


def _out_proj_residual(lhs_list, w_list, x, *, tm=512, residue_major=None):
    n, d = x.shape
    n_lhs = len(lhs_list)
    if residue_major is None:
        in_specs = [pl.BlockSpec((tm, l.shape[1]), lambda i: (i, 0)) for l in lhs_list]
    else:
        batch, seq = residue_major
        per = tm // DIL_MAX
        tps = seq // tm
        lhs_list = [l.reshape(batch, DIL_MAX, seq // DIL_MAX, l.shape[1]) for l in lhs_list]
        in_specs = [pl.BlockSpec((None, DIL_MAX, per, l.shape[-1]),
                                 lambda i: (i // tps, 0, i % tps, 0)) for l in lhs_list]
    kern = functools.partial(_out_proj_kernel, n_lhs=n_lhs, residue_major=residue_major is not None)
    in_specs += [pl.BlockSpec(w.shape, lambda i: (0, 0)) for w in w_list]
    in_specs.append(pl.BlockSpec((tm, d), lambda i: (i, 0)))
    return pl.pallas_call(
        kern,
        grid=(n // tm,),
        in_specs=in_specs,
        out_specs=pl.BlockSpec((tm, d), lambda i: (i, 0)),
        out_shape=jax.ShapeDtypeStruct((n, d), F32),
        compiler_params=_params("parallel"),
        name="out_proj",
    )(*lhs_list, *w_list, x)


def _mlp_kernel(x_ref, g_ref, w1_ref, w2_ref, gf_ref, o_ref, xn_ref, *, final_norm):
    f = pl.program_id(1)

    @pl.when(f == 0)
    def _():
        _norm_rows_into(x_ref, g_ref, xn_ref)
        o_ref[...] = x_ref[...]

    h = _dot(xn_ref[...], w1_ref[...])
    h = jnp.square(jnp.maximum(h, 0.0)).astype(BF16)
    o_ref[...] += _dot(h, w2_ref[...])

    if final_norm:
        @pl.when(f == pl.num_programs(1) - 1)
        def _():
            o_ref[...] = _rmsnorm(o_ref[...], gf_ref[...])


def _mlp_residual(x, gain, w1, w2, final_gain, *, final_norm, tm=512, tf=1024):
    n, d = x.shape
    d_ff = w1.shape[1]
    kern = functools.partial(_mlp_kernel, final_norm=final_norm)
    return pl.pallas_call(
        kern,
        grid=(n // tm, d_ff // tf),
        in_specs=[
            pl.BlockSpec((tm, d), lambda i, f: (i, 0)),
            pl.BlockSpec((1, d), lambda i, f: (0, 0)),
            pl.BlockSpec((d, tf), lambda i, f: (0, f)),
            pl.BlockSpec((tf, d), lambda i, f: (f, 0)),
            pl.BlockSpec((1, d), lambda i, f: (0, 0)),
        ],
        out_specs=pl.BlockSpec((tm, d), lambda i, f: (i, 0)),
        out_shape=jax.ShapeDtypeStruct((n, d), F32),
        scratch_shapes=[pltpu.VMEM((tm, d), BF16)],
        compiler_params=_params("parallel", "arbitrary"),
        name="mlp",
    )(x, gain, w1, w2, final_gain)


def _attn_bias_tables(slopes, sub):
    def table(q_pos, k_pos, dilation):
        diff = q_pos[:, None] - k_pos[None, :]
        valid = jnp.asarray((diff >= 0) & (diff <= ATTN_STEPS))
        dist = jnp.asarray((dilation * diff).astype(np.float32))
        bias = -(slopes * LOG2_E)[:, None, None] * dist[None]
        return jnp.where(valid[None], bias, MASK_VALUE).astype(F32)

    q = np.arange(128)
    b16_first = table(q, q, 16)
    b16 = table(q + 128, np.arange(256), 16)

    q4 = 4 * (q % 32) + q // 32
    k = np.arange(256)
    k4 = 4 * (k % 64) + k // 64
    b4 = jnp.stack([table(q4, k4, 4), table(q4 + 128, k4, 4)], axis=1)

    q = np.arange(256)
    q1 = 16 * (q % 16) + q // 16
    k = np.arange(384)
    k1 = 16 * (k % 24) + k // 24
    b1 = jnp.stack([table(q1, k1, 1), table(q1 + 128, k1, 1)], axis=1)
    return b1, b4, b16_first, b16


def _attn_kernel(q_ref, k_ref, v_ref, b1_ref, b4_ref, b16f_ref, b16_ref, o_ref,
                 m_ref, l_ref, acc_ref, *, sub):
    def gather(ref, starts, size):
        return jnp.concatenate([ref[pl.ds(s, size), :] for s in starts], axis=0)

    def scatter(ref, starts, size, val):
        for c, s in enumerate(starts):
            ref[pl.ds(s, size), :] = val[c * size:(c + 1) * size]

    def lanes(x):
        return jnp.broadcast_to(x, (x.shape[0], LANES))

    def across_keys(x, n_keys):
        return jnp.concatenate([x] * (n_keys // LANES), axis=1)

    def weighted_values_and_sum(p, v):
        pv = _dot(p.astype(BF16), jnp.concatenate([v, jnp.ones_like(v)], axis=1))
        return pv[:, :LANES], pv[:, LANES:]

    def first_tile(q, k, v, bias):
        s = _dot_nt(q, k) + bias
        m = jnp.max(s, axis=-1, keepdims=True)
        acc, l = weighted_values_and_sum(jnp.exp2(s - m), v)
        return lanes(m), l, acc

    def next_tile(q, k, v, bias, m_old, l_old, acc_old):
        s = _dot_nt(q, k) + bias
        m_new = jnp.maximum(m_old, lanes(jnp.max(s, axis=-1, keepdims=True)))
        alpha = jnp.exp2(m_old - m_new)
        acc, l = weighted_values_and_sum(jnp.exp2(s - across_keys(m_new, s.shape[1])), v)
        return m_new, alpha * l_old + l, alpha * acc_old + acc

    def group16(i, carry):
        bias_first, bias = b16f_ref[...], b16_ref[...]
        rows, res = [], []
        for u in range(TILES_16):
            base = (TILES_16 * i + u) * sub
            for j in range(sub // ATTN_STEPS):
                q_rows = pl.ds(pl.multiple_of(base + j * ATTN_STEPS, ATTN_STEPS), ATTN_STEPS)
                if j == 0:
                    k_rows, b = q_rows, bias_first
                else:
                    k_rows = pl.ds(pl.multiple_of(base + (j - 1) * ATTN_STEPS, ATTN_STEPS),
                                   2 * ATTN_STEPS)
                    b = bias
                rows.append(q_rows)
                res.append(first_tile(q_ref[q_rows, :], k_ref[k_rows, :], v_ref[k_rows, :], b))
        for r, (m, l, acc) in zip(rows, res):
            m_ref[r, :] = m
            l_ref[r, :] = l
            acc_ref[r, :] = acc
        return carry

    lax.fori_loop(0, DIL_MAX // TILES_16, group16, 0)

    def group4(i, carry):
        starts, res = [], []
        for u in range(BLOCKS_4):
            blk = BLOCKS_4 * i + u
            bias = b4_ref[jnp.where(blk == 0, 0, 1)]
            k_blk = jnp.maximum(blk - 1, 0)
            for r4 in range(4):
                qs = [pl.multiple_of((r4 + 4 * c) * sub + 32 * blk, 32) for c in range(4)]
                ks = [pl.multiple_of((r4 + 4 * c) * sub + 32 * k_blk, 32) for c in range(4)]
                starts.append(qs)
                res.append(next_tile(
                    gather(q_ref, qs, 32), gather(k_ref, ks, 64), gather(v_ref, ks, 64), bias,
                    gather(m_ref, qs, 32), gather(l_ref, qs, 32), gather(acc_ref, qs, 32)))
        for qs, (m, l, acc) in zip(starts, res):
            scatter(m_ref, qs, 32, m)
            scatter(l_ref, qs, 32, l)
            scatter(acc_ref, qs, 32, acc)
        return carry

    lax.fori_loop(0, sub // (32 * BLOCKS_4), group4, 0)

    def group1(i, carry):
        res = []
        for u in range(BLOCKS_1):
            blk = BLOCKS_1 * i + u
            bias = b1_ref[jnp.where(blk == 0, 0, 1)]
            k_lo = jnp.maximum(16 * blk - 8, 0)
            qs = [pl.multiple_of(r * sub + 16 * blk, 16) for r in range(DIL_MAX)]
            ks = [pl.multiple_of(r * sub + k_lo, 8) for r in range(DIL_MAX)]
            _, l, acc = next_tile(
                gather(q_ref, qs, 16), gather(k_ref, ks, 24), gather(v_ref, ks, 24), bias,
                gather(m_ref, qs, 16), gather(l_ref, qs, 16), gather(acc_ref, qs, 16))
            res.append((qs, (acc / l).astype(o_ref.dtype)))
        for qs, out in res:
            scatter(o_ref, qs, 16, out)
        return carry

    lax.fori_loop(0, sub // (16 * BLOCKS_1), group1, 0)


def _dilated_attention(qkv, slopes, *, batch, seq, dh):
    n = qkv.shape[0]
    sub = seq // DIL_MAX
    b1, b4, b16_first, b16 = _attn_bias_tables(slopes, sub)
    kern = functools.partial(_attn_kernel, sub=sub)
    return pl.pallas_call(
        kern,
        grid=(batch, ATTN_HEADS),
        in_specs=[
            pl.BlockSpec((seq, dh), lambda b, h: (b, h)),
            pl.BlockSpec((seq, dh), lambda b, h: (b, ATTN_HEADS + h)),
            pl.BlockSpec((seq, dh), lambda b, h: (b, 2 * ATTN_HEADS + h)),
            pl.BlockSpec((None,) + b1.shape[1:], lambda b, h: (h, 0, 0, 0)),
            pl.BlockSpec((None,) + b4.shape[1:], lambda b, h: (h, 0, 0, 0)),
            pl.BlockSpec((None,) + b16_first.shape[1:], lambda b, h: (h, 0, 0)),
            pl.BlockSpec((None,) + b16.shape[1:], lambda b, h: (h, 0, 0)),
        ],
        out_specs=pl.BlockSpec((seq, dh), lambda b, h: (b, h)),
        out_shape=jax.ShapeDtypeStruct((n, ATTN_HEADS * dh), BF16),
        scratch_shapes=[pltpu.VMEM((seq, LANES), F32), pltpu.VMEM((seq, LANES), F32),
                        pltpu.VMEM((seq, dh), F32)],
        compiler_params=_params("parallel", "parallel"),
        name="dilated_attn",
    )(qkv, qkv, qkv, b1, b4, b16_first, b16)


def kernel(x, norm_mix_g, norm_mlp_g, final_norm_g, hyb_w_in, conv_w, gla_w_gate2, gla_b_gate,
           gla_norm_g, hyb_w_out, attn_w_qkv, attn_w_o, mlp_w1, mlp_w2):
    batch, seq, d = x.shape
    n = batch * seq
    depth = norm_mix_g.shape[0]
    conv_ch = conv_w.shape[-1]
    dv = gla_norm_g.shape[-1]
    dk = gla_w_gate2.shape[-1] // GLA_HEADS
    dh = d // ATTN_HEADS
    assert depth == 2 and seq % (DIL_MAX * ATTN_STEPS) == 0 and seq % GLA_CHUNK == 0
    assert dk == LANES and dh == LANES and dv % LANES == 0

    row = lambda v: v.reshape(1, -1).astype(F32)
    xf = x.reshape(n, d)

    main_cols = 3 * conv_ch + 2 * GLA_HEADS * dk + 2 * GLA_HEADS * dv
    w_in = hyb_w_in[0]
    w_main = w_in[:, :main_cols].astype(BF16)
    w_low = jnp.pad(w_in[:, main_cols:], ((0, 0), (0, LANES - GLA_GATE_RANK))).astype(BF16)
    w_gate = jnp.pad(gla_w_gate2[0], ((0, LANES - GLA_GATE_RANK), (0, 0))).astype(BF16)
    proj, log_a = _in_proj_gla(xf, row(norm_mix_g[0]), w_main, w_low, w_gate, row(gla_b_gate[0]))

    y_conv = _gated_conv(proj, conv_w[0].astype(F32), batch=batch, seq=seq, ch=conv_ch)
    q_off = 3 * conv_ch
    k_off = q_off + GLA_HEADS * dk
    v_off = k_off + GLA_HEADS * dk
    r_off = v_off + GLA_HEADS * dv
    tri = jnp.asarray(np.tril(np.ones((GLA_CHUNK, GLA_CHUNK), np.float32)), BF16)
    o_gla = _gla(proj, log_a, tri, row(gla_norm_g[0]), batch=batch, seq=seq,
                 q_off=q_off, k_off=k_off, v_off=v_off, r_off=r_off, dk=dk, dv=dv)
    w_out = hyb_w_out[0].astype(BF16)
    xf = _out_proj_residual([y_conv, o_gla], [w_out[:conv_ch], w_out[conv_ch:]], xf)
    xf = _mlp_residual(xf, row(norm_mlp_g[0]), mlp_w1[0].astype(BF16), mlp_w2[0].astype(BF16),
                       row(final_norm_g), final_norm=False)

    qkv = _qkv_proj_residue_major(xf, row(norm_mix_g[1]), attn_w_qkv[0].astype(BF16),
                                  batch=batch, seq=seq, q_scale=float(dh) ** -0.5 * LOG2_E)
    slopes = jnp.exp2(-8.0 * jnp.arange(1, ATTN_HEADS + 1, dtype=F32) / ATTN_HEADS)
    attn = _dilated_attention(qkv, slopes, batch=batch, seq=seq, dh=dh)
    xf = _out_proj_residual([attn], [attn_w_o[0].astype(BF16)], xf, residue_major=(batch, seq))
    out = _mlp_residual(xf, row(norm_mlp_g[1]), mlp_w1[1].astype(BF16), mlp_w2[1].astype(BF16),
                        row(final_norm_g), final_norm=True)
    return out.reshape(batch, seq, d)
```

```python
import functools

import jax
import jax.numpy as jnp
import numpy as np
from jax import lax
from jax.experimental import pallas as pl
from jax.experimental.pallas import tpu as pltpu

F32 = jnp.float32
BF16 = jnp.bfloat16

NORM_EPS = 1e-6
CONV_K = 3
GLA_HEADS = 4
GLA_GATE_RANK = 16
GLA_GATE_TEMP = 16.0
ATTN_HEADS = 16
DILATED_GROUPS = ((128, 1), (512, 4), (2048, 16))
DIL_MAX = 16
ATTN_STEPS = 128

LANES = 128
GLA_CHUNK = 256
GLA_SUB = 16
GLA_BLOCK = 64
GLA_MAX_BLOCK_DECAY = 60.0
MASK_VALUE = -1e30
LOG2_E = 1.4426950408889634
TILES_16 = 16
BLOCKS_4 = 8
BLOCKS_1 = 16
VMEM_LIMIT_BYTES = 56 * 1024 * 1024


def _params(*semantics):
    return pltpu.CompilerParams(dimension_semantics=semantics, vmem_limit_bytes=VMEM_LIMIT_BYTES)


def _dot(a, b):
    return jnp.dot(a, b, preferred_element_type=F32)


def _dot_nt(a, b):
    return lax.dot_general(a, b, (((1,), (1,)), ((), ())), preferred_element_type=F32)


def _dot_tn(a, b):
    return lax.dot_general(a, b, (((0,), (0,)), ((), ())), preferred_element_type=F32)


def _rmsnorm(x, g):
    return x * lax.rsqrt(jnp.mean(x * x, axis=-1, keepdims=True) + NORM_EPS) * g


def _tile_of_trip(i):
    return jnp.maximum(i - 1, 0)


def _chunk_index(i, j, n_tiles, chunks):
    return jnp.minimum(i, n_tiles - 1) * chunks + j


def _in_proj_gla_kernel(xc_ref, g_ref, w_ref, wlow_ref, wgate_ref, bgate_ref, o_ref, loga_ref,
                        xn_ref):
    i, j = pl.program_id(0), pl.program_id(1)
    ch = xc_ref.shape[0]
    fill = i % 2
    rows = pl.ds(pl.multiple_of(j * ch, ch), ch)

    def norm_and_gate_chunk():
        xn = _rmsnorm(xc_ref[...], g_ref[...]).astype(BF16)
        xn_ref[fill, rows, :] = xn
        g_low = _dot(xn, wlow_ref[...]).astype(BF16)
        gate = _dot(g_low, wgate_ref[...]) + bgate_ref[...]
        log_sig = jnp.minimum(gate, 0.0) - jnp.log(1.0 + jnp.exp(-jnp.abs(gate)))
        loga_ref[rows, :] = log_sig * (1.0 / GLA_GATE_TEMP)

    @pl.when(i == 0)
    def _():
        norm_and_gate_chunk()

    @pl.when(i > 0)
    def _():
        norm_and_gate_chunk()
        o_ref[...] = _dot(xn_ref[1 - fill], w_ref[...]).astype(o_ref.dtype)


def _in_proj_gla(x, gain, w, w_low, w_gate, b_gate, *, tm=1024, tn=1536):
    n, d = x.shape
    cols = w.shape[1]
    n_gate = w_gate.shape[1]
    n_tiles, steps = n // tm, cols // tn
    ch = tm // steps
    return pl.pallas_call(
        _in_proj_gla_kernel,
        grid=(n_tiles + 1, steps),
        in_specs=[
            pl.BlockSpec((ch, d), lambda i, j: (_chunk_index(i, j, n_tiles, steps), 0)),
            pl.BlockSpec((1, d), lambda i, j: (0, 0)),
            pl.BlockSpec((d, tn), lambda i, j: (0, j)),
            pl.BlockSpec((d, LANES), lambda i, j: (0, 0)),
            pl.BlockSpec((LANES, n_gate), lambda i, j: (0, 0)),
            pl.BlockSpec((1, n_gate), lambda i, j: (0, 0)),
        ],
        out_specs=[
            pl.BlockSpec((tm, tn), lambda i, j: (_tile_of_trip(i), j * jnp.minimum(i, 1))),
            pl.BlockSpec((tm, n_gate), lambda i, j: (jnp.minimum(i, n_tiles - 1), 0)),
        ],
        out_shape=[
            jax.ShapeDtypeStruct((n, cols), BF16),
            jax.ShapeDtypeStruct((n, n_gate), F32),
        ],
        scratch_shapes=[pltpu.VMEM((2, tm, d), BF16)],
        compiler_params=_params("arbitrary", "arbitrary"),
        name="in_proj_gla",
    )(x, gain, w, w_low, w_gate, b_gate)


def _qkv_proj_kernel(xc_ref, g_ref, w_ref, scale_ref, o_ref, xn_ref):
    i, j = pl.program_id(0), pl.program_id(1)
    ch, d = xc_ref.shape
    per = ch // DIL_MAX
    fill = i % 2

    def norm_chunk():
        x = jnp.swapaxes(xc_ref[...].reshape(per, DIL_MAX, d), 0, 1).reshape(ch, d)
        xn = _rmsnorm(x, g_ref[...]).astype(BF16)
        xn_ref[fill, :, pl.ds(pl.multiple_of(j * per, per), per), :] = xn.reshape(DIL_MAX, per, d)

    @pl.when(i == 0)
    def _():
        norm_chunk()

    @pl.when(i > 0)
    def _():
        norm_chunk()
        xn = xn_ref[1 - fill]
        acc = _dot(xn.reshape(xn.shape[0] * xn.shape[1], d), w_ref[...]) * scale_ref[...]
        o_ref[...] = acc.astype(o_ref.dtype).reshape(o_ref.shape)


def _qkv_proj_residue_major(x, gain, w, col_scale, *, batch, seq, tm=1024, tn=1536):
    n, d = x.shape
    cols = w.shape[1]
    sub = seq // DIL_MAX
    per = tm // DIL_MAX
    tiles_per_seq = seq // tm
    n_tiles, steps = n // tm, cols // tn
    ch = tm // steps
    assert ch % (DIL_MAX * 16) == 0

    def out_index(i, j):
        t = _tile_of_trip(i)
        return (t // tiles_per_seq, 0, t % tiles_per_seq, j * jnp.minimum(i, 1))

    out = pl.pallas_call(
        _qkv_proj_kernel,
        grid=(n_tiles + 1, steps),
        in_specs=[
            pl.BlockSpec((ch, d), lambda i, j: (_chunk_index(i, j, n_tiles, steps), 0)),
            pl.BlockSpec((1, d), lambda i, j: (0, 0)),
            pl.BlockSpec((d, tn), lambda i, j: (0, j)),
            pl.BlockSpec((1, tn), lambda i, j: (0, j)),
        ],
        out_specs=pl.BlockSpec((None, DIL_MAX, per, tn), out_index),
        out_shape=jax.ShapeDtypeStruct((batch, DIL_MAX, sub, cols), BF16),
        scratch_shapes=[pltpu.VMEM((2, DIL_MAX, per, d), BF16)],
        compiler_params=_params("arbitrary", "arbitrary"),
        name="qkv_proj",
    )(x, gain, w, col_scale)
    return out.reshape(n, cols)


def _conv_out_proj_kernel(ab_ref, ac_ref, ax_ref, cw_ref, gla_ref, wa_ref, wb_ref, x_ref, o_ref,
                          ubuf_ref, *, tiles_per_seq):
    tb = ab_ref.shape[0]
    pad = 8

    @pl.when(pl.program_id(0) % tiles_per_seq == 0)
    def _():
        ubuf_ref[0:pad, :] = jnp.zeros((pad, ubuf_ref.shape[1]), F32)

    acc = x_ref[...] + _dot(gla_ref[...], wb_ref[...])
    u = ac_ref[...].astype(F32) * ax_ref[...].astype(F32)
    ubuf_ref[pad:pad + tb, :] = u
    u1 = ubuf_ref[pad - 1:pad - 1 + tb, :]
    u2 = ubuf_ref[pad - 2:pad - 2 + tb, :]
    cw = cw_ref[...]
    y = ab_ref[...].astype(F32) * (cw[0:1, :] * u2 + cw[1:2, :] * u1 + cw[2:3, :] * u)
    o_ref[...] = acc + _dot(y.astype(BF16), wa_ref[...])
    ubuf_ref[0:pad, :] = ubuf_ref[tb:tb + pad, :]


def _conv_out_proj_residual(proj, conv_w, o_gla, w_conv, w_gla, x, *, seq, tm=512):
    n, d = x.shape
    ch = conv_w.shape[1]
    kern = functools.partial(_conv_out_proj_kernel, tiles_per_seq=seq // tm)
    return pl.pallas_call(
        kern,
        grid=(n // tm,),
        in_specs=[
            pl.BlockSpec((tm, ch), lambda i: (i, 0)),
            pl.BlockSpec((tm, ch), lambda i: (i, 1)),
            pl.BlockSpec((tm, ch), lambda i: (i, 2)),
            pl.BlockSpec((CONV_K, ch), lambda i: (0, 0)),
            pl.BlockSpec((tm, o_gla.shape[1]), lambda i: (i, 0)),
            pl.BlockSpec(w_conv.shape, lambda i: (0, 0)),
            pl.BlockSpec(w_gla.shape, lambda i: (0, 0)),
            pl.BlockSpec((tm, d), lambda i: (i, 0)),
        ],
        out_specs=pl.BlockSpec((tm, d), lambda i: (i, 0)),
        out_shape=jax.ShapeDtypeStruct((n, d), F32),
        scratch_shapes=[pltpu.VMEM((tm + 8, ch), F32)],
        compiler_params=_params("arbitrary"),
        name="conv_out_proj",
    )(proj, proj, proj, conv_w, o_gla, w_conv, w_gla, x)


def _gla_scores_factored(q, k, g):
    c_len = q.shape[0]
    score_rows = []
    for a in range(c_len // GLA_BLOCK):
        lo, hi = a * GLA_BLOCK, (a + 1) * GLA_BLOCK
        n_keys = LANES * -(-hi // LANES)
        g_first = g[lo:lo + 1, :]
        q_a = (q[lo:hi] * jnp.exp(g[lo:hi] - g_first)).astype(BF16)
        k_a = (k[:n_keys] * jnp.exp(jnp.minimum(g_first - g[:n_keys], GLA_MAX_BLOCK_DECAY)))
        s_a = _dot_nt(q_a, k_a.astype(BF16))
        row = lax.broadcasted_iota(jnp.int32, s_a.shape, 0) + lo
        col = lax.broadcasted_iota(jnp.int32, s_a.shape, 1)
        s_a = jnp.where(col <= row, s_a, 0.0)
        if n_keys < c_len:
            s_a = jnp.concatenate([s_a, jnp.zeros((GLA_BLOCK, c_len - n_keys), F32)], axis=1)
        score_rows.append(s_a)
    return jnp.concatenate(score_rows, axis=0)


def _gla_scores_pairwise(q, k, g):
    c_len = q.shape[0]
    n_sub = c_len // GLA_SUB
    score_rows = [jnp.zeros((GLA_SUB, c_len), F32)]
    for a in range(1, n_sub):
        lo = a * GLA_SUB
        n_keys = LANES * -(-lo // LANES)
        g_ref_row = g[lo:lo + 1, :]
        q_a = (q[lo:lo + GLA_SUB] * jnp.exp(g[lo:lo + GLA_SUB] - g_ref_row)).astype(BF16)
        k_a = (k[:n_keys] * jnp.exp(jnp.minimum(g_ref_row - g[:n_keys], 0.0))).astype(BF16)
        s_a = _dot_nt(q_a, k_a)
        col = lax.broadcasted_iota(jnp.int32, s_a.shape, 1)
        s_a = jnp.where(col < lo, s_a, 0.0)
        if n_keys < c_len:
            s_a = jnp.concatenate([s_a, jnp.zeros((GLA_SUB, c_len - n_keys), F32)], axis=1)
        score_rows.append(s_a)
    scores = jnp.concatenate(score_rows, axis=0)

    row = lax.broadcasted_iota(jnp.int32, (c_len, LANES), 0)
    lane = lax.broadcasted_iota(jnp.int32, (c_len, LANES), 1)
    delta = row % LANES - lane
    band_id = jnp.where((delta >= 0) & (delta <= row % GLA_SUB), delta, -1)
    band = jnp.zeros((c_len, LANES), F32)
    for d in range(GLA_SUB):
        k_d = k if d == 0 else pltpu.roll(k, d, 0)
        g_d = g if d == 0 else pltpu.roll(g, d, 0)
        e = jnp.exp(jnp.minimum(g - g_d, 0.0))
        diag = jnp.sum(q * k_d * e, axis=-1, keepdims=True)
        band = jnp.where(band_id == d, diag, band)
    zeros = jnp.zeros((LANES, LANES), F32)
    band_rows = []
    for t in range(c_len // LANES):
        tiles = [zeros] * (c_len // LANES)
        tiles[t] = band[t * LANES:(t + 1) * LANES]
        band_rows.append(jnp.concatenate(tiles, axis=1))
    return scores + jnp.concatenate(band_rows, axis=0)


def _gla_kernel(q_ref, k_ref, v_ref, r_ref, la_ref, tri_ref, gn_ref, o_ref, st_ref, g_ref, *,
                q_scale, dk, dv):
    c_len = q_ref.shape[0]

    @pl.when(pl.program_id(1) == 0)
    def _():
        st_ref[...] = jnp.zeros_like(st_ref)

    la = la_ref[...]
    tri = tri_ref[...]
    la_hi = la.astype(BF16)
    rem = la - la_hi.astype(F32)
    la_mid = rem.astype(BF16)
    la_lo = (rem - la_mid.astype(F32)).astype(BF16)
    g_all = _dot(tri, la_hi) + _dot(tri, la_mid) + _dot(tri, la_lo)
    g_ref[...] = g_all
    block_decay = [g_all[lo:lo + 1, :] - g_all[lo + GLA_BLOCK - 1:lo + GLA_BLOCK, :]
                   for lo in range(0, c_len, GLA_BLOCK)]
    worst_decay = jnp.max(jnp.concatenate(block_decay, axis=0))

    def all_heads(scores_fn):
        for h in range(GLA_HEADS):
            g = g_ref[:, h * dk:(h + 1) * dk]
            g_last = g[c_len - 1:c_len, :]
            q = q_ref[:, h * dk:(h + 1) * dk].astype(F32) * q_scale
            k = k_ref[:, h * dk:(h + 1) * dk].astype(F32)
            v = v_ref[:, h * dv:(h + 1) * dv]

            st = st_ref[h]
            o = _dot_nt((q * jnp.exp(g)).astype(BF16), st.astype(BF16))
            k_dec = (k * jnp.exp(g_last - g)).astype(BF16)
            st_ref[h] = st * jnp.exp(g_last) + _dot_tn(v, k_dec)

            o = o + _dot(scores_fn(q, k, g).astype(BF16), v)
            o = _rmsnorm(o, gn_ref[...])
            r = r_ref[:, h * dv:(h + 1) * dv].astype(F32)
            o_ref[:, h * dv:(h + 1) * dv] = (o * (r / (1.0 + jnp.exp(-r)))).astype(o_ref.dtype)

    @pl.when(worst_decay <= GLA_MAX_BLOCK_DECAY)
    def _():
        all_heads(_gla_scores_factored)

    @pl.when(worst_decay > GLA_MAX_BLOCK_DECAY)
    def _():
        all_heads(_gla_scores_pairwise)


def _gla(proj, log_a, tri, norm_g, *, batch, seq, q_off, k_off, v_off, r_off, dk, dv):
    n = proj.shape[0]
    cpb = seq // GLA_CHUNK
    qk_w, vr_w = GLA_HEADS * dk, GLA_HEADS * dv
    row = lambda b, c: b * cpb + c
    kern = functools.partial(_gla_kernel, q_scale=float(dk) ** -0.5, dk=dk, dv=dv)
    return pl.pallas_call(
        kern,
        grid=(batch, cpb),
        in_specs=[
            pl.BlockSpec((GLA_CHUNK, qk_w), lambda b, c: (row(b, c), q_off // qk_w)),
            pl.BlockSpec((GLA_CHUNK, qk_w), lambda b, c: (row(b, c), k_off // qk_w)),
            pl.BlockSpec((GLA_CHUNK, vr_w), lambda b, c: (row(b, c), v_off // vr_w)),
            pl.BlockSpec((GLA_CHUNK, vr_w), lambda b, c: (row(b, c), r_off // vr_w)),
            pl.BlockSpec((GLA_CHUNK, qk_w), lambda b, c: (row(b, c), 0)),
            pl.BlockSpec((GLA_CHUNK, GLA_CHUNK), lambda b, c: (0, 0)),
            pl.BlockSpec((1, dv), lambda b, c: (0, 0)),
        ],
        out_specs=pl.BlockSpec((GLA_CHUNK, vr_w), lambda b, c: (row(b, c), 0)),
        out_shape=jax.ShapeDtypeStruct((n, vr_w), BF16),
        scratch_shapes=[pltpu.VMEM((GLA_HEADS, dv, dk), F32), pltpu.VMEM((GLA_CHUNK, qk_w), F32)],
        compiler_params=_params("parallel", "arbitrary"),
        name="gla",
    )(proj, proj, proj, proj, log_a, tri, norm_g)


def _attn_out_proj_kernel(a_ref, w_ref, x_ref, o_ref):
    tm, d = x_ref.shape
    acc = _dot(a_ref[...].reshape(tm, a_ref.shape[-1]), w_ref[...])
    acc = jnp.swapaxes(acc.reshape(DIL_MAX, tm // DIL_MAX, d), 0, 1).reshape(tm, d)
    o_ref[...] = x_ref[...] + acc


def _attn_out_proj_residual(attn, w, x, *, batch, seq, tm=512):
    n, d = x.shape
    per = tm // DIL_MAX
    tps = seq // tm
    attn = attn.reshape(batch, DIL_MAX, seq // DIL_MAX, attn.shape[1])
    return pl.pallas_call(
        _attn_out_proj_kernel,
        grid=(n // tm,),
        in_specs=[
            pl.BlockSpec((None, DIL_MAX, per, attn.shape[-1]), lambda i: (i // tps, 0, i % tps, 0)),
            pl.BlockSpec(w.shape, lambda i: (0, 0)),
            pl.BlockSpec((tm, d), lambda i: (i, 0)),
        ],
        out_specs=pl.BlockSpec((tm, d), lambda i: (i, 0)),
        out_shape=jax.ShapeDtypeStruct((n, d), F32),
        compiler_params=_params("parallel"),
        name="attn_out_proj",
    )(attn, w, x)


def _mlp_kernel(x_ref, xc_ref, g_ref, w1_ref, w2_ref, gf_ref, o_ref, xn_ref, *, final_norm):
    i, f = pl.program_id(0), pl.program_id(1)
    ch = xc_ref.shape[0]
    fill = i % 2

    def norm_chunk():
        xn_ref[fill, pl.ds(pl.multiple_of(f * ch, ch), ch), :] = (
            _rmsnorm(xc_ref[...], g_ref[...]).astype(BF16))

    @pl.when(i == 0)
    def _():
        norm_chunk()

    @pl.when((i > 0) & (f == 0))
    def _():
        o_ref[...] = x_ref[...]

    @pl.when(i > 0)
    def _():
        norm_chunk()
        h = _dot(xn_ref[1 - fill], w1_ref[...])
        h = jnp.square(jnp.maximum(h, 0.0)).astype(BF16)
        o_ref[...] += _dot(h, w2_ref[...])

    if final_norm:
        @pl.when((i > 0) & (f == pl.num_programs(1) - 1))
        def _():
            o_ref[...] = _rmsnorm(o_ref[...], gf_ref[...])


def _mlp_residual(x, gain, w1, w2, final_gain, *, final_norm, tm=512, tf=1024):
    n, d = x.shape
    d_ff = w1.shape[1]
    n_tiles, steps = n // tm, d_ff // tf
    ch = tm // steps
    kern = functools.partial(_mlp_kernel, final_norm=final_norm)
    return pl.pallas_call(
        kern,
        grid=(n_tiles + 1, steps),
        in_specs=[
            pl.BlockSpec((tm, d), lambda i, f: (_tile_of_trip(i), 0)),
            pl.BlockSpec((ch, d), lambda i, f: (_chunk_index(i, f, n_tiles, steps), 0)),
            pl.BlockSpec((1, d), lambda i, f: (0, 0)),
            pl.BlockSpec((d, tf), lambda i, f: (0, f)),
            pl.BlockSpec((tf, d), lambda i, f: (f, 0)),
            pl.BlockSpec((1, d), lambda i, f: (0, 0)),
        ],
        out_specs=pl.BlockSpec((tm, d), lambda i, f: (_tile_of_trip(i), 0)),
        out_shape=jax.ShapeDtypeStruct((n, d), F32),
        scratch_shapes=[pltpu.VMEM((2, tm, d), BF16)],
        compiler_params=_params("arbitrary", "arbitrary"),
        name="mlp",
    )(x, x, gain, w1, w2, final_gain)


def _attn_bias_tables(slopes, sub):
    def table(q_pos, k_pos, dilation):
        diff = q_pos[:, None] - k_pos[None, :]
        valid = jnp.asarray((diff >= 0) & (diff <= ATTN_STEPS))
        dist = jnp.asarray((dilation * diff).astype(np.float32))
        bias = -(slopes * LOG2_E)[:, None, None] * dist[None]
        return jnp.where(valid[None], bias, MASK_VALUE).astype(F32)

    q = np.arange(128)
    b16_first = table(q, q, 16)
    b16 = table(q + 128, np.arange(256), 16)

    q4 = 4 * (q % 32) + q // 32
    k = np.arange(256)
    k4 = 4 * (k % 64) + k // 64
    b4 = jnp.stack([table(q4, k4, 4), table(q4 + 128, k4, 4)], axis=1)

    q = np.arange(256)
    q1 = 16 * (q % 16) + q // 16
    k = np.arange(384)
    k1 = 16 * (k % 24) + k // 24
    b1 = jnp.stack([table(q1, k1, 1), table(q1 + 128, k1, 1)], axis=1)
    return b1, b4, b16_first, b16


def _attn_kernel(q_ref, k_ref, v_ref, b1_ref, b4_ref, b16f_ref, b16_ref, o_ref,
                 m_ref, l_ref, acc_ref, *, sub):
    def gather(ref, starts, size):
        return jnp.concatenate([ref[pl.ds(s, size), :] for s in starts], axis=0)

    def scatter(ref, starts, size, val):
        for c, s in enumerate(starts):
            ref[pl.ds(s, size), :] = val[c * size:(c + 1) * size]

    def lanes(x):
        return jnp.broadcast_to(x, (x.shape[0], LANES))

    def across_keys(x, n_keys):
        return jnp.concatenate([x] * (n_keys // LANES), axis=1)

    def weighted_values_and_sum(p, v):
        pv = _dot(p.astype(BF16), jnp.concatenate([v, jnp.ones_like(v)], axis=1))
        return pv[:, :LANES], pv[:, LANES:]

    def first_tile(q, k, v, bias):
        s = _dot_nt(q, k) + bias
        m = jnp.max(s, axis=-1, keepdims=True)
        acc, l = weighted_values_and_sum(jnp.exp2(s - m), v)
        return lanes(m), l, acc

    def next_tile(q, k, v, bias, m_old, l_old, acc_old):
        s = _dot_nt(q, k) + bias
        m_new = jnp.maximum(m_old, lanes(jnp.max(s, axis=-1, keepdims=True)))
        alpha = jnp.exp2(m_old - m_new)
        acc, l = weighted_values_and_sum(jnp.exp2(s - across_keys(m_new, s.shape[1])), v)
        return m_new, alpha * l_old + l, alpha * acc_old + acc

    def group16(i, carry):
        bias_first, bias = b16f_ref[...], b16_ref[...]
        rows, res = [], []
        for u in range(TILES_16):
            base = (TILES_16 * i + u) * sub
            for j in range(sub // ATTN_STEPS):
                q_rows = pl.ds(pl.multiple_of(base + j * ATTN_STEPS, ATTN_STEPS), ATTN_STEPS)
                if j == 0:
                    k_rows, b = q_rows, bias_first
                else:
                    k_rows = pl.ds(pl.multiple_of(base + (j - 1) * ATTN_STEPS, ATTN_STEPS),
                                   2 * ATTN_STEPS)
                    b = bias
                rows.append(q_rows)
                res.append(first_tile(q_ref[q_rows, :], k_ref[k_rows, :], v_ref[k_rows, :], b))
        for r, (m, l, acc) in zip(rows, res):
            m_ref[r, :] = m
            l_ref[r, :] = l
            acc_ref[r, :] = acc
        return carry

    lax.fori_loop(0, DIL_MAX // TILES_16, group16, 0)

    blocks_4 = min(BLOCKS_4, sub // 32)

    def group4(i, carry):
        starts, res = [], []
        for u in range(blocks_4):
            blk = blocks_4 * i + u
            bias = b4_ref[jnp.where(blk == 0, 0, 1)]
            k_blk = jnp.maximum(blk - 1, 0)
            for r4 in range(4):
                qs = [pl.multiple_of((r4 + 4 * c) * sub + 32 * blk, 32) for c in range(4)]
                ks = [pl.multiple_of((r4 + 4 * c) * sub + 32 * k_blk, 32) for c in range(4)]
                starts.append(qs)
                res.append(next_tile(
                    gather(q_ref, qs, 32), gather(k_ref, ks, 64), gather(v_ref, ks, 64), bias,
                    gather(m_ref, qs, 32), gather(l_ref, qs, 32), gather(acc_ref, qs, 32)))
        for qs, (m, l, acc) in zip(starts, res):
            scatter(m_ref, qs, 32, m)
            scatter(l_ref, qs, 32, l)
            scatter(acc_ref, qs, 32, acc)
        return carry

    lax.fori_loop(0, sub // (32 * blocks_4), group4, 0)

    blocks_1 = min(BLOCKS_1, sub // 16)

    def group1(i, carry):
        res = []
        for u in range(blocks_1):
            blk = blocks_1 * i + u
            bias = b1_ref[jnp.where(blk == 0, 0, 1)]
            k_lo = jnp.maximum(16 * blk - 8, 0)
            qs = [pl.multiple_of(r * sub + 16 * blk, 16) for r in range(DIL_MAX)]
            ks = [pl.multiple_of(r * sub + k_lo, 8) for r in range(DIL_MAX)]
            _, l, acc = next_tile(
                gather(q_ref, qs, 16), gather(k_ref, ks, 24), gather(v_ref, ks, 24), bias,
                gather(m_ref, qs, 16), gather(l_ref, qs, 16), gather(acc_ref, qs, 16))
            res.append((qs, (acc / l).astype(o_ref.dtype)))
        for qs, out in res:
            scatter(o_ref, qs, 16, out)
        return carry

    lax.fori_loop(0, sub // (16 * blocks_1), group1, 0)


def _dilated_attention(qkv, slopes, *, batch, seq, dh):
    n = qkv.shape[0]
    sub = seq // DIL_MAX
    b1, b4, b16_first, b16 = _attn_bias_tables(slopes, sub)
    kern = functools.partial(_attn_kernel, sub=sub)
    return pl.pallas_call(
        kern,
        grid=(batch, ATTN_HEADS),
        in_specs=[
            pl.BlockSpec((seq, dh), lambda b, h: (b, h)),
            pl.BlockSpec((seq, dh), lambda b, h: (b, ATTN_HEADS + h)),
            pl.BlockSpec((seq, dh), lambda b, h: (b, 2 * ATTN_HEADS + h)),
            pl.BlockSpec((None,) + b1.shape[1:], lambda b, h: (h, 0, 0, 0)),
            pl.BlockSpec((None,) + b4.shape[1:], lambda b, h: (h, 0, 0, 0)),
            pl.BlockSpec((None,) + b16_first.shape[1:], lambda b, h: (h, 0, 0)),
            pl.BlockSpec((None,) + b16.shape[1:], lambda b, h: (h, 0, 0)),
        ],
        out_specs=pl.BlockSpec((seq, dh), lambda b, h: (b, h)),
        out_shape=jax.ShapeDtypeStruct((n, ATTN_HEADS * dh), BF16),
        scratch_shapes=[pltpu.VMEM((seq, LANES), F32), pltpu.VMEM((seq, LANES), F32),
                        pltpu.VMEM((seq, dh), F32)],
        compiler_params=_params("parallel", "parallel"),
        name="dilated_attn",
    )(qkv, qkv, qkv, b1, b4, b16_first, b16)


def kernel(x, norm_mix_g, norm_mlp_g, final_norm_g, hyb_w_in, conv_w, gla_w_gate2, gla_b_gate,
           gla_norm_g, hyb_w_out, attn_w_qkv, attn_w_o, mlp_w1, mlp_w2):
    batch, seq, d = x.shape
    n = batch * seq
    depth = norm_mix_g.shape[0]
    conv_ch = conv_w.shape[-1]
    dv = gla_norm_g.shape[-1]
    dk = gla_w_gate2.shape[-1] // GLA_HEADS
    dh = d // ATTN_HEADS
    assert depth == 2 and seq % (DIL_MAX * ATTN_STEPS) == 0 and seq % GLA_CHUNK == 0
    assert dk == LANES and dh == LANES and dv % LANES == 0

    row = lambda v: v.reshape(1, -1).astype(F32)
    xf = x.reshape(n, d)

    main_cols = 3 * conv_ch + 2 * GLA_HEADS * dk + 2 * GLA_HEADS * dv
    w_in = hyb_w_in[0]
    w_main = w_in[:, :main_cols].astype(BF16)
    w_low = jnp.pad(w_in[:, main_cols:], ((0, 0), (0, LANES - GLA_GATE_RANK))).astype(BF16)
    w_gate = jnp.pad(gla_w_gate2[0], ((0, LANES - GLA_GATE_RANK), (0, 0))).astype(BF16)
    proj, log_a = _in_proj_gla(xf, row(norm_mix_g[0]), w_main, w_low, w_gate, row(gla_b_gate[0]))

    q_off = 3 * conv_ch
    k_off = q_off + GLA_HEADS * dk
    v_off = k_off + GLA_HEADS * dk
    r_off = v_off + GLA_HEADS * dv
    tri = jnp.asarray(np.tril(np.ones((GLA_CHUNK, GLA_CHUNK), np.float32)), BF16)
    o_gla = _gla(proj, log_a, tri, row(gla_norm_g[0]), batch=batch, seq=seq,
                 q_off=q_off, k_off=k_off, v_off=v_off, r_off=r_off, dk=dk, dv=dv)
    w_out = hyb_w_out[0].astype(BF16)
    xf = _conv_out_proj_residual(proj, conv_w[0].astype(F32), o_gla, w_out[:conv_ch],
                                 w_out[conv_ch:], xf, seq=seq)
    xf = _mlp_residual(xf, row(norm_mlp_g[0]), mlp_w1[0].astype(BF16), mlp_w2[0].astype(BF16),
                       row(final_norm_g), final_norm=False)

    col_scale = jnp.concatenate([jnp.full((1, d), float(dh) ** -0.5 * LOG2_E, F32),
                                 jnp.ones((1, 2 * d), F32)], axis=1)
    qkv = _qkv_proj_residue_major(xf, row(norm_mix_g[1]), attn_w_qkv[0].astype(BF16), col_scale,
                                  batch=batch, seq=seq)
    slopes = jnp.exp2(-8.0 * jnp.arange(1, ATTN_HEADS + 1, dtype=F32) / ATTN_HEADS)
    attn = _dilated_attention(qkv, slopes, batch=batch, seq=seq, dh=dh)
    xf = _attn_out_proj_residual(attn, attn_w_o[0].astype(BF16), xf, batch=batch, seq=seq)
    out = _mlp_residual(xf, row(norm_mlp_g[1]), mlp_w1[1].astype(BF16), mlp_w2[1].astype(BF16),
                        row(final_norm_g), final_norm=True)
    return out.reshape(batch, seq, d)
```

```python
import functools

import jax
import jax.numpy as jnp
import numpy as np
from jax import lax
from jax.experimental import pallas as pl
from jax.experimental.pallas import tpu as pltpu

F32 = jnp.float32
BF16 = jnp.bfloat16

NORM_EPS = 1e-6
CONV_K = 3
GLA_HEADS = 4
GLA_GATE_RANK = 16
GLA_GATE_TEMP = 16.0
ATTN_HEADS = 16
DILATED_GROUPS = ((128, 1), (512, 4), (2048, 16))
DIL_MAX = 16
ATTN_STEPS = 128

LANES = 128
GLA_CHUNK = 256
GLA_SUB = 16
GLA_BLOCK = 64
GLA_MAX_BLOCK_DECAY = 60.0
MASK_VALUE = -1e30
LOG2_E = 1.4426950408889634
TILES_16 = 16
BLOCKS_4 = 8
BLOCKS_1 = 16
VMEM_LIMIT_BYTES = 56 * 1024 * 1024


def _params(*semantics):
    return pltpu.CompilerParams(dimension_semantics=semantics, vmem_limit_bytes=VMEM_LIMIT_BYTES)


def _dot(a, b):
    return jnp.dot(a, b, preferred_element_type=F32)


def _dot_nt(a, b):
    return lax.dot_general(a, b, (((1,), (1,)), ((), ())), preferred_element_type=F32)


def _dot_tn(a, b):
    return lax.dot_general(a, b, (((0,), (0,)), ((), ())), preferred_element_type=F32)


def _rmsnorm(x, g):
    return x * lax.rsqrt(jnp.mean(x * x, axis=-1, keepdims=True) + NORM_EPS) * g


def _tile_of_trip(i):
    return jnp.maximum(i - 1, 0)


def _chunk_index(i, j, n_tiles, chunks):
    return jnp.minimum(i, n_tiles - 1) * chunks + j


def _in_proj_gla_kernel(xc_ref, g_ref, w_ref, wlow_ref, wgate_ref, bgate_ref, o_ref, loga_ref,
                        xn_ref):
    i, j = pl.program_id(0), pl.program_id(1)
    ch = xc_ref.shape[0]
    fill = i % 2
    rows = pl.ds(pl.multiple_of(j * ch, ch), ch)

    def norm_and_gate_chunk():
        xn = _rmsnorm(xc_ref[...], g_ref[...]).astype(BF16)
        xn_ref[fill, rows, :] = xn
        g_low = _dot(xn, wlow_ref[...]).astype(BF16)
        gate = _dot(g_low, wgate_ref[...]) + bgate_ref[...]
        log_sig = jnp.minimum(gate, 0.0) - jnp.log(1.0 + jnp.exp(-jnp.abs(gate)))
        loga_ref[rows, :] = log_sig * (1.0 / GLA_GATE_TEMP)

    @pl.when(i == 0)
    def _():
        norm_and_gate_chunk()

    @pl.when(i > 0)
    def _():
        norm_and_gate_chunk()
        o_ref[...] = _dot(xn_ref[1 - fill], w_ref[...]).astype(o_ref.dtype)


def _in_proj_gla(x, gain, w, w_low, w_gate, b_gate, *, cols, tm=1024, tn=1536):
    n, d = x.shape
    n_gate = w_gate.shape[1]
    n_tiles, steps = n // tm, cols // tn
    ch = tm // steps
    return pl.pallas_call(
        _in_proj_gla_kernel,
        grid=(n_tiles + 1, steps),
        in_specs=[
            pl.BlockSpec((ch, d), lambda i, j: (_chunk_index(i, j, n_tiles, steps), 0)),
            pl.BlockSpec((1, d), lambda i, j: (0, 0)),
            pl.BlockSpec((d, tn), lambda i, j: (0, j)),
            pl.BlockSpec((d, LANES), lambda i, j: (0, 0)),
            pl.BlockSpec((LANES, n_gate), lambda i, j: (0, 0)),
            pl.BlockSpec((1, n_gate), lambda i, j: (0, 0)),
        ],
        out_specs=[
            pl.BlockSpec((tm, tn), lambda i, j: (_tile_of_trip(i), j * jnp.minimum(i, 1))),
            pl.BlockSpec((tm, n_gate), lambda i, j: (jnp.minimum(i, n_tiles - 1), 0)),
        ],
        out_shape=[
            jax.ShapeDtypeStruct((n, cols), BF16),
            jax.ShapeDtypeStruct((n, n_gate), F32),
        ],
        scratch_shapes=[pltpu.VMEM((2, tm, d), BF16)],
        compiler_params=_params("arbitrary", "arbitrary"),
        name="in_proj_gla",
    )(x, gain, w, w_low, w_gate, b_gate)


def _qkv_proj_kernel(xc_ref, g_ref, w_ref, scale_ref, o_ref, xn_ref):
    i, j = pl.program_id(0), pl.program_id(1)
    ch, d = xc_ref.shape
    per = ch // DIL_MAX
    fill = i % 2

    def norm_chunk():
        x = jnp.swapaxes(xc_ref[...].reshape(per, DIL_MAX, d), 0, 1).reshape(ch, d)
        xn = _rmsnorm(x, g_ref[...]).astype(BF16)
        xn_ref[fill, :, pl.ds(pl.multiple_of(j * per, per), per), :] = xn.reshape(DIL_MAX, per, d)

    @pl.when(i == 0)
    def _():
        norm_chunk()

    @pl.when(i > 0)
    def _():
        norm_chunk()
        xn = xn_ref[1 - fill]
        acc = _dot(xn.reshape(xn.shape[0] * xn.shape[1], d), w_ref[...]) * scale_ref[...]
        o_ref[...] = acc.astype(o_ref.dtype).reshape(o_ref.shape)


def _qkv_proj_residue_major(x, gain, w, col_scale, *, batch, seq, tm=1024, tn=1536):
    n, d = x.shape
    cols = w.shape[1]
    sub = seq // DIL_MAX
    per = tm // DIL_MAX
    tiles_per_seq = seq // tm
    n_tiles, steps = n // tm, cols // tn
    ch = tm // steps
    assert ch % (DIL_MAX * 16) == 0

    def out_index(i, j):
        t = _tile_of_trip(i)
        return (t // tiles_per_seq, 0, t % tiles_per_seq, j * jnp.minimum(i, 1))

    out = pl.pallas_call(
        _qkv_proj_kernel,
        grid=(n_tiles + 1, steps),
        in_specs=[
            pl.BlockSpec((ch, d), lambda i, j: (_chunk_index(i, j, n_tiles, steps), 0)),
            pl.BlockSpec((1, d), lambda i, j: (0, 0)),
            pl.BlockSpec((d, tn), lambda i, j: (0, j)),
            pl.BlockSpec((1, tn), lambda i, j: (0, j)),
        ],
        out_specs=pl.BlockSpec((None, DIL_MAX, per, tn), out_index),
        out_shape=jax.ShapeDtypeStruct((batch, DIL_MAX, sub, cols), BF16),
        scratch_shapes=[pltpu.VMEM((2, DIL_MAX, per, d), BF16)],
        compiler_params=_params("arbitrary", "arbitrary"),
        name="qkv_proj",
    )(x, gain, w, col_scale)
    return out.reshape(n, cols)


def _conv_out_proj_kernel(ab_ref, ac_ref, ax_ref, cw_ref, gla_ref, wa_ref, wb_ref, x_ref, o_ref,
                          ubuf_ref, *, tiles_per_seq):
    tb = ab_ref.shape[0]
    pad = 8

    @pl.when(pl.program_id(0) % tiles_per_seq == 0)
    def _():
        ubuf_ref[0:pad, :] = jnp.zeros((pad, ubuf_ref.shape[1]), F32)

    acc = x_ref[...] + _dot(gla_ref[...], wb_ref[...])
    u = ac_ref[...].astype(F32) * ax_ref[...].astype(F32)
    ubuf_ref[pad:pad + tb, :] = u
    u1 = ubuf_ref[pad - 1:pad - 1 + tb, :]
    u2 = ubuf_ref[pad - 2:pad - 2 + tb, :]
    cw = cw_ref[...]
    y = ab_ref[...].astype(F32) * (cw[0:1, :] * u2 + cw[1:2, :] * u1 + cw[2:3, :] * u)
    o_ref[...] = acc + _dot(y.astype(BF16), wa_ref[...])
    ubuf_ref[0:pad, :] = ubuf_ref[tb:tb + pad, :]


def _conv_out_proj_residual(proj, conv_w, o_gla, w_out, x, *, seq, tm=512):
    n, d = x.shape
    ch = conv_w.shape[1]
    assert w_out.shape[0] == 2 * ch and o_gla.shape[1] == ch
    kern = functools.partial(_conv_out_proj_kernel, tiles_per_seq=seq // tm)
    return pl.pallas_call(
        kern,
        grid=(n // tm,),
        in_specs=[
            pl.BlockSpec((tm, ch), lambda i: (i, 0)),
            pl.BlockSpec((tm, ch), lambda i: (i, 1)),
            pl.BlockSpec((tm, ch), lambda i: (i, 2)),
            pl.BlockSpec((CONV_K, ch), lambda i: (0, 0)),
            pl.BlockSpec((tm, o_gla.shape[1]), lambda i: (i, 0)),
            pl.BlockSpec((ch, d), lambda i: (0, 0)),
            pl.BlockSpec((ch, d), lambda i: (1, 0)),
            pl.BlockSpec((tm, d), lambda i: (i, 0)),
        ],
        out_specs=pl.BlockSpec((tm, d), lambda i: (i, 0)),
        out_shape=jax.ShapeDtypeStruct((n, d), F32),
        scratch_shapes=[pltpu.VMEM((tm + 8, ch), F32)],
        compiler_params=_params("arbitrary"),
        name="conv_out_proj",
    )(proj, proj, proj, conv_w, o_gla, w_out, w_out, x)


def _gla_scores_factored(q, k, g):
    c_len = q.shape[0]
    score_rows = []
    for a in range(c_len // GLA_BLOCK):
        lo, hi = a * GLA_BLOCK, (a + 1) * GLA_BLOCK
        n_keys = LANES * -(-hi // LANES)
        g_first = g[lo:lo + 1, :]
        q_a = (q[lo:hi] * jnp.exp(g[lo:hi] - g_first)).astype(BF16)
        k_a = (k[:n_keys] * jnp.exp(jnp.minimum(g_first - g[:n_keys], GLA_MAX_BLOCK_DECAY)))
        s_a = _dot_nt(q_a, k_a.astype(BF16))
        row = lax.broadcasted_iota(jnp.int32, s_a.shape, 0) + lo
        col = lax.broadcasted_iota(jnp.int32, s_a.shape, 1)
        s_a = jnp.where(col <= row, s_a, 0.0)
        if n_keys < c_len:
            s_a = jnp.concatenate([s_a, jnp.zeros((GLA_BLOCK, c_len - n_keys), F32)], axis=1)
        score_rows.append(s_a)
    return jnp.concatenate(score_rows, axis=0)


def _gla_scores_pairwise(q, k, g):
    c_len = q.shape[0]
    n_sub = c_len // GLA_SUB
    score_rows = [jnp.zeros((GLA_SUB, c_len), F32)]
    for a in range(1, n_sub):
        lo = a * GLA_SUB
        n_keys = LANES * -(-lo // LANES)
        g_ref_row = g[lo:lo + 1, :]
        q_a = (q[lo:lo + GLA_SUB] * jnp.exp(g[lo:lo + GLA_SUB] - g_ref_row)).astype(BF16)
        k_a = (k[:n_keys] * jnp.exp(jnp.minimum(g_ref_row - g[:n_keys], 0.0))).astype(BF16)
        s_a = _dot_nt(q_a, k_a)
        col = lax.broadcasted_iota(jnp.int32, s_a.shape, 1)
        s_a = jnp.where(col < lo, s_a, 0.0)
        if n_keys < c_len:
            s_a = jnp.concatenate([s_a, jnp.zeros((GLA_SUB, c_len - n_keys), F32)], axis=1)
        score_rows.append(s_a)
    scores = jnp.concatenate(score_rows, axis=0)

    row = lax.broadcasted_iota(jnp.int32, (c_len, LANES), 0)
    lane = lax.broadcasted_iota(jnp.int32, (c_len, LANES), 1)
    delta = row % LANES - lane
    band_id = jnp.where((delta >= 0) & (delta <= row % GLA_SUB), delta, -1)
    band = jnp.zeros((c_len, LANES), F32)
    for d in range(GLA_SUB):
        k_d = k if d == 0 else pltpu.roll(k, d, 0)
        g_d = g if d == 0 else pltpu.roll(g, d, 0)
        e = jnp.exp(jnp.minimum(g - g_d, 0.0))
        diag = jnp.sum(q * k_d * e, axis=-1, keepdims=True)
        band = jnp.where(band_id == d, diag, band)
    zeros = jnp.zeros((LANES, LANES), F32)
    band_rows = []
    for t in range(c_len // LANES):
        tiles = [zeros] * (c_len // LANES)
        tiles[t] = band[t * LANES:(t + 1) * LANES]
        band_rows.append(jnp.concatenate(tiles, axis=1))
    return scores + jnp.concatenate(band_rows, axis=0)


def _gla_kernel(q_ref, k_ref, v_ref, r_ref, la_ref, tri_ref, gn_ref, o_ref, st_ref, g_ref, *,
                q_scale, dk, dv):
    c_len = q_ref.shape[0]

    @pl.when(pl.program_id(1) == 0)
    def _():
        st_ref[...] = jnp.zeros_like(st_ref)

    la = la_ref[...]
    tri = tri_ref[...]
    la_hi = la.astype(BF16)
    rem = la - la_hi.astype(F32)
    la_mid = rem.astype(BF16)
    la_lo = (rem - la_mid.astype(F32)).astype(BF16)
    g_all = _dot(tri, la_hi) + _dot(tri, la_mid) + _dot(tri, la_lo)
    g_ref[...] = g_all
    block_decay = [g_all[lo:lo + 1, :] - g_all[lo + GLA_BLOCK - 1:lo + GLA_BLOCK, :]
                   for lo in range(0, c_len, GLA_BLOCK)]
    worst_decay = jnp.max(jnp.concatenate(block_decay, axis=0))

    def all_heads(scores_fn):
        for h in range(GLA_HEADS):
            g = g_ref[:, h * dk:(h + 1) * dk]
            g_last = g[c_len - 1:c_len, :]
            q = q_ref[:, h * dk:(h + 1) * dk].astype(F32) * q_scale
            k = k_ref[:, h * dk:(h + 1) * dk].astype(F32)
            v = v_ref[:, h * dv:(h + 1) * dv]

            st = st_ref[h]
            o = _dot_nt((q * jnp.exp(g)).astype(BF16), st.astype(BF16))
            k_dec = (k * jnp.exp(g_last - g)).astype(BF16)
            st_ref[h] = st * jnp.exp(g_last) + _dot_tn(v, k_dec)

            o = o + _dot(scores_fn(q, k, g).astype(BF16), v)
            o = _rmsnorm(o, gn_ref[...])
            r = r_ref[:, h * dv:(h + 1) * dv].astype(F32)
            o_ref[:, h * dv:(h + 1) * dv] = (o * (r / (1.0 + jnp.exp(-r)))).astype(o_ref.dtype)

    @pl.when(worst_decay <= GLA_MAX_BLOCK_DECAY)
    def _():
        all_heads(_gla_scores_factored)

    @pl.when(worst_decay > GLA_MAX_BLOCK_DECAY)
    def _():
        all_heads(_gla_scores_pairwise)


def _gla(proj, log_a, tri, norm_g, *, batch, seq, q_off, k_off, v_off, r_off, dk, dv):
    n = proj.shape[0]
    cpb = seq // GLA_CHUNK
    qk_w, vr_w = GLA_HEADS * dk, GLA_HEADS * dv
    row = lambda b, c: b * cpb + c
    kern = functools.partial(_gla_kernel, q_scale=float(dk) ** -0.5, dk=dk, dv=dv)
    return pl.pallas_call(
        kern,
        grid=(batch, cpb),
        in_specs=[
            pl.BlockSpec((GLA_CHUNK, qk_w), lambda b, c: (row(b, c), q_off // qk_w)),
            pl.BlockSpec((GLA_CHUNK, qk_w), lambda b, c: (row(b, c), k_off // qk_w)),
            pl.BlockSpec((GLA_CHUNK, vr_w), lambda b, c: (row(b, c), v_off // vr_w)),
            pl.BlockSpec((GLA_CHUNK, vr_w), lambda b, c: (row(b, c), r_off // vr_w)),
            pl.BlockSpec((GLA_CHUNK, qk_w), lambda b, c: (row(b, c), 0)),
            pl.BlockSpec((GLA_CHUNK, GLA_CHUNK), lambda b, c: (0, 0)),
            pl.BlockSpec((1, dv), lambda b, c: (0, 0)),
        ],
        out_specs=pl.BlockSpec((GLA_CHUNK, vr_w), lambda b, c: (row(b, c), 0)),
        out_shape=jax.ShapeDtypeStruct((n, vr_w), BF16),
        scratch_shapes=[pltpu.VMEM((GLA_HEADS, dv, dk), F32), pltpu.VMEM((GLA_CHUNK, qk_w), F32)],
        compiler_params=_params("parallel", "arbitrary"),
        name="gla",
    )(proj, proj, proj, proj, log_a, tri, norm_g)


def _attn_out_proj_kernel(a_ref, w_ref, x_ref, o_ref):
    tm, d = x_ref.shape
    acc = _dot(a_ref[...].reshape(tm, a_ref.shape[-1]), w_ref[...])
    acc = jnp.swapaxes(acc.reshape(DIL_MAX, tm // DIL_MAX, d), 0, 1).reshape(tm, d)
    o_ref[...] = x_ref[...] + acc


def _attn_out_proj_residual(attn, w, x, *, batch, seq, tm=512):
    n, d = x.shape
    per = tm // DIL_MAX
    tps = seq // tm
    attn = attn.reshape(batch, DIL_MAX, seq // DIL_MAX, attn.shape[1])
    return pl.pallas_call(
        _attn_out_proj_kernel,
        grid=(n // tm,),
        in_specs=[
            pl.BlockSpec((None, DIL_MAX, per, attn.shape[-1]), lambda i: (i // tps, 0, i % tps, 0)),
            pl.BlockSpec(w.shape, lambda i: (0, 0)),
            pl.BlockSpec((tm, d), lambda i: (i, 0)),
        ],
        out_specs=pl.BlockSpec((tm, d), lambda i: (i, 0)),
        out_shape=jax.ShapeDtypeStruct((n, d), F32),
        compiler_params=_params("parallel"),
        name="attn_out_proj",
    )(attn, w, x)


def _mlp_kernel(x_ref, g_ref, w1_ref, w2_ref, gf_ref, o_ref, xn_ref, *, final_norm):
    f = pl.program_id(1)

    @pl.when(f == 0)
    def _():
        x = x_ref[...]
        xn_ref[...] = _rmsnorm(x, g_ref[...]).astype(BF16)
        o_ref[...] = x

    h = _dot(xn_ref[...], w1_ref[...])
    h = jnp.square(jnp.maximum(h, 0.0)).astype(BF16)
    o_ref[...] += _dot(h, w2_ref[...])

    if final_norm:
        @pl.when(f == pl.num_programs(1) - 1)
        def _():
            o_ref[...] = _rmsnorm(o_ref[...], gf_ref[...])


def _mlp_residual(x, gain, w1_layers, w2_layers, layer, final_gain, *, final_norm, tm=512,
                  tf=1024):
    n, d = x.shape
    d_ff = w1_layers.shape[2]
    kern = functools.partial(_mlp_kernel, final_norm=final_norm)
    return pl.pallas_call(
        kern,
        grid=(n // tm, d_ff // tf),
        in_specs=[
            pl.BlockSpec((tm, d), lambda i, f: (i, 0)),
            pl.BlockSpec((1, d), lambda i, f: (0, 0)),
            pl.BlockSpec((None, d, tf), lambda i, f: (layer, 0, f)),
            pl.BlockSpec((None, tf, d), lambda i, f: (layer, f, 0)),
            pl.BlockSpec((1, d), lambda i, f: (0, 0)),
        ],
        out_specs=pl.BlockSpec((tm, d), lambda i, f: (i, 0)),
        out_shape=jax.ShapeDtypeStruct((n, d), F32),
        scratch_shapes=[pltpu.VMEM((tm, d), BF16)],
        compiler_params=_params("parallel", "arbitrary"),
        name="mlp",
    )(x, gain, w1_layers, w2_layers, final_gain)


def _attn_bias_tables(slopes, sub):
    def table(q_pos, k_pos, dilation):
        diff = q_pos[:, None] - k_pos[None, :]
        valid = jnp.asarray((diff >= 0) & (diff <= ATTN_STEPS))
        dist = jnp.asarray((dilation * diff).astype(np.float32))
        bias = -(slopes * LOG2_E)[:, None, None] * dist[None]
        return jnp.where(valid[None], bias, MASK_VALUE).astype(F32)

    q = np.arange(128)
    b16_first = table(q, q, 16)
    b16 = table(q + 128, np.arange(256), 16)

    q4 = 4 * (q % 32) + q // 32
    k = np.arange(256)
    k4 = 4 * (k % 64) + k // 64
    b4 = jnp.stack([table(q4, k4, 4), table(q4 + 128, k4, 4)], axis=1)

    q = np.arange(256)
    q1 = 16 * (q % 16) + q // 16
    k = np.arange(384)
    k1 = 16 * (k % 24) + k // 24
    b1 = jnp.stack([table(q1, k1, 1), table(q1 + 128, k1, 1)], axis=1)
    return b1, b4, b16_first, b16


def _attn_kernel(q_ref, k_ref, v_ref, b1_ref, b4_ref, b16f_ref, b16_ref, o_ref,
                 m_ref, l_ref, acc_ref, *, sub):
    def gather(ref, starts, size):
        return jnp.concatenate([ref[pl.ds(s, size), :] for s in starts], axis=0)

    def scatter(ref, starts, size, val):
        for c, s in enumerate(starts):
            ref[pl.ds(s, size), :] = val[c * size:(c + 1) * size]

    def lanes(x):
        return jnp.broadcast_to(x, (x.shape[0], LANES))

    def across_keys(x, n_keys):
        return jnp.concatenate([x] * (n_keys // LANES), axis=1)

    def weighted_values_and_sum(p, v):
        pv = _dot(p.astype(BF16), jnp.concatenate([v, jnp.ones_like(v)], axis=1))
        return pv[:, :LANES], pv[:, LANES:]

    def first_tile(q, k, v, bias):
        s = _dot_nt(q, k) + bias
        m = jnp.max(s, axis=-1, keepdims=True)
        acc, l = weighted_values_and_sum(jnp.exp2(s - m), v)
        return lanes(m), l, acc

    def next_tile(q, k, v, bias, m_old, l_old, acc_old):
        s = _dot_nt(q, k) + bias
        m_new = jnp.maximum(m_old, lanes(jnp.max(s, axis=-1, keepdims=True)))
        alpha = jnp.exp2(m_old - m_new)
        acc, l = weighted_values_and_sum(jnp.exp2(s - across_keys(m_new, s.shape[1])), v)
        return m_new, alpha * l_old + l, alpha * acc_old + acc

    def group16(i, carry):
        bias_first, bias = b16f_ref[...], b16_ref[...]
        rows, res = [], []
        for u in range(TILES_16):
            base = (TILES_16 * i + u) * sub
            for j in range(sub // ATTN_STEPS):
                q_rows = pl.ds(pl.multiple_of(base + j * ATTN_STEPS, ATTN_STEPS), ATTN_STEPS)
                if j == 0:
                    k_rows, b = q_rows, bias_first
                else:
                    k_rows = pl.ds(pl.multiple_of(base + (j - 1) * ATTN_STEPS, ATTN_STEPS),
                                   2 * ATTN_STEPS)
                    b = bias
                rows.append(q_rows)
                res.append(first_tile(q_ref[q_rows, :], k_ref[k_rows, :], v_ref[k_rows, :], b))
        for r, (m, l, acc) in zip(rows, res):
            m_ref[r, :] = m
            l_ref[r, :] = l
            acc_ref[r, :] = acc
        return carry

    lax.fori_loop(0, DIL_MAX // TILES_16, group16, 0)

    blocks_4 = min(BLOCKS_4, sub // 32)

    def group4(i, carry):
        starts, res = [], []
        for u in range(blocks_4):
            blk = blocks_4 * i + u
            bias = b4_ref[jnp.where(blk == 0, 0, 1)]
            k_blk = jnp.maximum(blk - 1, 0)
            for r4 in range(4):
                qs = [pl.multiple_of((r4 + 4 * c) * sub + 32 * blk, 32) for c in range(4)]
                ks = [pl.multiple_of((r4 + 4 * c) * sub + 32 * k_blk, 32) for c in range(4)]
                starts.append(qs)
                res.append(next_tile(
                    gather(q_ref, qs, 32), gather(k_ref, ks, 64), gather(v_ref, ks, 64), bias,
                    gather(m_ref, qs, 32), gather(l_ref, qs, 32), gather(acc_ref, qs, 32)))
        for qs, (m, l, acc) in zip(starts, res):
            scatter(m_ref, qs, 32, m)
            scatter(l_ref, qs, 32, l)
            scatter(acc_ref, qs, 32, acc)
        return carry

    lax.fori_loop(0, sub // (32 * blocks_4), group4, 0)

    blocks_1 = min(BLOCKS_1, sub // 16)

    def group1(i, carry):
        res = []
        for u in range(blocks_1):
            blk = blocks_1 * i + u
            bias = b1_ref[jnp.where(blk == 0, 0, 1)]
            k_lo = jnp.maximum(16 * blk - 8, 0)
            qs = [pl.multiple_of(r * sub + 16 * blk, 16) for r in range(DIL_MAX)]
            ks = [pl.multiple_of(r * sub + k_lo, 8) for r in range(DIL_MAX)]
            _, l, acc = next_tile(
                gather(q_ref, qs, 16), gather(k_ref, ks, 24), gather(v_ref, ks, 24), bias,
                gather(m_ref, qs, 16), gather(l_ref, qs, 16), gather(acc_ref, qs, 16))
            res.append((qs, (acc / l).astype(o_ref.dtype)))
        for qs, out in res:
            scatter(o_ref, qs, 16, out)
        return carry

    lax.fori_loop(0, sub // (16 * blocks_1), group1, 0)


def _dilated_attention(qkv, slopes, *, batch, seq, dh):
    n = qkv.shape[0]
    sub = seq // DIL_MAX
    b1, b4, b16_first, b16 = _attn_bias_tables(slopes, sub)
    kern = functools.partial(_attn_kernel, sub=sub)
    return pl.pallas_call(
        kern,
        grid=(batch, ATTN_HEADS),
        in_specs=[
            pl.BlockSpec((seq, dh), lambda b, h: (b, h)),
            pl.BlockSpec((seq, dh), lambda b, h: (b, ATTN_HEADS + h)),
            pl.BlockSpec((seq, dh), lambda b, h: (b, 2 * ATTN_HEADS + h)),
            pl.BlockSpec((None,) + b1.shape[1:], lambda b, h: (h, 0, 0, 0)),
            pl.BlockSpec((None,) + b4.shape[1:], lambda b, h: (h, 0, 0, 0)),
            pl.BlockSpec((None,) + b16_first.shape[1:], lambda b, h: (h, 0, 0)),
            pl.BlockSpec((None,) + b16.shape[1:], lambda b, h: (h, 0, 0)),
        ],
        out_specs=pl.BlockSpec((seq, dh), lambda b, h: (b, h)),
        out_shape=jax.ShapeDtypeStruct((n, ATTN_HEADS * dh), BF16),
        scratch_shapes=[pltpu.VMEM((seq, LANES), F32), pltpu.VMEM((seq, LANES), F32),
                        pltpu.VMEM((seq, dh), F32)],
        compiler_params=_params("parallel", "parallel"),
        name="dilated_attn",
    )(qkv, qkv, qkv, b1, b4, b16_first, b16)


def kernel(x, norm_mix_g, norm_mlp_g, final_norm_g, hyb_w_in, conv_w, gla_w_gate2, gla_b_gate,
           gla_norm_g, hyb_w_out, attn_w_qkv, attn_w_o, mlp_w1, mlp_w2):
    batch, seq, d = x.shape
    n = batch * seq
    depth = norm_mix_g.shape[0]
    conv_ch = conv_w.shape[-1]
    dv = gla_norm_g.shape[-1]
    dk = gla_w_gate2.shape[-1] // GLA_HEADS
    dh = d // ATTN_HEADS
    assert depth == 2 and seq % (DIL_MAX * ATTN_STEPS) == 0 and seq % GLA_CHUNK == 0
    assert dk == LANES and dh == LANES and dv % LANES == 0

    row = lambda v: v.reshape(1, -1).astype(F32)
    xf = x.reshape(n, d)

    main_cols = 3 * conv_ch + 2 * GLA_HEADS * dk + 2 * GLA_HEADS * dv
    w_in = hyb_w_in[0]
    w_low = jnp.pad(w_in[:, main_cols:], ((0, 0), (0, LANES - GLA_GATE_RANK))).astype(BF16)
    w_gate = jnp.pad(gla_w_gate2[0], ((0, LANES - GLA_GATE_RANK), (0, 0))).astype(BF16)
    proj, log_a = _in_proj_gla(xf, row(norm_mix_g[0]), w_in.astype(BF16), w_low, w_gate,
                               row(gla_b_gate[0]), cols=main_cols)
    w1_layers, w2_layers = mlp_w1.astype(BF16), mlp_w2.astype(BF16)

    q_off = 3 * conv_ch
    k_off = q_off + GLA_HEADS * dk
    v_off = k_off + GLA_HEADS * dk
    r_off = v_off + GLA_HEADS * dv
    tri = jnp.asarray(np.tril(np.ones((GLA_CHUNK, GLA_CHUNK), np.float32)), BF16)
    o_gla = _gla(proj, log_a, tri, row(gla_norm_g[0]), batch=batch, seq=seq,
                 q_off=q_off, k_off=k_off, v_off=v_off, r_off=r_off, dk=dk, dv=dv)
    xf = _conv_out_proj_residual(proj, conv_w[0].astype(F32), o_gla, hyb_w_out[0].astype(BF16), xf,
                                 seq=seq)
    xf = _mlp_residual(xf, row(norm_mlp_g[0]), w1_layers, w2_layers, 0, row(final_norm_g),
                       final_norm=False)

    col_scale = jnp.concatenate([jnp.full((1, d), float(dh) ** -0.5 * LOG2_E, F32),
                                 jnp.ones((1, 2 * d), F32)], axis=1)
    qkv = _qkv_proj_residue_major(xf, row(norm_mix_g[1]), attn_w_qkv[0].astype(BF16), col_scale,
                                  batch=batch, seq=seq)
    slopes = jnp.exp2(-8.0 * jnp.arange(1, ATTN_HEADS + 1, dtype=F32) / ATTN_HEADS)
    attn = _dilated_attention(qkv, slopes, batch=batch, seq=seq, dh=dh)
    xf = _attn_out_proj_residual(attn, attn_w_o[0].astype(BF16), xf, batch=batch, seq=seq)
    out = _mlp_residual(xf, row(norm_mlp_g[1]), w1_layers, w2_layers, 1, row(final_norm_g),
                        final_norm=True)
    return out.reshape(batch, seq, d)
```

```python
import functools

import jax
import jax.numpy as jnp
import numpy as np
from jax import lax
from jax.experimental import pallas as pl
from jax.experimental.pallas import tpu as pltpu

F32 = jnp.float32
BF16 = jnp.bfloat16

NORM_EPS = 1e-6
CONV_K = 3
GLA_HEADS = 4
GLA_GATE_RANK = 16
GLA_GATE_TEMP = 16.0
ATTN_HEADS = 16
ATTN_HEADS_PER_STEP = 2
DILATED_GROUPS = ((128, 1), (512, 4), (2048, 16))
DIL_MAX = 16
ATTN_STEPS = 128

LANES = 128
GLA_CHUNK = 256
GLA_SUB = 16
GLA_BLOCK = 64
GLA_MAX_BLOCK_DECAY = 60.0
MASK_VALUE = -1e30
LOG2_E = 1.4426950408889634
TILES_16 = 16
BLOCKS_4 = 8
BLOCKS_1 = 16
VMEM_LIMIT_BYTES = 56 * 1024 * 1024


def _params(*semantics):
    return pltpu.CompilerParams(dimension_semantics=semantics, vmem_limit_bytes=VMEM_LIMIT_BYTES)


def _dot(a, b):
    return jnp.dot(a, b, preferred_element_type=F32)


def _dot_nt(a, b):
    return lax.dot_general(a, b, (((1,), (1,)), ((), ())), preferred_element_type=F32)


def _dot_tn(a, b):
    return lax.dot_general(a, b, (((0,), (0,)), ((), ())), preferred_element_type=F32)


def _rmsnorm(x, g):
    return x * lax.rsqrt(jnp.mean(x * x, axis=-1, keepdims=True) + NORM_EPS) * g


def _tile_of_trip(i):
    return jnp.maximum(i - 1, 0)


def _chunk_index(i, j, n_tiles, chunks):
    return jnp.minimum(i, n_tiles - 1) * chunks + j


def _in_proj_gla_kernel(xc_ref, g_ref, w_ref, wlow_ref, wgate_ref, bgate_ref, o_ref, loga_ref,
                        xn_ref):
    i, j = pl.program_id(0), pl.program_id(1)
    ch = xc_ref.shape[0]
    fill = i % 2
    rows = pl.ds(pl.multiple_of(j * ch, ch), ch)

    def norm_and_gate_chunk():
        xn = _rmsnorm(xc_ref[...], g_ref[...]).astype(BF16)
        xn_ref[fill, rows, :] = xn
        g_low = _dot(xn, wlow_ref[...]).astype(BF16)
        gate = _dot(g_low, wgate_ref[...]) + bgate_ref[...]
        log_sig = jnp.minimum(gate, 0.0) - jnp.log(1.0 + jnp.exp(-jnp.abs(gate)))
        loga_ref[rows, :] = log_sig * (1.0 / GLA_GATE_TEMP)

    @pl.when(i == 0)
    def _():
        norm_and_gate_chunk()

    @pl.when(i > 0)
    def _():
        norm_and_gate_chunk()
        o_ref[...] = _dot(xn_ref[1 - fill], w_ref[...]).astype(o_ref.dtype)


def _in_proj_gla(x, gain, w, w_low, w_gate, b_gate, *, cols, tm=1024, tn=1536):
    n, d = x.shape
    n_gate = w_gate.shape[1]
    n_tiles, steps = n // tm, cols // tn
    ch = tm // steps
    return pl.pallas_call(
        _in_proj_gla_kernel,
        grid=(n_tiles + 1, steps),
        in_specs=[
            pl.BlockSpec((ch, d), lambda i, j: (_chunk_index(i, j, n_tiles, steps), 0)),
            pl.BlockSpec((1, d), lambda i, j: (0, 0)),
            pl.BlockSpec((d, tn), lambda i, j: (0, j)),
            pl.BlockSpec((d, LANES), lambda i, j: (0, 0)),
            pl.BlockSpec((LANES, n_gate), lambda i, j: (0, 0)),
            pl.BlockSpec((1, n_gate), lambda i, j: (0, 0)),
        ],
        out_specs=[
            pl.BlockSpec((tm, tn), lambda i, j: (_tile_of_trip(i), j * jnp.minimum(i, 1))),
            pl.BlockSpec((tm, n_gate), lambda i, j: (jnp.minimum(i, n_tiles - 1), 0)),
        ],
        out_shape=[
            jax.ShapeDtypeStruct((n, cols), BF16),
            jax.ShapeDtypeStruct((n, n_gate), F32),
        ],
        scratch_shapes=[pltpu.VMEM((2, tm, d), BF16)],
        compiler_params=_params("arbitrary", "arbitrary"),
        name="in_proj_gla",
    )(x, gain, w, w_low, w_gate, b_gate)


def _qkv_proj_kernel(xc_ref, g_ref, w_ref, scale_ref, o_ref, xn_ref):
    i, j = pl.program_id(0), pl.program_id(1)
    ch, d = xc_ref.shape
    per = ch // DIL_MAX
    fill = i % 2

    def norm_chunk():
        xn = _rmsnorm(xc_ref[...], g_ref[...]).astype(BF16)
        xn_ref[fill, :, pl.ds(pl.multiple_of(j * per, per), per), :] = (
            jnp.swapaxes(xn.reshape(per, DIL_MAX, d), 0, 1))

    @pl.when(i == 0)
    def _():
        norm_chunk()

    @pl.when(i > 0)
    def _():
        norm_chunk()
        xn = xn_ref[1 - fill]
        acc = _dot(xn.reshape(xn.shape[0] * xn.shape[1], d), w_ref[...]) * scale_ref[...]
        o_ref[...] = acc.astype(o_ref.dtype).reshape(o_ref.shape)


def _qkv_proj_residue_major(x, gain, w, col_scale, *, batch, seq, tm=1024, tn=1536):
    n, d = x.shape
    cols = w.shape[1]
    sub = seq // DIL_MAX
    per = tm // DIL_MAX
    tiles_per_seq = seq // tm
    n_tiles, steps = n // tm, cols // tn
    ch = tm // steps
    assert ch % (DIL_MAX * 16) == 0

    def out_index(i, j):
        t = _tile_of_trip(i)
        return (t // tiles_per_seq, 0, t % tiles_per_seq, j * jnp.minimum(i, 1))

    out = pl.pallas_call(
        _qkv_proj_kernel,
        grid=(n_tiles + 1, steps),
        in_specs=[
            pl.BlockSpec((ch, d), lambda i, j: (_chunk_index(i, j, n_tiles, steps), 0)),
            pl.BlockSpec((1, d), lambda i, j: (0, 0)),
            pl.BlockSpec((d, tn), lambda i, j: (0, j)),
            pl.BlockSpec((1, tn), lambda i, j: (0, j)),
        ],
        out_specs=pl.BlockSpec((None, DIL_MAX, per, tn), out_index),
        out_shape=jax.ShapeDtypeStruct((batch, DIL_MAX, sub, cols), BF16),
        scratch_shapes=[pltpu.VMEM((2, DIL_MAX, per, d), BF16)],
        compiler_params=_params("arbitrary", "arbitrary"),
        name="qkv_proj",
    )(x, gain, w, col_scale)
    return out.reshape(n, cols)


def _conv_out_proj_kernel(ab_ref, ac_ref, ax_ref, cw_ref, gla_ref, wa_ref, wb_ref, x_ref, o_ref,
                          ubuf_ref, *, tiles_per_seq):
    tb = ab_ref.shape[0]
    pad = 8

    @pl.when(pl.program_id(0) % tiles_per_seq == 0)
    def _():
        ubuf_ref[0:pad, :] = jnp.zeros((pad, ubuf_ref.shape[1]), F32)

    acc = x_ref[...] + _dot(gla_ref[...], wb_ref[...])
    u = ac_ref[...].astype(F32) * ax_ref[...].astype(F32)
    ubuf_ref[pad:pad + tb, :] = u
    u1 = ubuf_ref[pad - 1:pad - 1 + tb, :]
    u2 = ubuf_ref[pad - 2:pad - 2 + tb, :]
    cw = cw_ref[...]
    y = ab_ref[...].astype(F32) * (cw[0:1, :] * u2 + cw[1:2, :] * u1 + cw[2:3, :] * u)
    o_ref[...] = acc + _dot(y.astype(BF16), wa_ref[...])
    ubuf_ref[0:pad, :] = ubuf_ref[tb:tb + pad, :]


def _conv_out_proj_residual(proj, conv_w, o_gla, w_out, x, *, seq, tm=512):
    n, d = x.shape
    ch = conv_w.shape[1]
    assert w_out.shape[0] == 2 * ch and o_gla.shape[1] == ch
    kern = functools.partial(_conv_out_proj_kernel, tiles_per_seq=seq // tm)
    return pl.pallas_call(
        kern,
        grid=(n // tm,),
        in_specs=[
            pl.BlockSpec((tm, ch), lambda i: (i, 0)),
            pl.BlockSpec((tm, ch), lambda i: (i, 1)),
            pl.BlockSpec((tm, ch), lambda i: (i, 2)),
            pl.BlockSpec((CONV_K, ch), lambda i: (0, 0)),
            pl.BlockSpec((tm, o_gla.shape[1]), lambda i: (i, 0)),
            pl.BlockSpec((ch, d), lambda i: (0, 0)),
            pl.BlockSpec((ch, d), lambda i: (1, 0)),
            pl.BlockSpec((tm, d), lambda i: (i, 0)),
        ],
        out_specs=pl.BlockSpec((tm, d), lambda i: (i, 0)),
        out_shape=jax.ShapeDtypeStruct((n, d), F32),
        scratch_shapes=[pltpu.VMEM((tm + 8, ch), F32)],
        compiler_params=_params("arbitrary"),
        name="conv_out_proj",
    )(proj, proj, proj, conv_w, o_gla, w_out, w_out, x)


def _gla_scores_factored(q, k, g):
    c_len = q.shape[0]
    score_rows = []
    for a in range(c_len // GLA_BLOCK):
        lo, hi = a * GLA_BLOCK, (a + 1) * GLA_BLOCK
        n_keys = LANES * -(-hi // LANES)
        g_first = g[lo:lo + 1, :]
        q_a = (q[lo:hi] * jnp.exp(g[lo:hi] - g_first)).astype(BF16)
        k_a = (k[:n_keys] * jnp.exp(jnp.minimum(g_first - g[:n_keys], GLA_MAX_BLOCK_DECAY)))
        s_a = _dot_nt(q_a, k_a.astype(BF16))
        row = lax.broadcasted_iota(jnp.int32, s_a.shape, 0) + lo
        col = lax.broadcasted_iota(jnp.int32, s_a.shape, 1)
        s_a = jnp.where(col <= row, s_a, 0.0)
        if n_keys < c_len:
            s_a = jnp.concatenate([s_a, jnp.zeros((GLA_BLOCK, c_len - n_keys), F32)], axis=1)
        score_rows.append(s_a)
    return jnp.concatenate(score_rows, axis=0)


def _gla_scores_pairwise(q, k, g):
    c_len = q.shape[0]
    n_sub = c_len // GLA_SUB
    score_rows = [jnp.zeros((GLA_SUB, c_len), F32)]
    for a in range(1, n_sub):
        lo = a * GLA_SUB
        n_keys = LANES * -(-lo // LANES)
        g_ref_row = g[lo:lo + 1, :]
        q_a = (q[lo:lo + GLA_SUB] * jnp.exp(g[lo:lo + GLA_SUB] - g_ref_row)).astype(BF16)
        k_a = (k[:n_keys] * jnp.exp(jnp.minimum(g_ref_row - g[:n_keys], 0.0))).astype(BF16)
        s_a = _dot_nt(q_a, k_a)
        col = lax.broadcasted_iota(jnp.int32, s_a.shape, 1)
        s_a = jnp.where(col < lo, s_a, 0.0)
        if n_keys < c_len:
            s_a = jnp.concatenate([s_a, jnp.zeros((GLA_SUB, c_len - n_keys), F32)], axis=1)
        score_rows.append(s_a)
    scores = jnp.concatenate(score_rows, axis=0)

    row = lax.broadcasted_iota(jnp.int32, (c_len, LANES), 0)
    lane = lax.broadcasted_iota(jnp.int32, (c_len, LANES), 1)
    delta = row % LANES - lane
    band_id = jnp.where((delta >= 0) & (delta <= row % GLA_SUB), delta, -1)
    band = jnp.zeros((c_len, LANES), F32)
    for d in range(GLA_SUB):
        k_d = k if d == 0 else pltpu.roll(k, d, 0)
        g_d = g if d == 0 else pltpu.roll(g, d, 0)
        e = jnp.exp(jnp.minimum(g - g_d, 0.0))
        diag = jnp.sum(q * k_d * e, axis=-1, keepdims=True)
        band = jnp.where(band_id == d, diag, band)
    zeros = jnp.zeros((LANES, LANES), F32)
    band_rows = []
    for t in range(c_len // LANES):
        tiles = [zeros] * (c_len // LANES)
        tiles[t] = band[t * LANES:(t + 1) * LANES]
        band_rows.append(jnp.concatenate(tiles, axis=1))
    return scores + jnp.concatenate(band_rows, axis=0)


def _gla_kernel(q_ref, k_ref, v_ref, r_ref, la_ref, tri_ref, gn_ref, o_ref, st_ref, g_ref,
                inter_ref, *, q_scale, dk, dv):
    c_len = q_ref.shape[0]

    @pl.when(pl.program_id(1) == 0)
    def _():
        st_ref[...] = jnp.zeros_like(st_ref)

    la = la_ref[...]
    tri = tri_ref[...]
    la_hi = la.astype(BF16)
    rem = la - la_hi.astype(F32)
    la_mid = rem.astype(BF16)
    la_lo = (rem - la_mid.astype(F32)).astype(BF16)
    g_all = _dot(tri, la_hi) + _dot(tri, la_mid) + _dot(tri, la_lo)
    g_ref[...] = g_all
    block_decay = [g_all[lo:lo + 1, :] - g_all[lo + GLA_BLOCK - 1:lo + GLA_BLOCK, :]
                   for lo in range(0, c_len, GLA_BLOCK)]
    worst_decay = jnp.max(jnp.concatenate(block_decay, axis=0))

    def head_operands(h):
        g = g_ref[:, h * dk:(h + 1) * dk]
        q = q_ref[:, h * dk:(h + 1) * dk].astype(F32) * q_scale
        k = k_ref[:, h * dk:(h + 1) * dk].astype(F32)
        return g, q, k, v_ref[:, h * dv:(h + 1) * dv]

    def finish_head(h, o):
        o = _rmsnorm(o, gn_ref[...])
        r = r_ref[:, h * dv:(h + 1) * dv].astype(F32)
        o_ref[:, h * dv:(h + 1) * dv] = (o * (r / (1.0 + jnp.exp(-r)))).astype(o_ref.dtype)

    for h in range(GLA_HEADS):
        g, q, k, v = head_operands(h)
        g_last = g[c_len - 1:c_len, :]
        st = st_ref[h]
        o_inter = _dot_nt((q * jnp.exp(g)).astype(BF16), st.astype(BF16))
        inter_ref[h] = o_inter
        k_dec = (k * jnp.exp(g_last - g)).astype(BF16)
        st_ref[h] = st * jnp.exp(g_last) + _dot_tn(v, k_dec)
        finish_head(h, o_inter + _dot(_gla_scores_factored(q, k, g).astype(BF16), v))

    @pl.when(worst_decay > GLA_MAX_BLOCK_DECAY)
    def _():
        for h in range(GLA_HEADS):
            g, q, k, v = head_operands(h)
            finish_head(h, inter_ref[h] + _dot(_gla_scores_pairwise(q, k, g).astype(BF16), v))


def _gla(proj, log_a, tri, norm_g, *, batch, seq, q_off, k_off, v_off, r_off, dk, dv):
    n = proj.shape[0]
    cpb = seq // GLA_CHUNK
    qk_w, vr_w = GLA_HEADS * dk, GLA_HEADS * dv
    row = lambda b, c: b * cpb + c
    kern = functools.partial(_gla_kernel, q_scale=float(dk) ** -0.5, dk=dk, dv=dv)
    return pl.pallas_call(
        kern,
        grid=(batch, cpb),
        in_specs=[
            pl.BlockSpec((GLA_CHUNK, qk_w), lambda b, c: (row(b, c), q_off // qk_w)),
            pl.BlockSpec((GLA_CHUNK, qk_w), lambda b, c: (row(b, c), k_off // qk_w)),
            pl.BlockSpec((GLA_CHUNK, vr_w), lambda b, c: (row(b, c), v_off // vr_w)),
            pl.BlockSpec((GLA_CHUNK, vr_w), lambda b, c: (row(b, c), r_off // vr_w)),
            pl.BlockSpec((GLA_CHUNK, qk_w), lambda b, c: (row(b, c), 0)),
            pl.BlockSpec((GLA_CHUNK, GLA_CHUNK), lambda b, c: (0, 0)),
            pl.BlockSpec((1, dv), lambda b, c: (0, 0)),
        ],
        out_specs=pl.BlockSpec((GLA_CHUNK, vr_w), lambda b, c: (row(b, c), 0)),
        out_shape=jax.ShapeDtypeStruct((n, vr_w), BF16),
        scratch_shapes=[pltpu.VMEM((GLA_HEADS, dv, dk), F32), pltpu.VMEM((GLA_CHUNK, qk_w), F32),
                        pltpu.VMEM((GLA_HEADS, GLA_CHUNK, dv), F32)],
        compiler_params=_params("parallel", "arbitrary"),
        name="gla",
    )(proj, proj, proj, proj, log_a, tri, norm_g)


def _attn_out_proj_kernel(a_ref, w_ref, x_ref, o_ref):
    tm, d = x_ref.shape
    acc = _dot(a_ref[...].reshape(tm, a_ref.shape[-1]), w_ref[...])
    acc = jnp.swapaxes(acc.reshape(DIL_MAX, tm // DIL_MAX, d), 0, 1).reshape(tm, d)
    o_ref[...] = x_ref[...] + acc


def _attn_out_proj_residual(attn, w, x, *, batch, seq, tm=512):
    n, d = x.shape
    per = tm // DIL_MAX
    tps = seq // tm
    attn = attn.reshape(batch, DIL_MAX, seq // DIL_MAX, attn.shape[1])
    return pl.pallas_call(
        _attn_out_proj_kernel,
        grid=(n // tm,),
        in_specs=[
            pl.BlockSpec((None, DIL_MAX, per, attn.shape[-1]), lambda i: (i // tps, 0, i % tps, 0)),
            pl.BlockSpec(w.shape, lambda i: (0, 0)),
            pl.BlockSpec((tm, d), lambda i: (i, 0)),
        ],
        out_specs=pl.BlockSpec((tm, d), lambda i: (i, 0)),
        out_shape=jax.ShapeDtypeStruct((n, d), F32),
        compiler_params=_params("parallel"),
        name="attn_out_proj",
    )(attn, w, x)


def _mlp_kernel(x_ref, g_ref, w1_ref, w2_ref, gf_ref, o_ref, xn_ref, *, final_norm):
    f = pl.program_id(1)

    @pl.when(f == 0)
    def _():
        x = x_ref[...]
        xn_ref[...] = _rmsnorm(x, g_ref[...]).astype(BF16)
        o_ref[...] = x

    h = _dot(xn_ref[...], w1_ref[...])
    h = jnp.square(jnp.maximum(h, 0.0)).astype(BF16)
    o_ref[...] += _dot(h, w2_ref[...])

    if final_norm:
        @pl.when(f == pl.num_programs(1) - 1)
        def _():
            o_ref[...] = _rmsnorm(o_ref[...], gf_ref[...])


def _mlp_residual(x, gain, w1_layers, w2_layers, layer, final_gain, *, final_norm, tm=512,
                  tf=1024):
    n, d = x.shape
    d_ff = w1_layers.shape[2]
    kern = functools.partial(_mlp_kernel, final_norm=final_norm)
    return pl.pallas_call(
        kern,
        grid=(n // tm, d_ff // tf),
        in_specs=[
            pl.BlockSpec((tm, d), lambda i, f: (i, 0)),
            pl.BlockSpec((1, d), lambda i, f: (0, 0)),
            pl.BlockSpec((None, d, tf), lambda i, f: (layer, 0, f)),
            pl.BlockSpec((None, tf, d), lambda i, f: (layer, f, 0)),
            pl.BlockSpec((1, d), lambda i, f: (0, 0)),
        ],
        out_specs=pl.BlockSpec((tm, d), lambda i, f: (i, 0)),
        out_shape=jax.ShapeDtypeStruct((n, d), F32),
        scratch_shapes=[pltpu.VMEM((tm, d), BF16)],
        compiler_params=_params("parallel", "arbitrary"),
        name="mlp",
    )(x, gain, w1_layers, w2_layers, final_gain)


def _attn_bias_tables(slopes, sub):
    def table(q_pos, k_pos, dilation):
        diff = q_pos[:, None] - k_pos[None, :]
        valid = jnp.asarray((diff >= 0) & (diff <= ATTN_STEPS))
        dist = jnp.asarray((dilation * diff).astype(np.float32))
        bias = -(slopes * LOG2_E)[:, None, None] * dist[None]
        return jnp.where(valid[None], bias, MASK_VALUE).astype(F32)

    q = np.arange(128)
    b16_first = table(q, q, 16)
    b16 = table(q + 128, np.arange(256), 16)

    q4 = 4 * (q % 32) + q // 32
    k = np.arange(256)
    k4 = 4 * (k % 64) + k // 64
    b4 = jnp.stack([table(q4, k4, 4), table(q4 + 128, k4, 4)], axis=1)

    q = np.arange(256)
    q1 = 16 * (q % 16) + q // 16
    k = np.arange(384)
    k1 = 16 * (k % 24) + k // 24
    b1 = jnp.stack([table(q1, k1, 1), table(q1 + 128, k1, 1)], axis=1)
    return b1, b4, b16_first, b16


def _attn_kernel(q_all, k_all, v_all, b1_all, b4_all, b16f_all, b16_all, o_all, *scratch, sub,
                 heads):
    def gather(ref, starts, size):
        return jnp.concatenate([ref[pl.ds(s, size), :] for s in starts], axis=0)

    def scatter(ref, starts, size, val):
        for c, s in enumerate(starts):
            ref[pl.ds(s, size), :] = val[c * size:(c + 1) * size]

    def lanes(x):
        return jnp.broadcast_to(x, (x.shape[0], LANES))

    def across_keys(x, n_keys):
        return jnp.concatenate([x] * (n_keys // LANES), axis=1)

    def weighted_values_and_sum(p, v):
        pv = _dot(p.astype(BF16), jnp.concatenate([v, jnp.ones_like(v)], axis=1))
        return pv[:, :LANES], pv[:, LANES:]

    def first_tile(q, k, v, bias):
        s = _dot_nt(q, k) + bias
        m = jnp.max(s, axis=-1, keepdims=True)
        acc, l = weighted_values_and_sum(jnp.exp2(s - m), v)
        return lanes(m), l, acc

    def next_tile(q, k, v, bias, m_old, l_old, acc_old):
        s = _dot_nt(q, k) + bias
        m_new = jnp.maximum(m_old, lanes(jnp.max(s, axis=-1, keepdims=True)))
        alpha = jnp.exp2(m_old - m_new)
        acc, l = weighted_values_and_sum(jnp.exp2(s - across_keys(m_new, s.shape[1])), v)
        return m_new, alpha * l_old + l, alpha * acc_old + acc

    dh = q_all.shape[1] // heads

    def loop(trips, body):
        if trips == 1:
            body(0, 0)
        else:
            lax.fori_loop(0, trips, body, 0)

    blocks_4 = min(BLOCKS_4, sub // 32)
    blocks_1 = min(BLOCKS_1, sub // 16)
    for group in (16, 4, 1):
        for h in range(heads):
            cols = slice(h * dh, (h + 1) * dh)
            head_refs = (q_all.at[:, cols], k_all.at[:, cols], v_all.at[:, cols], o_all.at[:, cols],
                         b1_all.at[h], b4_all.at[h], b16f_all.at[h], b16_all.at[h],
                         *scratch[3 * h:3 * h + 3])
            _attn_group(group, head_refs, sub, blocks_4, blocks_1, loop, first_tile, next_tile,
                        gather, scatter)


def _attn_group(group, head_refs, sub, blocks_4, blocks_1, loop, first_tile, next_tile, gather,
                scatter):
    (q_ref, k_ref, v_ref, o_ref, b1_ref, b4_ref, b16f_ref, b16_ref, m_ref, l_ref, acc_ref) = head_refs

    def group16(i, carry):
        bias_first, bias = b16f_ref[...], b16_ref[...]
        rows, res = [], []
        for u in range(TILES_16):
            base = (TILES_16 * i + u) * sub
            for j in range(sub // ATTN_STEPS):
                q_rows = pl.ds(pl.multiple_of(base + j * ATTN_STEPS, ATTN_STEPS), ATTN_STEPS)
                if j == 0:
                    k_rows, b = q_rows, bias_first
                else:
                    k_rows = pl.ds(pl.multiple_of(base + (j - 1) * ATTN_STEPS, ATTN_STEPS),
                                   2 * ATTN_STEPS)
                    b = bias
                rows.append(q_rows)
                res.append(first_tile(q_ref[q_rows, :], k_ref[k_rows, :], v_ref[k_rows, :], b))
        for r, (m, l, acc) in zip(rows, res):
            m_ref[r, :] = m
            l_ref[r, :] = l
            acc_ref[r, :] = acc
        return carry

    def group4(i, carry):
        starts, res = [], []
        for u in range(blocks_4):
            blk = blocks_4 * i + u
            bias = b4_ref[jnp.where(blk == 0, 0, 1)]
            k_blk = jnp.maximum(blk - 1, 0)
            for r4 in range(4):
                qs = [pl.multiple_of((r4 + 4 * c) * sub + 32 * blk, 32) for c in range(4)]
                ks = [pl.multiple_of((r4 + 4 * c) * sub + 32 * k_blk, 32) for c in range(4)]
                starts.append(qs)
                res.append(next_tile(
                    gather(q_ref, qs, 32), gather(k_ref, ks, 64), gather(v_ref, ks, 64), bias,
                    gather(m_ref, qs, 32), gather(l_ref, qs, 32), gather(acc_ref, qs, 32)))
        for qs, (m, l, acc) in zip(starts, res):
            scatter(m_ref, qs, 32, m)
            scatter(l_ref, qs, 32, l)
            scatter(acc_ref, qs, 32, acc)
        return carry

    def group1(i, carry):
        res = []
        for u in range(blocks_1):
            blk = blocks_1 * i + u
            bias = b1_ref[jnp.where(blk == 0, 0, 1)]
            k_lo = jnp.maximum(16 * blk - 8, 0)
            qs = [pl.multiple_of(r * sub + 16 * blk, 16) for r in range(DIL_MAX)]
            ks = [pl.multiple_of(r * sub + k_lo, 8) for r in range(DIL_MAX)]
            _, l, acc = next_tile(
                gather(q_ref, qs, 16), gather(k_ref, ks, 24), gather(v_ref, ks, 24), bias,
                gather(m_ref, qs, 16), gather(l_ref, qs, 16), gather(acc_ref, qs, 16))
            res.append((qs, (acc / l).astype(o_ref.dtype)))
        for qs, out in res:
            scatter(o_ref, qs, 16, out)
        return carry

    if group == 16:
        loop(DIL_MAX // TILES_16, group16)
    elif group == 4:
        loop(sub // (32 * blocks_4), group4)
    else:
        loop(sub // (16 * blocks_1), group1)


def _dilated_attention(qkv, slopes, *, batch, seq, dh):
    n = qkv.shape[0]
    sub = seq // DIL_MAX
    b1, b4, b16_first, b16 = _attn_bias_tables(slopes, sub)
    hp = ATTN_HEADS_PER_STEP
    groups = ATTN_HEADS // hp
    kern = functools.partial(_attn_kernel, sub=sub, heads=hp)
    return pl.pallas_call(
        kern,
        grid=(batch, groups),
        in_specs=[
            pl.BlockSpec((seq, hp * dh), lambda b, h: (b, h)),
            pl.BlockSpec((seq, hp * dh), lambda b, h: (b, groups + h)),
            pl.BlockSpec((seq, hp * dh), lambda b, h: (b, 2 * groups + h)),
            pl.BlockSpec((hp,) + b1.shape[1:], lambda b, h: (h, 0, 0, 0)),
            pl.BlockSpec((hp,) + b4.shape[1:], lambda b, h: (h, 0, 0, 0)),
            pl.BlockSpec((hp,) + b16_first.shape[1:], lambda b, h: (h, 0, 0)),
            pl.BlockSpec((hp,) + b16.shape[1:], lambda b, h: (h, 0, 0)),
        ],
        out_specs=pl.BlockSpec((seq, hp * dh), lambda b, h: (b, h)),
        out_shape=jax.ShapeDtypeStruct((n, ATTN_HEADS * dh), BF16),
        scratch_shapes=[pltpu.VMEM((seq, LANES), F32), pltpu.VMEM((seq, LANES), F32),
                        pltpu.VMEM((seq, dh), F32)] * hp,
        compiler_params=_params("parallel", "parallel"),
        name="dilated_attn",
    )(qkv, qkv, qkv, b1, b4, b16_first, b16)


def kernel(x, norm_mix_g, norm_mlp_g, final_norm_g, hyb_w_in, conv_w, gla_w_gate2, gla_b_gate,
           gla_norm_g, hyb_w_out, attn_w_qkv, attn_w_o, mlp_w1, mlp_w2):
    batch, seq, d = x.shape
    n = batch * seq
    depth = norm_mix_g.shape[0]
    conv_ch = conv_w.shape[-1]
    dv = gla_norm_g.shape[-1]
    dk = gla_w_gate2.shape[-1] // GLA_HEADS
    dh = d // ATTN_HEADS
    assert depth == 2 and seq % (DIL_MAX * ATTN_STEPS) == 0 and seq % GLA_CHUNK == 0
    assert dk == LANES and dh == LANES and dv % LANES == 0

    row = lambda v: v.reshape(1, -1).astype(F32)
    xf = x.reshape(n, d)

    main_cols = 3 * conv_ch + 2 * GLA_HEADS * dk + 2 * GLA_HEADS * dv
    w_in = hyb_w_in[0]
    w_low = jnp.pad(w_in[:, main_cols:], ((0, 0), (0, LANES - GLA_GATE_RANK))).astype(BF16)
    w_gate = jnp.pad(gla_w_gate2[0], ((0, LANES - GLA_GATE_RANK), (0, 0))).astype(BF16)
    proj, log_a = _in_proj_gla(xf, row(norm_mix_g[0]), w_in.astype(BF16), w_low, w_gate,
                               row(gla_b_gate[0]), cols=main_cols)
    w1_layers, w2_layers = mlp_w1.astype(BF16), mlp_w2.astype(BF16)

    q_off = 3 * conv_ch
    k_off = q_off + GLA_HEADS * dk
    v_off = k_off + GLA_HEADS * dk
    r_off = v_off + GLA_HEADS * dv
    tri = jnp.asarray(np.tril(np.ones((GLA_CHUNK, GLA_CHUNK), np.float32)), BF16)
    o_gla = _gla(proj, log_a, tri, row(gla_norm_g[0]), batch=batch, seq=seq,
                 q_off=q_off, k_off=k_off, v_off=v_off, r_off=r_off, dk=dk, dv=dv)
    xf = _conv_out_proj_residual(proj, conv_w[0].astype(F32), o_gla, hyb_w_out[0].astype(BF16), xf,
                                 seq=seq)
    xf = _mlp_residual(xf, row(norm_mlp_g[0]), w1_layers, w2_layers, 0, row(final_norm_g),
                       final_norm=False)

    col_scale = jnp.concatenate([jnp.full((1, d), float(dh) ** -0.5 * LOG2_E, F32),
                                 jnp.ones((1, 2 * d), F32)], axis=1)
    qkv = _qkv_proj_residue_major(xf, row(norm_mix_g[1]), attn_w_qkv[0].astype(BF16), col_scale,
                                  batch=batch, seq=seq)
    slopes = jnp.exp2(-8.0 * jnp.arange(1, ATTN_HEADS + 1, dtype=F32) / ATTN_HEADS)
    attn = _dilated_attention(qkv, slopes, batch=batch, seq=seq, dh=dh)
    xf = _attn_out_proj_residual(attn, attn_w_o[0].astype(BF16), xf, batch=batch, seq=seq)
    out = _mlp_residual(xf, row(norm_mlp_g[1]), w1_layers, w2_layers, 1, row(final_norm_g),
                        final_norm=True)
    return out.reshape(batch, seq, d)
```

```python
import functools

import jax
import jax.numpy as jnp
import numpy as np
from jax import lax
from jax.experimental import pallas as pl
from jax.experimental.pallas import tpu as pltpu

F32 = jnp.float32
BF16 = jnp.bfloat16

NORM_EPS = 1e-6
CONV_K = 3
GLA_HEADS = 4
GLA_GATE_RANK = 16
GLA_GATE_TEMP = 16.0
ATTN_HEADS = 16
ATTN_HEADS_PER_STEP = 2
DILATED_GROUPS = ((128, 1), (512, 4), (2048, 16))
DIL_MAX = 16
ATTN_STEPS = 128

LANES = 128
GLA_CHUNK = 256
GLA_SUB = 16
GLA_BLOCK = 64
GLA_MAX_BLOCK_DECAY = 60.0
MASK_VALUE = -1e30
LOG2_E = 1.4426950408889634
TILES_16 = 16
BLOCKS_4 = 8
BLOCKS_1 = 16
V7X_VMEM_BYTES = 64 * 1024 * 1024
VMEM_LIMIT_BYTES = 56 * 1024 * 1024


def _params(*semantics, vmem_limit_bytes=VMEM_LIMIT_BYTES):
    return pltpu.CompilerParams(dimension_semantics=semantics, vmem_limit_bytes=vmem_limit_bytes)


def _nbytes(shape, dtype):
    return int(np.prod(shape)) * jnp.dtype(dtype).itemsize


def _dot(a, b):
    return jnp.dot(a, b, preferred_element_type=F32)


def _dot_nt(a, b):
    return lax.dot_general(a, b, (((1,), (1,)), ((), ())), preferred_element_type=F32)


def _dot_tn(a, b):
    return lax.dot_general(a, b, (((0,), (0,)), ((), ())), preferred_element_type=F32)


def _rmsnorm(x, g):
    return x * lax.rsqrt(jnp.mean(x * x, axis=-1, keepdims=True) + NORM_EPS) * g


def _tile_of_trip(i):
    return jnp.maximum(i - 1, 0)


def _chunk_index(i, j, n_tiles, chunks):
    return jnp.minimum(i, n_tiles - 1) * chunks + j


def _in_proj_gla_kernel(xc_ref, g_ref, w_ref, wlow_ref, wgate_ref, bgate_ref, o_ref, loga_ref,
                        xn_ref):
    i, j = pl.program_id(0), pl.program_id(1)
    ch = xc_ref.shape[0]
    fill = i % 2
    rows = pl.ds(pl.multiple_of(j * ch, ch), ch)

    def norm_and_gate_chunk():
        xn = _rmsnorm(xc_ref[...], g_ref[...]).astype(BF16)
        xn_ref[fill, rows, :] = xn
        g_low = _dot(xn, wlow_ref[...]).astype(BF16)
        gate = _dot(g_low, wgate_ref[...]) + bgate_ref[...]
        log_sig = jnp.minimum(gate, 0.0) - jnp.log(1.0 + jnp.exp(-jnp.abs(gate)))
        loga_ref[rows, :] = log_sig * (1.0 / GLA_GATE_TEMP)

    @pl.when(i == 0)
    def _():
        norm_and_gate_chunk()

    @pl.when(i > 0)
    def _():
        norm_and_gate_chunk()
        o_ref[...] = _dot(xn_ref[1 - fill], w_ref[...]).astype(o_ref.dtype)


def _in_proj_gla(x, gain, w, w_low, w_gate, b_gate, *, cols, tm=1024, tn=1536):
    n, d = x.shape
    n_gate = w_gate.shape[1]
    n_tiles, steps = n // tm, cols // tn
    ch = tm // steps
    return pl.pallas_call(
        _in_proj_gla_kernel,
        grid=(n_tiles + 1, steps),
        in_specs=[
            pl.BlockSpec((ch, d), lambda i, j: (_chunk_index(i, j, n_tiles, steps), 0)),
            pl.BlockSpec((1, d), lambda i, j: (0, 0)),
            pl.BlockSpec((d, tn), lambda i, j: (0, j)),
            pl.BlockSpec((d, LANES), lambda i, j: (0, 0)),
            pl.BlockSpec((LANES, n_gate), lambda i, j: (0, 0)),
            pl.BlockSpec((1, n_gate), lambda i, j: (0, 0)),
        ],
        out_specs=[
            pl.BlockSpec((tm, tn), lambda i, j: (_tile_of_trip(i), j * jnp.minimum(i, 1))),
            pl.BlockSpec((tm, n_gate), lambda i, j: (jnp.minimum(i, n_tiles - 1), 0)),
        ],
        out_shape=[
            jax.ShapeDtypeStruct((n, cols), BF16),
            jax.ShapeDtypeStruct((n, n_gate), F32),
        ],
        scratch_shapes=[pltpu.VMEM((2, tm, d), BF16)],
        compiler_params=_params("arbitrary", "arbitrary"),
        name="in_proj_gla",
    )(x, gain, w, w_low, w_gate, b_gate)


def _qkv_proj_kernel(xc_ref, g_ref, w_ref, scale_ref, o_ref, xn_ref):
    i, j = pl.program_id(0), pl.program_id(1)
    ch, d = xc_ref.shape
    per = ch // DIL_MAX
    fill = i % 2

    def norm_chunk():
        xn = _rmsnorm(xc_ref[...], g_ref[...]).astype(BF16)
        xn_ref[fill, :, pl.ds(pl.multiple_of(j * per, per), per), :] = (
            jnp.swapaxes(xn.reshape(per, DIL_MAX, d), 0, 1))

    @pl.when(i == 0)
    def _():
        norm_chunk()

    @pl.when(i > 0)
    def _():
        norm_chunk()
        xn = xn_ref[1 - fill]
        acc = _dot(xn.reshape(xn.shape[0] * xn.shape[1], d), w_ref[...]) * scale_ref[...]
        o_ref[...] = acc.astype(o_ref.dtype).reshape(o_ref.shape)


def _qkv_proj_residue_major(x, gain, w, col_scale, *, batch, seq, tm=1024, tn=1536):
    n, d = x.shape
    cols = w.shape[1]
    sub = seq // DIL_MAX
    per = tm // DIL_MAX
    tiles_per_seq = seq // tm
    n_tiles, steps = n // tm, cols // tn
    ch = tm // steps
    assert ch % (DIL_MAX * 16) == 0

    def out_index(i, j):
        t = _tile_of_trip(i)
        return (t // tiles_per_seq, 0, t % tiles_per_seq, j * jnp.minimum(i, 1))

    out = pl.pallas_call(
        _qkv_proj_kernel,
        grid=(n_tiles + 1, steps),
        in_specs=[
            pl.BlockSpec((ch, d), lambda i, j: (_chunk_index(i, j, n_tiles, steps), 0)),
            pl.BlockSpec((1, d), lambda i, j: (0, 0)),
            pl.BlockSpec((d, tn), lambda i, j: (0, j)),
            pl.BlockSpec((1, tn), lambda i, j: (0, j)),
        ],
        out_specs=pl.BlockSpec((None, DIL_MAX, per, tn), out_index),
        out_shape=jax.ShapeDtypeStruct((batch, DIL_MAX, sub, cols), BF16),
        scratch_shapes=[pltpu.VMEM((2, DIL_MAX, per, d), BF16)],
        compiler_params=_params("arbitrary", "arbitrary"),
        name="qkv_proj",
    )(x, gain, w, col_scale)
    return out.reshape(n, cols)


def _conv_out_proj_kernel(ab_ref, ac_ref, ax_ref, cw_ref, gla_ref, wa_ref, wb_ref, x_ref, o_ref,
                          ubuf_ref, *, tiles_per_seq):
    tb = ab_ref.shape[0]
    pad = 8

    @pl.when(pl.program_id(0) % tiles_per_seq == 0)
    def _():
        ubuf_ref[0:pad, :] = jnp.zeros((pad, ubuf_ref.shape[1]), F32)

    acc = x_ref[...] + _dot(gla_ref[...], wb_ref[...])
    u = ac_ref[...].astype(F32) * ax_ref[...].astype(F32)
    ubuf_ref[pad:pad + tb, :] = u
    u1 = ubuf_ref[pad - 1:pad - 1 + tb, :]
    u2 = ubuf_ref[pad - 2:pad - 2 + tb, :]
    cw = cw_ref[...]
    y = ab_ref[...].astype(F32) * (cw[0:1, :] * u2 + cw[1:2, :] * u1 + cw[2:3, :] * u)
    o_ref[...] = acc + _dot(y.astype(BF16), wa_ref[...])
    ubuf_ref[0:pad, :] = ubuf_ref[tb:tb + pad, :]


def _conv_out_proj_residual(proj, conv_w, o_gla, w_out, x, *, seq, tm=512):
    n, d = x.shape
    ch = conv_w.shape[1]
    assert w_out.shape[0] == 2 * ch and o_gla.shape[1] == ch
    kern = functools.partial(_conv_out_proj_kernel, tiles_per_seq=seq // tm)
    return pl.pallas_call(
        kern,
        grid=(n // tm,),
        in_specs=[
            pl.BlockSpec((tm, ch), lambda i: (i, 0)),
            pl.BlockSpec((tm, ch), lambda i: (i, 1)),
            pl.BlockSpec((tm, ch), lambda i: (i, 2)),
            pl.BlockSpec((CONV_K, ch), lambda i: (0, 0)),
            pl.BlockSpec((tm, o_gla.shape[1]), lambda i: (i, 0)),
            pl.BlockSpec((ch, d), lambda i: (0, 0)),
            pl.BlockSpec((ch, d), lambda i: (1, 0)),
            pl.BlockSpec((tm, d), lambda i: (i, 0)),
        ],
        out_specs=pl.BlockSpec((tm, d), lambda i: (i, 0)),
        out_shape=jax.ShapeDtypeStruct((n, d), F32),
        scratch_shapes=[pltpu.VMEM((tm + 8, ch), F32)],
        compiler_params=_params("arbitrary"),
        name="conv_out_proj",
    )(proj, proj, proj, conv_w, o_gla, w_out, w_out, x)


def _gla_scores_factored(q, k, g):
    c_len = q.shape[0]
    score_rows = []
    for a in range(c_len // GLA_BLOCK):
        lo, hi = a * GLA_BLOCK, (a + 1) * GLA_BLOCK
        n_keys = LANES * -(-hi // LANES)
        g_first = g[lo:lo + 1, :]
        q_a = (q[lo:hi] * jnp.exp(g[lo:hi] - g_first)).astype(BF16)
        k_a = (k[:n_keys] * jnp.exp(jnp.minimum(g_first - g[:n_keys], GLA_MAX_BLOCK_DECAY)))
        s_a = _dot_nt(q_a, k_a.astype(BF16))
        row = lax.broadcasted_iota(jnp.int32, s_a.shape, 0) + lo
        col = lax.broadcasted_iota(jnp.int32, s_a.shape, 1)
        s_a = jnp.where(col <= row, s_a, 0.0)
        if n_keys < c_len:
            s_a = jnp.concatenate([s_a, jnp.zeros((GLA_BLOCK, c_len - n_keys), F32)], axis=1)
        score_rows.append(s_a)
    return jnp.concatenate(score_rows, axis=0)


def _gla_scores_pairwise(q, k, g):
    c_len = q.shape[0]
    n_sub = c_len // GLA_SUB
    score_rows = [jnp.zeros((GLA_SUB, c_len), F32)]
    for a in range(1, n_sub):
        lo = a * GLA_SUB
        n_keys = LANES * -(-lo // LANES)
        g_ref_row = g[lo:lo + 1, :]
        q_a = (q[lo:lo + GLA_SUB] * jnp.exp(g[lo:lo + GLA_SUB] - g_ref_row)).astype(BF16)
        k_a = (k[:n_keys] * jnp.exp(jnp.minimum(g_ref_row - g[:n_keys], 0.0))).astype(BF16)
        s_a = _dot_nt(q_a, k_a)
        col = lax.broadcasted_iota(jnp.int32, s_a.shape, 1)
        s_a = jnp.where(col < lo, s_a, 0.0)
        if n_keys < c_len:
            s_a = jnp.concatenate([s_a, jnp.zeros((GLA_SUB, c_len - n_keys), F32)], axis=1)
        score_rows.append(s_a)
    scores = jnp.concatenate(score_rows, axis=0)

    row = lax.broadcasted_iota(jnp.int32, (c_len, LANES), 0)
    lane = lax.broadcasted_iota(jnp.int32, (c_len, LANES), 1)
    delta = row % LANES - lane
    band_id = jnp.where((delta >= 0) & (delta <= row % GLA_SUB), delta, -1)
    band = jnp.zeros((c_len, LANES), F32)
    for d in range(GLA_SUB):
        k_d = k if d == 0 else pltpu.roll(k, d, 0)
        g_d = g if d == 0 else pltpu.roll(g, d, 0)
        e = jnp.exp(jnp.minimum(g - g_d, 0.0))
        diag = jnp.sum(q * k_d * e, axis=-1, keepdims=True)
        band = jnp.where(band_id == d, diag, band)
    zeros = jnp.zeros((LANES, LANES), F32)
    band_rows = []
    for t in range(c_len // LANES):
        tiles = [zeros] * (c_len // LANES)
        tiles[t] = band[t * LANES:(t + 1) * LANES]
        band_rows.append(jnp.concatenate(tiles, axis=1))
    return scores + jnp.concatenate(band_rows, axis=0)


def _gla_kernel(q_ref, k_ref, v_ref, r_ref, la_ref, tri_ref, gn_ref, o_ref, st_ref, g_ref,
                inter_ref, *, q_scale, dk, dv):
    c_len = q_ref.shape[0]

    @pl.when(pl.program_id(1) == 0)
    def _():
        st_ref[...] = jnp.zeros_like(st_ref)

    la = la_ref[...]
    tri = tri_ref[...]
    la_hi = la.astype(BF16)
    rem = la - la_hi.astype(F32)
    la_mid = rem.astype(BF16)
    la_lo = (rem - la_mid.astype(F32)).astype(BF16)
    g_all = _dot(tri, la_hi) + _dot(tri, la_mid) + _dot(tri, la_lo)
    g_ref[...] = g_all
    block_decay = [g_all[lo:lo + 1, :] - g_all[lo + GLA_BLOCK - 1:lo + GLA_BLOCK, :]
                   for lo in range(0, c_len, GLA_BLOCK)]
    worst_decay = jnp.max(jnp.concatenate(block_decay, axis=0))

    def head_operands(h):
        g = g_ref[:, h * dk:(h + 1) * dk]
        q = q_ref[:, h * dk:(h + 1) * dk].astype(F32) * q_scale
        k = k_ref[:, h * dk:(h + 1) * dk].astype(F32)
        return g, q, k, v_ref[:, h * dv:(h + 1) * dv]

    def finish_head(h, o):
        o = _rmsnorm(o, gn_ref[...])
        r = r_ref[:, h * dv:(h + 1) * dv].astype(F32)
        o_ref[:, h * dv:(h + 1) * dv] = (o * (r / (1.0 + jnp.exp(-r)))).astype(o_ref.dtype)

    for h in range(GLA_HEADS):
        g, q, k, v = head_operands(h)
        g_last = g[c_len - 1:c_len, :]
        st = st_ref[h]
        o_inter = _dot_nt((q * jnp.exp(g)).astype(BF16), st.astype(BF16))
        inter_ref[h] = o_inter
        k_dec = (k * jnp.exp(g_last - g)).astype(BF16)
        st_ref[h] = st * jnp.exp(g_last) + _dot_tn(v, k_dec)
        finish_head(h, o_inter + _dot(_gla_scores_factored(q, k, g).astype(BF16), v))

    @pl.when(worst_decay > GLA_MAX_BLOCK_DECAY)
    def _():
        for h in range(GLA_HEADS):
            g, q, k, v = head_operands(h)
            finish_head(h, inter_ref[h] + _dot(_gla_scores_pairwise(q, k, g).astype(BF16), v))


def _gla(proj, log_a, tri, norm_g, *, batch, seq, q_off, k_off, v_off, r_off, dk, dv):
    n = proj.shape[0]
    cpb = seq // GLA_CHUNK
    qk_w, vr_w = GLA_HEADS * dk, GLA_HEADS * dv
    row = lambda b, c: b * cpb + c
    kern = functools.partial(_gla_kernel, q_scale=float(dk) ** -0.5, dk=dk, dv=dv)
    return pl.pallas_call(
        kern,
        grid=(batch, cpb),
        in_specs=[
            pl.BlockSpec((GLA_CHUNK, qk_w), lambda b, c: (row(b, c), q_off // qk_w)),
            pl.BlockSpec((GLA_CHUNK, qk_w), lambda b, c: (row(b, c), k_off // qk_w)),
            pl.BlockSpec((GLA_CHUNK, vr_w), lambda b, c: (row(b, c), v_off // vr_w)),
            pl.BlockSpec((GLA_CHUNK, vr_w), lambda b, c: (row(b, c), r_off // vr_w)),
            pl.BlockSpec((GLA_CHUNK, qk_w), lambda b, c: (row(b, c), 0)),
            pl.BlockSpec((GLA_CHUNK, GLA_CHUNK), lambda b, c: (0, 0)),
            pl.BlockSpec((1, dv), lambda b, c: (0, 0)),
        ],
        out_specs=pl.BlockSpec((GLA_CHUNK, vr_w), lambda b, c: (row(b, c), 0)),
        out_shape=jax.ShapeDtypeStruct((n, vr_w), BF16),
        scratch_shapes=[pltpu.VMEM((GLA_HEADS, dv, dk), F32), pltpu.VMEM((GLA_CHUNK, qk_w), F32),
                        pltpu.VMEM((GLA_HEADS, GLA_CHUNK, dv), F32)],
        compiler_params=_params("parallel", "arbitrary"),
        name="gla",
    )(proj, proj, proj, proj, log_a, tri, norm_g)


def _attn_out_proj_kernel(a_ref, w_ref, x_ref, o_ref):
    tm, d = x_ref.shape
    acc = _dot(a_ref[...].reshape(tm, a_ref.shape[-1]), w_ref[...])
    acc = jnp.swapaxes(acc.reshape(DIL_MAX, tm // DIL_MAX, d), 0, 1).reshape(tm, d)
    o_ref[...] = x_ref[...] + acc


def _attn_out_proj_residual(attn, w, x, *, batch, seq, tm=512):
    n, d = x.shape
    per = tm // DIL_MAX
    tps = seq // tm
    attn = attn.reshape(batch, DIL_MAX, seq // DIL_MAX, attn.shape[1])
    return pl.pallas_call(
        _attn_out_proj_kernel,
        grid=(n // tm,),
        in_specs=[
            pl.BlockSpec((None, DIL_MAX, per, attn.shape[-1]), lambda i: (i // tps, 0, i % tps, 0)),
            pl.BlockSpec(w.shape, lambda i: (0, 0)),
            pl.BlockSpec((tm, d), lambda i: (i, 0)),
        ],
        out_specs=pl.BlockSpec((tm, d), lambda i: (i, 0)),
        out_shape=jax.ShapeDtypeStruct((n, d), F32),
        compiler_params=_params("parallel"),
        name="attn_out_proj",
    )(attn, w, x)


def _mlp_kernel(x_ref, g_ref, w1_ref, w2_ref, gf_ref, o_ref, xn_ref, *, final_norm):
    f = pl.program_id(1)

    @pl.when(f == 0)
    def _():
        x = x_ref[...]
        xn_ref[...] = _rmsnorm(x, g_ref[...]).astype(BF16)
        o_ref[...] = x

    h = _dot(xn_ref[...], w1_ref[...])
    h = jnp.square(jnp.maximum(h, 0.0)).astype(BF16)
    o_ref[...] += _dot(h, w2_ref[...])

    if final_norm:
        @pl.when(f == pl.num_programs(1) - 1)
        def _():
            o_ref[...] = _rmsnorm(o_ref[...], gf_ref[...])


def _mlp_residual(x, gain, w1_layers, w2_layers, layer, final_gain, *, final_norm, tm=512,
                  tf=2048):
    n, d = x.shape
    d_ff = w1_layers.shape[2]
    kern = functools.partial(_mlp_kernel, final_norm=final_norm)
    vmem = (2 * (2 * _nbytes((tm, d), F32) + 2 * _nbytes((d, tf), BF16)) + _nbytes((tm, d), BF16)
            + _nbytes((tm, tf), F32) + _nbytes((tm, tf), BF16))
    vmem_limit = min(vmem + (4 << 20), V7X_VMEM_BYTES)
    return pl.pallas_call(
        kern,
        grid=(n // tm, d_ff // tf),
        in_specs=[
            pl.BlockSpec((tm, d), lambda i, f: (i, 0)),
            pl.BlockSpec((1, d), lambda i, f: (0, 0)),
            pl.BlockSpec((None, d, tf), lambda i, f: (layer, 0, f)),
            pl.BlockSpec((None, tf, d), lambda i, f: (layer, f, 0)),
            pl.BlockSpec((1, d), lambda i, f: (0, 0)),
        ],
        out_specs=pl.BlockSpec((tm, d), lambda i, f: (i, 0)),
        out_shape=jax.ShapeDtypeStruct((n, d), F32),
        scratch_shapes=[pltpu.VMEM((tm, d), BF16)],
        compiler_params=_params("parallel", "arbitrary", vmem_limit_bytes=vmem_limit),
        name="mlp",
    )(x, gain, w1_layers, w2_layers, final_gain)


def _attn_bias_tables(slopes, sub):
    def table(q_pos, k_pos, dilation):
        diff = q_pos[:, None] - k_pos[None, :]
        valid = jnp.asarray((diff >= 0) & (diff <= ATTN_STEPS))
        dist = jnp.asarray((dilation * diff).astype(np.float32))
        bias = -(slopes * LOG2_E)[:, None, None] * dist[None]
        return jnp.where(valid[None], bias, MASK_VALUE).astype(F32)

    q = np.arange(128)
    b16_first = table(q, q, 16)
    b16 = table(q + 128, np.arange(256), 16)

    q4 = 4 * (q % 32) + q // 32
    k = np.arange(256)
    k4 = 4 * (k % 64) + k // 64
    b4 = jnp.stack([table(q4, k4, 4), table(q4 + 128, k4, 4)], axis=1)

    q = np.arange(256)
    q1 = 16 * (q % 16) + q // 16
    k = np.arange(384)
    k1 = 16 * (k % 24) + k // 24
    b1 = jnp.stack([table(q1, k1, 1), table(q1 + 128, k1, 1)], axis=1)
    return b1, b4, b16_first, b16


def _attn_kernel(q_all, k_all, v_all, b1_all, b4_all, b16f_all, b16_all, o_all, *scratch, sub,
                 heads):
    def gather(ref, starts, size):
        return jnp.concatenate([ref[pl.ds(s, size), :] for s in starts], axis=0)

    def scatter(ref, starts, size, val):
        for c, s in enumerate(starts):
            ref[pl.ds(s, size), :] = val[c * size:(c + 1) * size]

    def lanes(x):
        return jnp.broadcast_to(x, (x.shape[0], LANES))

    def across_keys(x, n_keys):
        return jnp.concatenate([x] * (n_keys // LANES), axis=1)

    def weighted_values_and_sum(p, v):
        pv = _dot(p.astype(BF16), jnp.concatenate([v, jnp.ones_like(v)], axis=1))
        return pv[:, :LANES], pv[:, LANES:]

    def first_tile(q, k, v, bias):
        s = _dot_nt(q, k) + bias
        m = jnp.max(s, axis=-1, keepdims=True)
        acc, l = weighted_values_and_sum(jnp.exp2(s - m), v)
        return lanes(m), l, acc

    def next_tile(q, k, v, bias, m_old, l_old, acc_old):
        s = _dot_nt(q, k) + bias
        m_new = jnp.maximum(m_old, lanes(jnp.max(s, axis=-1, keepdims=True)))
        alpha = jnp.exp2(m_old - m_new)
        acc, l = weighted_values_and_sum(jnp.exp2(s - across_keys(m_new, s.shape[1])), v)
        return m_new, alpha * l_old + l, alpha * acc_old + acc

    dh = q_all.shape[1] // heads

    def loop(trips, body):
        if trips == 1:
            body(0, 0)
        else:
            lax.fori_loop(0, trips, body, 0)

    blocks_4 = min(BLOCKS_4, sub // 32)
    blocks_1 = min(BLOCKS_1, sub // 16)
    for group in (16, 4, 1):
        for h in range(heads):
            cols = slice(h * dh, (h + 1) * dh)
            head_refs = (q_all.at[:, cols], k_all.at[:, cols], v_all.at[:, cols], o_all.at[:, cols],
                         b1_all.at[h], b4_all.at[h], b16f_all.at[h], b16_all.at[h],
                         *scratch[3 * h:3 * h + 3])
            _attn_group(group, head_refs, sub, blocks_4, blocks_1, loop, first_tile, next_tile,
                        gather, scatter)


def _attn_group(group, head_refs, sub, blocks_4, blocks_1, loop, first_tile, next_tile, gather,
                scatter):
    (q_ref, k_ref, v_ref, o_ref, b1_ref, b4_ref, b16f_ref, b16_ref, m_ref, l_ref, acc_ref) = head_refs

    def group16(i, carry):
        bias_first, bias = b16f_ref[...], b16_ref[...]
        rows, res = [], []
        for u in range(TILES_16):
            base = (TILES_16 * i + u) * sub
            for j in range(sub // ATTN_STEPS):
                q_rows = pl.ds(pl.multiple_of(base + j * ATTN_STEPS, ATTN_STEPS), ATTN_STEPS)
                if j == 0:
                    k_rows, b = q_rows, bias_first
                else:
                    k_rows = pl.ds(pl.multiple_of(base + (j - 1) * ATTN_STEPS, ATTN_STEPS),
                                   2 * ATTN_STEPS)
                    b = bias
                rows.append(q_rows)
                res.append(first_tile(q_ref[q_rows, :], k_ref[k_rows, :], v_ref[k_rows, :], b))
        for r, (m, l, acc) in zip(rows, res):
            m_ref[r, :] = m
            l_ref[r, :] = l
            acc_ref[r, :] = acc
        return carry

    def group4(i, carry):
        starts, res = [], []
        for u in range(blocks_4):
            blk = blocks_4 * i + u
            bias = b4_ref[jnp.where(blk == 0, 0, 1)]
            k_blk = jnp.maximum(blk - 1, 0)
            for r4 in range(4):
                qs = [pl.multiple_of((r4 + 4 * c) * sub + 32 * blk, 32) for c in range(4)]
                ks = [pl.multiple_of((r4 + 4 * c) * sub + 32 * k_blk, 32) for c in range(4)]
                starts.append(qs)
                res.append(next_tile(
                    gather(q_ref, qs, 32), gather(k_ref, ks, 64), gather(v_ref, ks, 64), bias,
                    gather(m_ref, qs, 32), gather(l_ref, qs, 32), gather(acc_ref, qs, 32)))
        for qs, (m, l, acc) in zip(starts, res):
            scatter(m_ref, qs, 32, m)
            scatter(l_ref, qs, 32, l)
            scatter(acc_ref, qs, 32, acc)
        return carry

    def group1(i, carry):
        res = []
        for u in range(blocks_1):
            blk = blocks_1 * i + u
            bias = b1_ref[jnp.where(blk == 0, 0, 1)]
            k_lo = jnp.maximum(16 * blk - 8, 0)
            qs = [pl.multiple_of(r * sub + 16 * blk, 16) for r in range(DIL_MAX)]
            ks = [pl.multiple_of(r * sub + k_lo, 8) for r in range(DIL_MAX)]
            _, l, acc = next_tile(
                gather(q_ref, qs, 16), gather(k_ref, ks, 24), gather(v_ref, ks, 24), bias,
                gather(m_ref, qs, 16), gather(l_ref, qs, 16), gather(acc_ref, qs, 16))
            res.append((qs, (acc / l).astype(o_ref.dtype)))
        for qs, out in res:
            scatter(o_ref, qs, 16, out)
        return carry

    if group == 16:
        loop(DIL_MAX // TILES_16, group16)
    elif group == 4:
        loop(sub // (32 * blocks_4), group4)
    else:
        loop(sub // (16 * blocks_1), group1)


def _dilated_attention(qkv, slopes, *, batch, seq, dh):
    n = qkv.shape[0]
    sub = seq // DIL_MAX
    b1, b4, b16_first, b16 = _attn_bias_tables(slopes, sub)
    hp = ATTN_HEADS_PER_STEP
    groups = ATTN_HEADS // hp
    kern = functools.partial(_attn_kernel, sub=sub, heads=hp)
    return pl.pallas_call(
        kern,
        grid=(batch, groups),
        in_specs=[
            pl.BlockSpec((seq, hp * dh), lambda b, h: (b, h)),
            pl.BlockSpec((seq, hp * dh), lambda b, h: (b, groups + h)),
            pl.BlockSpec((seq, hp * dh), lambda b, h: (b, 2 * groups + h)),
            pl.BlockSpec((hp,) + b1.shape[1:], lambda b, h: (h, 0, 0, 0)),
            pl.BlockSpec((hp,) + b4.shape[1:], lambda b, h: (h, 0, 0, 0)),
            pl.BlockSpec((hp,) + b16_first.shape[1:], lambda b, h: (h, 0, 0)),
            pl.BlockSpec((hp,) + b16.shape[1:], lambda b, h: (h, 0, 0)),
        ],
        out_specs=pl.BlockSpec((seq, hp * dh), lambda b, h: (b, h)),
        out_shape=jax.ShapeDtypeStruct((n, ATTN_HEADS * dh), BF16),
        scratch_shapes=[pltpu.VMEM((seq, LANES), F32), pltpu.VMEM((seq, LANES), F32),
                        pltpu.VMEM((seq, dh), F32)] * hp,
        compiler_params=_params("parallel", "parallel"),
        name="dilated_attn",
    )(qkv, qkv, qkv, b1, b4, b16_first, b16)


def kernel(x, norm_mix_g, norm_mlp_g, final_norm_g, hyb_w_in, conv_w, gla_w_gate2, gla_b_gate,
           gla_norm_g, hyb_w_out, attn_w_qkv, attn_w_o, mlp_w1, mlp_w2):
    batch, seq, d = x.shape
    n = batch * seq
    depth = norm_mix_g.shape[0]
    conv_ch = conv_w.shape[-1]
    dv = gla_norm_g.shape[-1]
    dk = gla_w_gate2.shape[-1] // GLA_HEADS
    dh = d // ATTN_HEADS
    assert depth == 2 and seq % (DIL_MAX * ATTN_STEPS) == 0 and seq % GLA_CHUNK == 0
    assert dk == LANES and dh == LANES and dv % LANES == 0

    row = lambda v: v.reshape(1, -1).astype(F32)
    xf = x.reshape(n, d)

    main_cols = 3 * conv_ch + 2 * GLA_HEADS * dk + 2 * GLA_HEADS * dv
    w_in = hyb_w_in[0]
    w_low = jnp.pad(w_in[:, main_cols:], ((0, 0), (0, LANES - GLA_GATE_RANK))).astype(BF16)
    w_gate = jnp.pad(gla_w_gate2[0], ((0, LANES - GLA_GATE_RANK), (0, 0))).astype(BF16)
    proj, log_a = _in_proj_gla(xf, row(norm_mix_g[0]), w_in.astype(BF16), w_low, w_gate,
                               row(gla_b_gate[0]), cols=main_cols)
    w1_layers, w2_layers = mlp_w1.astype(BF16), mlp_w2.astype(BF16)

    q_off = 3 * conv_ch
    k_off = q_off + GLA_HEADS * dk
    v_off = k_off + GLA_HEADS * dk
    r_off = v_off + GLA_HEADS * dv
    tri = jnp.asarray(np.tril(np.ones((GLA_CHUNK, GLA_CHUNK), np.float32)), BF16)
    o_gla = _gla(proj, log_a, tri, row(gla_norm_g[0]), batch=batch, seq=seq,
                 q_off=q_off, k_off=k_off, v_off=v_off, r_off=r_off, dk=dk, dv=dv)
    xf = _conv_out_proj_residual(proj, conv_w[0].astype(F32), o_gla, hyb_w_out[0].astype(BF16), xf,
                                 seq=seq)
    xf = _mlp_residual(xf, row(norm_mlp_g[0]), w1_layers, w2_layers, 0, row(final_norm_g),
                       final_norm=False)

    col_scale = jnp.concatenate([jnp.full((1, d), float(dh) ** -0.5 * LOG2_E, F32),
                                 jnp.ones((1, 2 * d), F32)], axis=1)
    qkv = _qkv_proj_residue_major(xf, row(norm_mix_g[1]), attn_w_qkv[0].astype(BF16), col_scale,
                                  batch=batch, seq=seq)
    slopes = jnp.exp2(-8.0 * jnp.arange(1, ATTN_HEADS + 1, dtype=F32) / ATTN_HEADS)
    attn = _dilated_attention(qkv, slopes, batch=batch, seq=seq, dh=dh)
    xf = _attn_out_proj_residual(attn, attn_w_o[0].astype(BF16), xf, batch=batch, seq=seq)
    out = _mlp_residual(xf, row(norm_mlp_g[1]), w1_layers, w2_layers, 1, row(final_norm_g),
                        final_norm=True)
    return out.reshape(batch, seq, d)
```

```python
import functools

import jax
import jax.numpy as jnp
import numpy as np
from jax import lax
from jax.experimental import pallas as pl
from jax.experimental.pallas import tpu as pltpu

F32 = jnp.float32
BF16 = jnp.bfloat16

NORM_EPS = 1e-6
CONV_K = 3
GLA_HEADS = 4
GLA_GATE_RANK = 16
GLA_GATE_TEMP = 16.0
ATTN_HEADS = 16
ATTN_HEADS_PER_STEP = 2
DILATED_GROUPS = ((128, 1), (512, 4), (2048, 16))
DIL_MAX = 16
ATTN_STEPS = 128

LANES = 128
GLA_CHUNK = 256
GLA_SUB = 16
GLA_BLOCK = 64
GLA_MAX_BLOCK_DECAY = 60.0
MASK_VALUE = -1e30
LOG2_E = 1.4426950408889634
TILES_16 = 16
BLOCKS_4 = 8
BLOCKS_1 = 16
V7X_VMEM_BYTES = 64 * 1024 * 1024
VMEM_LIMIT_BYTES = 56 * 1024 * 1024


def _params(*semantics, vmem_limit_bytes=VMEM_LIMIT_BYTES):
    return pltpu.CompilerParams(dimension_semantics=semantics, vmem_limit_bytes=vmem_limit_bytes)


def _nbytes(shape, dtype):
    return int(np.prod(shape)) * jnp.dtype(dtype).itemsize


def _dot(a, b):
    return jnp.dot(a, b, preferred_element_type=F32)


def _dot_nt(a, b):
    return lax.dot_general(a, b, (((1,), (1,)), ((), ())), preferred_element_type=F32)


def _dot_tn(a, b):
    return lax.dot_general(a, b, (((0,), (0,)), ((), ())), preferred_element_type=F32)


def _rmsnorm(x, g):
    return x * lax.rsqrt(jnp.mean(x * x, axis=-1, keepdims=True) + NORM_EPS) * g


def _tile_of_trip(i):
    return jnp.maximum(i - 1, 0)


def _chunk_index(i, j, n_tiles, chunks):
    return jnp.minimum(i, n_tiles - 1) * chunks + j


def _in_proj_gla_kernel(xc_ref, g_ref, w_ref, wlow_ref, wgate_ref, bgate_ref, *rest, n_cast):
    cast_in, (o_ref, loga_ref) = rest[:n_cast], rest[n_cast:n_cast + 2]
    cast_out, xn_ref = rest[n_cast + 2:2 * n_cast + 2], rest[2 * n_cast + 2]
    i, j = pl.program_id(0), pl.program_id(1)
    ch = xc_ref.shape[0]
    fill = i % 2
    rows = pl.ds(pl.multiple_of(j * ch, ch), ch)

    def norm_and_gate_chunk():
        xn = _rmsnorm(xc_ref[...], g_ref[...]).astype(BF16)
        xn_ref[fill, rows, :] = xn
        g_low = _dot(xn, wlow_ref[...]).astype(BF16)
        gate = _dot(g_low, wgate_ref[...]) + bgate_ref[...]
        log_sig = jnp.minimum(gate, 0.0) - jnp.log(1.0 + jnp.exp(-jnp.abs(gate)))
        loga_ref[rows, :] = log_sig * (1.0 / GLA_GATE_TEMP)
        for src, dst in zip(cast_in, cast_out):
            dst[...] = src[...].astype(dst.dtype)

    @pl.when(i == 0)
    def _():
        norm_and_gate_chunk()

    @pl.when(i > 0)
    def _():
        norm_and_gate_chunk()
        o_ref[...] = _dot(xn_ref[1 - fill], w_ref[...]).astype(o_ref.dtype)


def _in_proj_gla(x, gain, w, w_low, w_gate, b_gate, *, cols, cast_layers=(), tm=1024, tn=1536):
    n, d = x.shape
    n_gate = w_gate.shape[1]
    n_tiles, steps = n // tm, cols // tn
    ch = tm // steps
    cast_in_specs, cast_out_specs, cast_shapes = [], [], []
    for arr, layer in cast_layers:
        rows, width = arr.shape[1:]
        slab = rows // (n_tiles * steps)
        assert rows % (n_tiles * steps) == 0 and slab % 16 == 0
        last = n_tiles * steps - 1
        cast_in_specs.append(pl.BlockSpec(
            (None, slab, width),
            lambda i, j, layer=layer, last=last: (layer, jnp.minimum(i * steps + j, last), 0)))
        cast_out_specs.append(pl.BlockSpec(
            (slab, width), lambda i, j, last=last: (jnp.minimum(i * steps + j, last), 0)))
        cast_shapes.append(jax.ShapeDtypeStruct((rows, width), BF16))
    kern = functools.partial(_in_proj_gla_kernel, n_cast=len(cast_layers))
    out = pl.pallas_call(
        kern,
        grid=(n_tiles + 1, steps),
        in_specs=[
            pl.BlockSpec((ch, d), lambda i, j: (_chunk_index(i, j, n_tiles, steps), 0)),
            pl.BlockSpec((1, d), lambda i, j: (0, 0)),
            pl.BlockSpec((d, tn), lambda i, j: (0, j)),
            pl.BlockSpec((d, LANES), lambda i, j: (0, 0)),
            pl.BlockSpec((LANES, n_gate), lambda i, j: (0, 0)),
            pl.BlockSpec((1, n_gate), lambda i, j: (0, 0)),
        ] + cast_in_specs,
        out_specs=[
            pl.BlockSpec((tm, tn), lambda i, j: (_tile_of_trip(i), j * jnp.minimum(i, 1))),
            pl.BlockSpec((tm, n_gate), lambda i, j: (jnp.minimum(i, n_tiles - 1), 0)),
        ] + cast_out_specs,
        out_shape=[
            jax.ShapeDtypeStruct((n, cols), BF16),
            jax.ShapeDtypeStruct((n, n_gate), F32),
        ] + cast_shapes,
        scratch_shapes=[pltpu.VMEM((2, tm, d), BF16)],
        compiler_params=_params("arbitrary", "arbitrary"),
        name="in_proj_gla",
    )(x, gain, w, w_low, w_gate, b_gate, *[arr for arr, _ in cast_layers])
    return out[0], out[1], out[2:]


def _qkv_proj_kernel(xc_ref, g_ref, w_ref, scale_ref, o_ref, xn_ref):
    i, j = pl.program_id(0), pl.program_id(1)
    ch, d = xc_ref.shape
    per = ch // DIL_MAX
    fill = i % 2

    def norm_chunk():
        xn = _rmsnorm(xc_ref[...], g_ref[...]).astype(BF16)
        xn_ref[fill, :, pl.ds(pl.multiple_of(j * per, per), per), :] = (
            jnp.swapaxes(xn.reshape(per, DIL_MAX, d), 0, 1))

    @pl.when(i == 0)
    def _():
        norm_chunk()

    @pl.when(i > 0)
    def _():
        norm_chunk()
        xn = xn_ref[1 - fill]
        acc = _dot(xn.reshape(xn.shape[0] * xn.shape[1], d), w_ref[...]) * scale_ref[...]
        o_ref[...] = acc.astype(o_ref.dtype).reshape(o_ref.shape)


def _qkv_proj_residue_major(x, gain, w, col_scale, *, batch, seq, tm=1024, tn=1536):
    n, d = x.shape
    cols = w.shape[1]
    sub = seq // DIL_MAX
    per = tm // DIL_MAX
    tiles_per_seq = seq // tm
    n_tiles, steps = n // tm, cols // tn
    ch = tm // steps
    assert ch % (DIL_MAX * 16) == 0

    def out_index(i, j):
        t = _tile_of_trip(i)
        return (t // tiles_per_seq, 0, t % tiles_per_seq, j * jnp.minimum(i, 1))

    out = pl.pallas_call(
        _qkv_proj_kernel,
        grid=(n_tiles + 1, steps),
        in_specs=[
            pl.BlockSpec((ch, d), lambda i, j: (_chunk_index(i, j, n_tiles, steps), 0)),
            pl.BlockSpec((1, d), lambda i, j: (0, 0)),
            pl.BlockSpec((d, tn), lambda i, j: (0, j)),
            pl.BlockSpec((1, tn), lambda i, j: (0, j)),
        ],
        out_specs=pl.BlockSpec((None, DIL_MAX, per, tn), out_index),
        out_shape=jax.ShapeDtypeStruct((batch, DIL_MAX, sub, cols), BF16),
        scratch_shapes=[pltpu.VMEM((2, DIL_MAX, per, d), BF16)],
        compiler_params=_params("arbitrary", "arbitrary"),
        name="qkv_proj",
    )(x, gain, w, col_scale)
    return out.reshape(n, cols)


def _conv_out_proj_kernel(ab_ref, ac_ref, ax_ref, cw_ref, gla_ref, wa_ref, wb_ref, x_ref, o_ref,
                          ubuf_ref, *, tiles_per_seq):
    tb = ab_ref.shape[0]
    pad = 8

    @pl.when(pl.program_id(0) % tiles_per_seq == 0)
    def _():
        ubuf_ref[0:pad, :] = jnp.zeros((pad, ubuf_ref.shape[1]), F32)

    acc = x_ref[...] + _dot(gla_ref[...], wb_ref[...])
    u = ac_ref[...].astype(F32) * ax_ref[...].astype(F32)
    ubuf_ref[pad:pad + tb, :] = u
    u1 = ubuf_ref[pad - 1:pad - 1 + tb, :]
    u2 = ubuf_ref[pad - 2:pad - 2 + tb, :]
    cw = cw_ref[...]
    y = ab_ref[...].astype(F32) * (cw[0:1, :] * u2 + cw[1:2, :] * u1 + cw[2:3, :] * u)
    o_ref[...] = acc + _dot(y.astype(BF16), wa_ref[...])
    ubuf_ref[0:pad, :] = ubuf_ref[tb:tb + pad, :]


def _conv_out_proj_residual(proj, conv_w, o_gla, w_out, x, *, seq, tm=512):
    n, d = x.shape
    ch = conv_w.shape[1]
    assert w_out.shape[0] == 2 * ch and o_gla.shape[1] == ch
    kern = functools.partial(_conv_out_proj_kernel, tiles_per_seq=seq // tm)
    return pl.pallas_call(
        kern,
        grid=(n // tm,),
        in_specs=[
            pl.BlockSpec((tm, ch), lambda i: (i, 0)),
            pl.BlockSpec((tm, ch), lambda i: (i, 1)),
            pl.BlockSpec((tm, ch), lambda i: (i, 2)),
            pl.BlockSpec((CONV_K, ch), lambda i: (0, 0)),
            pl.BlockSpec((tm, o_gla.shape[1]), lambda i: (i, 0)),
            pl.BlockSpec((ch, d), lambda i: (0, 0)),
            pl.BlockSpec((ch, d), lambda i: (1, 0)),
            pl.BlockSpec((tm, d), lambda i: (i, 0)),
        ],
        out_specs=pl.BlockSpec((tm, d), lambda i: (i, 0)),
        out_shape=jax.ShapeDtypeStruct((n, d), F32),
        scratch_shapes=[pltpu.VMEM((tm + 8, ch), F32)],
        compiler_params=_params("arbitrary"),
        name="conv_out_proj",
    )(proj, proj, proj, conv_w, o_gla, w_out, w_out, x)


def _gla_scores_factored(q, k, g):
    c_len = q.shape[0]
    score_rows = []
    for a in range(c_len // GLA_BLOCK):
        lo, hi = a * GLA_BLOCK, (a + 1) * GLA_BLOCK
        n_keys = LANES * -(-hi // LANES)
        g_first = g[lo:lo + 1, :]
        q_a = (q[lo:hi] * jnp.exp(g[lo:hi] - g_first)).astype(BF16)
        k_a = (k[:n_keys] * jnp.exp(jnp.minimum(g_first - g[:n_keys], GLA_MAX_BLOCK_DECAY)))
        s_a = _dot_nt(q_a, k_a.astype(BF16))
        row = lax.broadcasted_iota(jnp.int32, s_a.shape, 0) + lo
        col = lax.broadcasted_iota(jnp.int32, s_a.shape, 1)
        s_a = jnp.where(col <= row, s_a, 0.0)
        if n_keys < c_len:
            s_a = jnp.concatenate([s_a, jnp.zeros((GLA_BLOCK, c_len - n_keys), F32)], axis=1)
        score_rows.append(s_a)
    return jnp.concatenate(score_rows, axis=0)


def _gla_scores_pairwise(q, k, g):
    c_len = q.shape[0]
    n_sub = c_len // GLA_SUB
    score_rows = [jnp.zeros((GLA_SUB, c_len), F32)]
    for a in range(1, n_sub):
        lo = a * GLA_SUB
        n_keys = LANES * -(-lo // LANES)
        g_ref_row = g[lo:lo + 1, :]
        q_a = (q[lo:lo + GLA_SUB] * jnp.exp(g[lo:lo + GLA_SUB] - g_ref_row)).astype(BF16)
        k_a = (k[:n_keys] * jnp.exp(jnp.minimum(g_ref_row - g[:n_keys], 0.0))).astype(BF16)
        s_a = _dot_nt(q_a, k_a)
        col = lax.broadcasted_iota(jnp.int32, s_a.shape, 1)
        s_a = jnp.where(col < lo, s_a, 0.0)
        if n_keys < c_len:
            s_a = jnp.concatenate([s_a, jnp.zeros((GLA_SUB, c_len - n_keys), F32)], axis=1)
        score_rows.append(s_a)
    scores = jnp.concatenate(score_rows, axis=0)

    row = lax.broadcasted_iota(jnp.int32, (c_len, LANES), 0)
    lane = lax.broadcasted_iota(jnp.int32, (c_len, LANES), 1)
    delta = row % LANES - lane
    band_id = jnp.where((delta >= 0) & (delta <= row % GLA_SUB), delta, -1)
    band = jnp.zeros((c_len, LANES), F32)
    for d in range(GLA_SUB):
        k_d = k if d == 0 else pltpu.roll(k, d, 0)
        g_d = g if d == 0 else pltpu.roll(g, d, 0)
        e = jnp.exp(jnp.minimum(g - g_d, 0.0))
        diag = jnp.sum(q * k_d * e, axis=-1, keepdims=True)
        band = jnp.where(band_id == d, diag, band)
    zeros = jnp.zeros((LANES, LANES), F32)
    band_rows = []
    for t in range(c_len // LANES):
        tiles = [zeros] * (c_len // LANES)
        tiles[t] = band[t * LANES:(t + 1) * LANES]
        band_rows.append(jnp.concatenate(tiles, axis=1))
    return scores + jnp.concatenate(band_rows, axis=0)


def _gla_kernel(q_ref, k_ref, v_ref, r_ref, la_ref, tri_ref, gn_ref, o_ref, st_ref, g_ref,
                inter_ref, *, q_scale, dk, dv):
    c_len = q_ref.shape[0]

    @pl.when(pl.program_id(1) == 0)
    def _():
        st_ref[...] = jnp.zeros_like(st_ref)

    la = la_ref[...]
    tri = tri_ref[...]
    la_hi = la.astype(BF16)
    rem = la - la_hi.astype(F32)
    la_mid = rem.astype(BF16)
    la_lo = (rem - la_mid.astype(F32)).astype(BF16)
    g_all = _dot(tri, la_hi) + _dot(tri, la_mid) + _dot(tri, la_lo)
    g_ref[...] = g_all
    block_decay = [g_all[lo:lo + 1, :] - g_all[lo + GLA_BLOCK - 1:lo + GLA_BLOCK, :]
                   for lo in range(0, c_len, GLA_BLOCK)]
    worst_decay = jnp.max(jnp.concatenate(block_decay, axis=0))

    def head_operands(h):
        g = g_ref[:, h * dk:(h + 1) * dk]
        q = q_ref[:, h * dk:(h + 1) * dk].astype(F32) * q_scale
        k = k_ref[:, h * dk:(h + 1) * dk].astype(F32)
        return g, q, k, v_ref[:, h * dv:(h + 1) * dv]

    def finish_head(h, o):
        o = _rmsnorm(o, gn_ref[...])
        r = r_ref[:, h * dv:(h + 1) * dv].astype(F32)
        o_ref[:, h * dv:(h + 1) * dv] = (o * (r / (1.0 + jnp.exp(-r)))).astype(o_ref.dtype)

    for h in range(GLA_HEADS):
        g, q, k, v = head_operands(h)
        g_last = g[c_len - 1:c_len, :]
        st = st_ref[h]
        o_inter = _dot_nt((q * jnp.exp(g)).astype(BF16), st.astype(BF16))
        inter_ref[h] = o_inter
        k_dec = (k * jnp.exp(g_last - g)).astype(BF16)
        st_ref[h] = st * jnp.exp(g_last) + _dot_tn(v, k_dec)
        finish_head(h, o_inter + _dot(_gla_scores_factored(q, k, g).astype(BF16), v))

    @pl.when(worst_decay > GLA_MAX_BLOCK_DECAY)
    def _():
        for h in range(GLA_HEADS):
            g, q, k, v = head_operands(h)
            finish_head(h, inter_ref[h] + _dot(_gla_scores_pairwise(q, k, g).astype(BF16), v))


def _gla(proj, log_a, tri, norm_g, *, batch, seq, q_off, k_off, v_off, r_off, dk, dv):
    n = proj.shape[0]
    cpb = seq // GLA_CHUNK
    qk_w, vr_w = GLA_HEADS * dk, GLA_HEADS * dv
    row = lambda b, c: b * cpb + c
    kern = functools.partial(_gla_kernel, q_scale=float(dk) ** -0.5, dk=dk, dv=dv)
    return pl.pallas_call(
        kern,
        grid=(batch, cpb),
        in_specs=[
            pl.BlockSpec((GLA_CHUNK, qk_w), lambda b, c: (row(b, c), q_off // qk_w)),
            pl.BlockSpec((GLA_CHUNK, qk_w), lambda b, c: (row(b, c), k_off // qk_w)),
            pl.BlockSpec((GLA_CHUNK, vr_w), lambda b, c: (row(b, c), v_off // vr_w)),
            pl.BlockSpec((GLA_CHUNK, vr_w), lambda b, c: (row(b, c), r_off // vr_w)),
            pl.BlockSpec((GLA_CHUNK, qk_w), lambda b, c: (row(b, c), 0)),
            pl.BlockSpec((GLA_CHUNK, GLA_CHUNK), lambda b, c: (0, 0)),
            pl.BlockSpec((1, dv), lambda b, c: (0, 0)),
        ],
        out_specs=pl.BlockSpec((GLA_CHUNK, vr_w), lambda b, c: (row(b, c), 0)),
        out_shape=jax.ShapeDtypeStruct((n, vr_w), BF16),
        scratch_shapes=[pltpu.VMEM((GLA_HEADS, dv, dk), F32), pltpu.VMEM((GLA_CHUNK, qk_w), F32),
                        pltpu.VMEM((GLA_HEADS, GLA_CHUNK, dv), F32)],
        compiler_params=_params("parallel", "arbitrary"),
        name="gla",
    )(proj, proj, proj, proj, log_a, tri, norm_g)


def _attn_out_proj_kernel(a_ref, w_ref, x_ref, o_ref):
    tm, d = x_ref.shape
    acc = _dot(a_ref[...].reshape(tm, a_ref.shape[-1]), w_ref[...])
    acc = jnp.swapaxes(acc.reshape(DIL_MAX, tm // DIL_MAX, d), 0, 1).reshape(tm, d)
    o_ref[...] = x_ref[...] + acc


def _attn_out_proj_residual(attn, w, x, *, batch, seq, tm=512):
    n, d = x.shape
    per = tm // DIL_MAX
    tps = seq // tm
    attn = attn.reshape(batch, DIL_MAX, seq // DIL_MAX, attn.shape[1])
    return pl.pallas_call(
        _attn_out_proj_kernel,
        grid=(n // tm,),
        in_specs=[
            pl.BlockSpec((None, DIL_MAX, per, attn.shape[-1]), lambda i: (i // tps, 0, i % tps, 0)),
            pl.BlockSpec(w.shape, lambda i: (0, 0)),
            pl.BlockSpec((tm, d), lambda i: (i, 0)),
        ],
        out_specs=pl.BlockSpec((tm, d), lambda i: (i, 0)),
        out_shape=jax.ShapeDtypeStruct((n, d), F32),
        compiler_params=_params("parallel"),
        name="attn_out_proj",
    )(attn, w, x)


def _mlp_kernel(x_ref, g_ref, w1_ref, w2_ref, gf_ref, *rest, final_norm, n_cast):
    cast_in, o_ref = rest[:n_cast], rest[n_cast]
    cast_out, xn_ref = rest[n_cast + 1:2 * n_cast + 1], rest[2 * n_cast + 1]
    f = pl.program_id(1)

    @pl.when(f == 0)
    def _():
        x = x_ref[...]
        xn_ref[...] = _rmsnorm(x, g_ref[...]).astype(BF16)
        o_ref[...] = x

    for src, dst in zip(cast_in, cast_out):
        dst[...] = src[...].astype(dst.dtype)

    h = _dot(xn_ref[...], w1_ref[...])
    h = jnp.square(jnp.maximum(h, 0.0)).astype(BF16)
    o_ref[...] += _dot(h, w2_ref[...])

    if final_norm:
        @pl.when(f == pl.num_programs(1) - 1)
        def _():
            o_ref[...] = _rmsnorm(o_ref[...], gf_ref[...])


def _mlp_residual(x, gain, w1, w2, final_gain, *, final_norm, cast_layers=(), tm=512, tf=2048):
    n, d = x.shape
    d_ff = w1.shape[1]
    steps = (n // tm) * (d_ff // tf)
    kern = functools.partial(_mlp_kernel, final_norm=final_norm, n_cast=len(cast_layers))
    cast_in_specs, cast_out_specs, cast_shapes = [], [], []
    for arr, layer in cast_layers:
        rows, cols = arr.shape[1:]
        slab = rows // steps
        assert rows % steps == 0 and slab % 16 == 0
        cast_in_specs.append(pl.BlockSpec(
            (None, slab, cols), lambda i, f, layer=layer: (layer, i * (d_ff // tf) + f, 0)))
        cast_out_specs.append(pl.BlockSpec((slab, cols), lambda i, f: (i * (d_ff // tf) + f, 0)))
        cast_shapes.append(jax.ShapeDtypeStruct((rows, cols), BF16))
    vmem = (2 * (2 * _nbytes((tm, d), F32) + 2 * _nbytes((d, tf), BF16)) + _nbytes((tm, d), BF16)
            + _nbytes((tm, tf), F32) + _nbytes((tm, tf), BF16))
    vmem += sum(2 * (_nbytes(s.block_shape[1:], F32) + _nbytes(s.block_shape[1:], BF16))
                for s in cast_in_specs)
    vmem_limit = min(vmem + (4 << 20), V7X_VMEM_BYTES)
    out = pl.pallas_call(
        kern,
        grid=(n // tm, d_ff // tf),
        in_specs=[
            pl.BlockSpec((tm, d), lambda i, f: (i, 0)),
            pl.BlockSpec((1, d), lambda i, f: (0, 0)),
            pl.BlockSpec((d, tf), lambda i, f: (0, f)),
            pl.BlockSpec((tf, d), lambda i, f: (f, 0)),
            pl.BlockSpec((1, d), lambda i, f: (0, 0)),
        ] + cast_in_specs,
        out_specs=[pl.BlockSpec((tm, d), lambda i, f: (i, 0))] + cast_out_specs,
        out_shape=[jax.ShapeDtypeStruct((n, d), F32)] + cast_shapes,
        scratch_shapes=[pltpu.VMEM((tm, d), BF16)],
        compiler_params=_params("arbitrary", "arbitrary", vmem_limit_bytes=vmem_limit),
        name="mlp",
    )(x, gain, w1, w2, final_gain, *[arr for arr, _ in cast_layers])
    return out[0], out[1:]


def _attn_bias_tables(slopes, sub):
    def table(q_pos, k_pos, dilation):
        diff = q_pos[:, None] - k_pos[None, :]
        valid = jnp.asarray((diff >= 0) & (diff <= ATTN_STEPS))
        dist = jnp.asarray((dilation * diff).astype(np.float32))
        bias = -(slopes * LOG2_E)[:, None, None] * dist[None]
        return jnp.where(valid[None], bias, MASK_VALUE).astype(F32)

    q = np.arange(128)
    b16_first = table(q, q, 16)
    b16 = table(q + 128, np.arange(256), 16)

    q4 = 4 * (q % 32) + q // 32
    k = np.arange(256)
    k4 = 4 * (k % 64) + k // 64
    b4 = jnp.stack([table(q4, k4, 4), table(q4 + 128, k4, 4)], axis=1)

    q = np.arange(256)
    q1 = 16 * (q % 16) + q // 16
    k = np.arange(384)
    k1 = 16 * (k % 24) + k // 24
    b1 = jnp.stack([table(q1, k1, 1), table(q1 + 128, k1, 1)], axis=1)
    return b1, b4, b16_first, b16


def _attn_kernel(q_all, k_all, v_all, b1_all, b4_all, b16f_all, b16_all, o_all, *scratch, sub,
                 heads):
    def gather(ref, starts, size):
        return jnp.concatenate([ref[pl.ds(s, size), :] for s in starts], axis=0)

    def scatter(ref, starts, size, val):
        for c, s in enumerate(starts):
            ref[pl.ds(s, size), :] = val[c * size:(c + 1) * size]

    def lanes(x):
        return jnp.broadcast_to(x, (x.shape[0], LANES))

    def across_keys(x, n_keys):
        return jnp.concatenate([x] * (n_keys // LANES), axis=1)

    def weighted_values_and_sum(p, v):
        pv = _dot(p.astype(BF16), jnp.concatenate([v, jnp.ones_like(v)], axis=1))
        return pv[:, :LANES], pv[:, LANES:]

    def first_tile(q, k, v, bias):
        s = _dot_nt(q, k) + bias
        m = jnp.max(s, axis=-1, keepdims=True)
        acc, l = weighted_values_and_sum(jnp.exp2(s - m), v)
        return lanes(m), l, acc

    def next_tile(q, k, v, bias, m_old, l_old, acc_old):
        s = _dot_nt(q, k) + bias
        m_new = jnp.maximum(m_old, lanes(jnp.max(s, axis=-1, keepdims=True)))
        alpha = jnp.exp2(m_old - m_new)
        acc, l = weighted_values_and_sum(jnp.exp2(s - across_keys(m_new, s.shape[1])), v)
        return m_new, alpha * l_old + l, alpha * acc_old + acc

    dh = q_all.shape[1] // heads

    def loop(trips, body):
        if trips == 1:
            body(0, 0)
        else:
            lax.fori_loop(0, trips, body, 0)

    blocks_4 = min(BLOCKS_4, sub // 32)
    blocks_1 = min(BLOCKS_1, sub // 16)
    for group in (16, 4, 1):
        for h in range(heads):
            cols = slice(h * dh, (h + 1) * dh)
            head_refs = (q_all.at[:, cols], k_all.at[:, cols], v_all.at[:, cols], o_all.at[:, cols],
                         b1_all.at[h], b4_all.at[h], b16f_all.at[h], b16_all.at[h],
                         *scratch[3 * h:3 * h + 3])
            _attn_group(group, head_refs, sub, blocks_4, blocks_1, loop, first_tile, next_tile,
                        gather, scatter)


def _attn_group(group, head_refs, sub, blocks_4, blocks_1, loop, first_tile, next_tile, gather,
                scatter):
    (q_ref, k_ref, v_ref, o_ref, b1_ref, b4_ref, b16f_ref, b16_ref, m_ref, l_ref, acc_ref) = head_refs

    def group16(i, carry):
        bias_first, bias = b16f_ref[...], b16_ref[...]
        rows, res = [], []
        for u in range(TILES_16):
            base = (TILES_16 * i + u) * sub
            for j in range(sub // ATTN_STEPS):
                q_rows = pl.ds(pl.multiple_of(base + j * ATTN_STEPS, ATTN_STEPS), ATTN_STEPS)
                if j == 0:
                    k_rows, b = q_rows, bias_first
                else:
                    k_rows = pl.ds(pl.multiple_of(base + (j - 1) * ATTN_STEPS, ATTN_STEPS),
                                   2 * ATTN_STEPS)
                    b = bias
                rows.append(q_rows)
                res.append(first_tile(q_ref[q_rows, :], k_ref[k_rows, :], v_ref[k_rows, :], b))
        for r, (m, l, acc) in zip(rows, res):
            m_ref[r, :] = m
            l_ref[r, :] = l
            acc_ref[r, :] = acc
        return carry

    def group4(i, carry):
        starts, res = [], []
        for u in range(blocks_4):
            blk = blocks_4 * i + u
            bias = b4_ref[jnp.where(blk == 0, 0, 1)]
            k_blk = jnp.maximum(blk - 1, 0)
            for r4 in range(4):
                qs = [pl.multiple_of((r4 + 4 * c) * sub + 32 * blk, 32) for c in range(4)]
                ks = [pl.multiple_of((r4 + 4 * c) * sub + 32 * k_blk, 32) for c in range(4)]
                starts.append(qs)
                res.append(next_tile(
                    gather(q_ref, qs, 32), gather(k_ref, ks, 64), gather(v_ref, ks, 64), bias,
                    gather(m_ref, qs, 32), gather(l_ref, qs, 32), gather(acc_ref, qs, 32)))
        for qs, (m, l, acc) in zip(starts, res):
            scatter(m_ref, qs, 32, m)
            scatter(l_ref, qs, 32, l)
            scatter(acc_ref, qs, 32, acc)
        return carry

    def group1(i, carry):
        res = []
        for u in range(blocks_1):
            blk = blocks_1 * i + u
            bias = b1_ref[jnp.where(blk == 0, 0, 1)]
            k_lo = jnp.maximum(16 * blk - 8, 0)
            qs = [pl.multiple_of(r * sub + 16 * blk, 16) for r in range(DIL_MAX)]
            ks = [pl.multiple_of(r * sub + k_lo, 8) for r in range(DIL_MAX)]
            _, l, acc = next_tile(
                gather(q_ref, qs, 16), gather(k_ref, ks, 24), gather(v_ref, ks, 24), bias,
                gather(m_ref, qs, 16), gather(l_ref, qs, 16), gather(acc_ref, qs, 16))
            res.append((qs, (acc / l).astype(o_ref.dtype)))
        for qs, out in res:
            scatter(o_ref, qs, 16, out)
        return carry

    if group == 16:
        loop(DIL_MAX // TILES_16, group16)
    elif group == 4:
        loop(sub // (32 * blocks_4), group4)
    else:
        loop(sub // (16 * blocks_1), group1)


def _dilated_attention(qkv, slopes, *, batch, seq, dh):
    n = qkv.shape[0]
    sub = seq // DIL_MAX
    b1, b4, b16_first, b16 = _attn_bias_tables(slopes, sub)
    hp = ATTN_HEADS_PER_STEP
    groups = ATTN_HEADS // hp
    kern = functools.partial(_attn_kernel, sub=sub, heads=hp)
    return pl.pallas_call(
        kern,
        grid=(batch, groups),
        in_specs=[
            pl.BlockSpec((seq, hp * dh), lambda b, h: (b, h)),
            pl.BlockSpec((seq, hp * dh), lambda b, h: (b, groups + h)),
            pl.BlockSpec((seq, hp * dh), lambda b, h: (b, 2 * groups + h)),
            pl.BlockSpec((hp,) + b1.shape[1:], lambda b, h: (h, 0, 0, 0)),
            pl.BlockSpec((hp,) + b4.shape[1:], lambda b, h: (h, 0, 0, 0)),
            pl.BlockSpec((hp,) + b16_first.shape[1:], lambda b, h: (h, 0, 0)),
            pl.BlockSpec((hp,) + b16.shape[1:], lambda b, h: (h, 0, 0)),
        ],
        out_specs=pl.BlockSpec((seq, hp * dh), lambda b, h: (b, h)),
        out_shape=jax.ShapeDtypeStruct((n, ATTN_HEADS * dh), BF16),
        scratch_shapes=[pltpu.VMEM((seq, LANES), F32), pltpu.VMEM((seq, LANES), F32),
                        pltpu.VMEM((seq, dh), F32)] * hp,
        compiler_params=_params("parallel", "parallel"),
        name="dilated_attn",
    )(qkv, qkv, qkv, b1, b4, b16_first, b16)


def kernel(x, norm_mix_g, norm_mlp_g, final_norm_g, hyb_w_in, conv_w, gla_w_gate2, gla_b_gate,
           gla_norm_g, hyb_w_out, attn_w_qkv, attn_w_o, mlp_w1, mlp_w2):
    batch, seq, d = x.shape
    n = batch * seq
    depth = norm_mix_g.shape[0]
    conv_ch = conv_w.shape[-1]
    dv = gla_norm_g.shape[-1]
    dk = gla_w_gate2.shape[-1] // GLA_HEADS
    dh = d // ATTN_HEADS
    assert depth == 2 and seq % (DIL_MAX * ATTN_STEPS) == 0 and seq % GLA_CHUNK == 0
    assert dk == LANES and dh == LANES and dv % LANES == 0

    row = lambda v: v.reshape(1, -1).astype(F32)
    xf = x.reshape(n, d)

    main_cols = 3 * conv_ch + 2 * GLA_HEADS * dk + 2 * GLA_HEADS * dv
    w_in = hyb_w_in[0]
    w_low = jnp.pad(w_in[:, main_cols:], ((0, 0), (0, LANES - GLA_GATE_RANK))).astype(BF16)
    w_gate = jnp.pad(gla_w_gate2[0], ((0, LANES - GLA_GATE_RANK), (0, 0))).astype(BF16)
    proj, log_a, (w1_first, w2_first) = _in_proj_gla(
        xf, row(norm_mix_g[0]), w_in.astype(BF16), w_low, w_gate, row(gla_b_gate[0]),
        cols=main_cols, cast_layers=((mlp_w1, 0), (mlp_w2, 0)))

    q_off = 3 * conv_ch
    k_off = q_off + GLA_HEADS * dk
    v_off = k_off + GLA_HEADS * dk
    r_off = v_off + GLA_HEADS * dv
    tri = jnp.asarray(np.tril(np.ones((GLA_CHUNK, GLA_CHUNK), np.float32)), BF16)
    o_gla = _gla(proj, log_a, tri, row(gla_norm_g[0]), batch=batch, seq=seq,
                 q_off=q_off, k_off=k_off, v_off=v_off, r_off=r_off, dk=dk, dv=dv)
    xf = _conv_out_proj_residual(proj, conv_w[0].astype(F32), o_gla, hyb_w_out[0].astype(BF16), xf,
                                 seq=seq)
    xf, (w1_next, w2_next) = _mlp_residual(
        xf, row(norm_mlp_g[0]), w1_first, w2_first, row(final_norm_g),
        final_norm=False, cast_layers=((mlp_w1, 1), (mlp_w2, 1)))

    col_scale = jnp.concatenate([jnp.full((1, d), float(dh) ** -0.5 * LOG2_E, F32),
                                 jnp.ones((1, 2 * d), F32)], axis=1)
    qkv = _qkv_proj_residue_major(xf, row(norm_mix_g[1]), attn_w_qkv[0].astype(BF16), col_scale,
                                  batch=batch, seq=seq)
    slopes = jnp.exp2(-8.0 * jnp.arange(1, ATTN_HEADS + 1, dtype=F32) / ATTN_HEADS)
    attn = _dilated_attention(qkv, slopes, batch=batch, seq=seq, dh=dh)
    xf = _attn_out_proj_residual(attn, attn_w_o[0].astype(BF16), xf, batch=batch, seq=seq)
    out, _ = _mlp_residual(xf, row(norm_mlp_g[1]), w1_next, w2_next, row(final_norm_g),
                           final_norm=True)
    return out.reshape(batch, seq, d)
```

```python
import functools

import jax
import jax.numpy as jnp
import numpy as np
from jax import lax
from jax.experimental import pallas as pl
from jax.experimental.pallas import tpu as pltpu

F32 = jnp.float32
BF16 = jnp.bfloat16

NORM_EPS = 1e-6
CONV_K = 3
GLA_HEADS = 4
GLA_GATE_RANK = 16
GLA_GATE_TEMP = 16.0
ATTN_HEADS = 16
ATTN_HEADS_PER_STEP = 2
DILATED_GROUPS = ((128, 1), (512, 4), (2048, 16))
DIL_MAX = 16
ATTN_STEPS = 128

LANES = 128
GLA_CHUNK = 256
GLA_SUB = 16
GLA_BLOCK = 64
GLA_MAX_BLOCK_DECAY = 60.0
MASK_VALUE = -1e30
LOG2_E = 1.4426950408889634
TILES_16 = 16
BLOCKS_4 = 8
BLOCKS_1 = 16
V7X_VMEM_BYTES = 64 * 1024 * 1024
VMEM_LIMIT_BYTES = 56 * 1024 * 1024


def _params(*semantics, vmem_limit_bytes=VMEM_LIMIT_BYTES):
    return pltpu.CompilerParams(dimension_semantics=semantics, vmem_limit_bytes=vmem_limit_bytes)


def _nbytes(shape, dtype):
    return int(np.prod(shape)) * jnp.dtype(dtype).itemsize


def _dot(a, b):
    return jnp.dot(a, b, preferred_element_type=F32)


def _dot_nt(a, b):
    return lax.dot_general(a, b, (((1,), (1,)), ((), ())), preferred_element_type=F32)


def _dot_tn(a, b):
    return lax.dot_general(a, b, (((0,), (0,)), ((), ())), preferred_element_type=F32)


def _rmsnorm(x, g):
    return x * lax.rsqrt(jnp.mean(x * x, axis=-1, keepdims=True) + NORM_EPS) * g


def _tile_of_trip(i):
    return jnp.maximum(i - 1, 0)


def _chunk_index(i, j, n_tiles, chunks):
    return jnp.minimum(i, n_tiles - 1) * chunks + j


def _in_proj_gla_kernel(xc_ref, g_ref, w_ref, wlow_ref, wgate_ref, bgate_ref, *rest, n_cast):
    cast_in, (o_ref, loga_ref) = rest[:n_cast], rest[n_cast:n_cast + 2]
    cast_out, xn_ref = rest[n_cast + 2:2 * n_cast + 2], rest[2 * n_cast + 2]
    i, j = pl.program_id(0), pl.program_id(1)
    ch = xc_ref.shape[0]
    fill = i % 2
    rows = pl.ds(pl.multiple_of(j * ch, ch), ch)

    def norm_and_gate_chunk():
        xn = _rmsnorm(xc_ref[...], g_ref[...]).astype(BF16)
        xn_ref[fill, rows, :] = xn
        g_low = _dot(xn, wlow_ref[...]).astype(BF16)
        gate = _dot(g_low, wgate_ref[...]) + bgate_ref[...]
        log_sig = jnp.minimum(gate, 0.0) - jnp.log(1.0 + jnp.exp(-jnp.abs(gate)))
        loga_ref[rows, :] = log_sig * (1.0 / GLA_GATE_TEMP)
        for src, dst in zip(cast_in, cast_out):
            dst[...] = src[...].astype(dst.dtype)

    @pl.when(i == 0)
    def _():
        norm_and_gate_chunk()

    @pl.when(i > 0)
    def _():
        norm_and_gate_chunk()
        o_ref[...] = _dot(xn_ref[1 - fill], w_ref[...]).astype(o_ref.dtype)


def _in_proj_gla(x, gain, w, w_low, w_gate, b_gate, *, cols, cast_layers=(), tm=1024, tn=1536):
    n, d = x.shape
    n_gate = w_gate.shape[1]
    n_tiles, steps = n // tm, cols // tn
    ch = tm // steps
    cast_in_specs, cast_out_specs, cast_shapes = [], [], []
    for arr, layer in cast_layers:
        rows, width = arr.shape[1:]
        slab = rows // (n_tiles * steps)
        assert rows % (n_tiles * steps) == 0 and slab % 16 == 0
        last = n_tiles * steps - 1
        cast_in_specs.append(pl.BlockSpec(
            (None, slab, width),
            lambda i, j, layer=layer, last=last: (layer, jnp.minimum(i * steps + j, last), 0)))
        cast_out_specs.append(pl.BlockSpec(
            (slab, width), lambda i, j, last=last: (jnp.minimum(i * steps + j, last), 0)))
        cast_shapes.append(jax.ShapeDtypeStruct((rows, width), BF16))
    kern = functools.partial(_in_proj_gla_kernel, n_cast=len(cast_layers))
    out = pl.pallas_call(
        kern,
        grid=(n_tiles + 1, steps),
        in_specs=[
            pl.BlockSpec((ch, d), lambda i, j: (_chunk_index(i, j, n_tiles, steps), 0)),
            pl.BlockSpec((1, d), lambda i, j: (0, 0)),
            pl.BlockSpec((d, tn), lambda i, j: (0, j)),
            pl.BlockSpec((d, LANES), lambda i, j: (0, 0)),
            pl.BlockSpec((LANES, n_gate), lambda i, j: (0, 0)),
            pl.BlockSpec((1, n_gate), lambda i, j: (0, 0)),
        ] + cast_in_specs,
        out_specs=[
            pl.BlockSpec((tm, tn), lambda i, j: (_tile_of_trip(i), j * jnp.minimum(i, 1))),
            pl.BlockSpec((tm, n_gate), lambda i, j: (jnp.minimum(i, n_tiles - 1), 0)),
        ] + cast_out_specs,
        out_shape=[
            jax.ShapeDtypeStruct((n, cols), BF16),
            jax.ShapeDtypeStruct((n, n_gate), F32),
        ] + cast_shapes,
        scratch_shapes=[pltpu.VMEM((2, tm, d), BF16)],
        compiler_params=_params("arbitrary", "arbitrary"),
        name="in_proj_gla",
    )(x, gain, w, w_low, w_gate, b_gate, *[arr for arr, _ in cast_layers])
    return out[0], out[1], out[2:]


def _qkv_proj_kernel(xc_ref, g_ref, w_ref, scale_ref, o_ref, xn_ref):
    i, j = pl.program_id(0), pl.program_id(1)
    ch, d = xc_ref.shape
    per = ch // DIL_MAX
    fill = i % 2

    def norm_chunk():
        xn = _rmsnorm(xc_ref[...], g_ref[...]).astype(BF16)
        xn_ref[fill, :, pl.ds(pl.multiple_of(j * per, per), per), :] = (
            jnp.swapaxes(xn.reshape(per, DIL_MAX, d), 0, 1))

    @pl.when(i == 0)
    def _():
        norm_chunk()

    @pl.when(i > 0)
    def _():
        norm_chunk()
        xn = xn_ref[1 - fill]
        acc = _dot(xn.reshape(xn.shape[0] * xn.shape[1], d), w_ref[...]) * scale_ref[...]
        o_ref[...] = acc.astype(o_ref.dtype).reshape(o_ref.shape)


def _qkv_proj_residue_major(x, gain, w, col_scale, *, batch, seq, tm=1024, tn=1536):
    n, d = x.shape
    cols = w.shape[1]
    sub = seq // DIL_MAX
    per = tm // DIL_MAX
    tiles_per_seq = seq // tm
    n_tiles, steps = n // tm, cols // tn
    ch = tm // steps
    assert ch % (DIL_MAX * 16) == 0

    def out_index(i, j):
        t = _tile_of_trip(i)
        return (t // tiles_per_seq, 0, t % tiles_per_seq, j * jnp.minimum(i, 1))

    out = pl.pallas_call(
        _qkv_proj_kernel,
        grid=(n_tiles + 1, steps),
        in_specs=[
            pl.BlockSpec((ch, d), lambda i, j: (_chunk_index(i, j, n_tiles, steps), 0)),
            pl.BlockSpec((1, d), lambda i, j: (0, 0)),
            pl.BlockSpec((d, tn), lambda i, j: (0, j)),
            pl.BlockSpec((1, tn), lambda i, j: (0, j)),
        ],
        out_specs=pl.BlockSpec((None, DIL_MAX, per, tn), out_index),
        out_shape=jax.ShapeDtypeStruct((batch, DIL_MAX, sub, cols), BF16),
        scratch_shapes=[pltpu.VMEM((2, DIL_MAX, per, d), BF16)],
        compiler_params=_params("arbitrary", "arbitrary"),
        name="qkv_proj",
    )(x, gain, w, col_scale)
    return out.reshape(n, cols)


def _conv_out_proj_kernel(ab_ref, ac_ref, ax_ref, cw_ref, gla_ref, wa_ref, wb_ref, x_ref, o_ref,
                          ubuf_ref, *, tiles_per_seq):
    tb = ab_ref.shape[0]
    pad = 8

    @pl.when(pl.program_id(0) % tiles_per_seq == 0)
    def _():
        ubuf_ref[0:pad, :] = jnp.zeros((pad, ubuf_ref.shape[1]), F32)

    acc = x_ref[...] + _dot(gla_ref[...], wb_ref[...])
    u = ac_ref[...].astype(F32) * ax_ref[...].astype(F32)
    ubuf_ref[pad:pad + tb, :] = u
    u1 = ubuf_ref[pad - 1:pad - 1 + tb, :]
    u2 = ubuf_ref[pad - 2:pad - 2 + tb, :]
    cw = cw_ref[...]
    y = ab_ref[...].astype(F32) * (cw[0:1, :] * u2 + cw[1:2, :] * u1 + cw[2:3, :] * u)
    o_ref[...] = acc + _dot(y.astype(BF16), wa_ref[...])
    ubuf_ref[0:pad, :] = ubuf_ref[tb:tb + pad, :]


def _conv_out_proj_residual(proj, conv_w, o_gla, w_out, x, *, seq, tm=512):
    n, d = x.shape
    ch = conv_w.shape[1]
    assert w_out.shape[0] == 2 * ch and o_gla.shape[1] == ch
    kern = functools.partial(_conv_out_proj_kernel, tiles_per_seq=seq // tm)
    return pl.pallas_call(
        kern,
        grid=(n // tm,),
        in_specs=[
            pl.BlockSpec((tm, ch), lambda i: (i, 0)),
            pl.BlockSpec((tm, ch), lambda i: (i, 1)),
            pl.BlockSpec((tm, ch), lambda i: (i, 2)),
            pl.BlockSpec((CONV_K, ch), lambda i: (0, 0)),
            pl.BlockSpec((tm, o_gla.shape[1]), lambda i: (i, 0)),
            pl.BlockSpec((ch, d), lambda i: (0, 0)),
            pl.BlockSpec((ch, d), lambda i: (1, 0)),
            pl.BlockSpec((tm, d), lambda i: (i, 0)),
        ],
        out_specs=pl.BlockSpec((tm, d), lambda i: (i, 0)),
        out_shape=jax.ShapeDtypeStruct((n, d), F32),
        scratch_shapes=[pltpu.VMEM((tm + 8, ch), F32)],
        compiler_params=_params("arbitrary"),
        name="conv_out_proj",
    )(proj, proj, proj, conv_w, o_gla, w_out, w_out, x)


def _gla_scores_factored(q, k, g):
    c_len = q.shape[0]
    score_rows = []
    for a in range(c_len // GLA_BLOCK):
        lo, hi = a * GLA_BLOCK, (a + 1) * GLA_BLOCK
        n_keys = LANES * -(-hi // LANES)
        g_first = g[lo:lo + 1, :]
        q_a = (q[lo:hi] * jnp.exp(g[lo:hi] - g_first)).astype(BF16)
        k_a = (k[:n_keys] * jnp.exp(jnp.minimum(g_first - g[:n_keys], GLA_MAX_BLOCK_DECAY)))
        s_a = _dot_nt(q_a, k_a.astype(BF16))
        row = lax.broadcasted_iota(jnp.int32, s_a.shape, 0) + lo
        col = lax.broadcasted_iota(jnp.int32, s_a.shape, 1)
        s_a = jnp.where(col <= row, s_a, 0.0)
        if n_keys < c_len:
            s_a = jnp.concatenate([s_a, jnp.zeros((GLA_BLOCK, c_len - n_keys), F32)], axis=1)
        score_rows.append(s_a)
    return jnp.concatenate(score_rows, axis=0)


def _gla_scores_pairwise(q, k, g):
    c_len = q.shape[0]
    n_sub = c_len // GLA_SUB
    score_rows = [jnp.zeros((GLA_SUB, c_len), F32)]
    for a in range(1, n_sub):
        lo = a * GLA_SUB
        n_keys = LANES * -(-lo // LANES)
        g_ref_row = g[lo:lo + 1, :]
        q_a = (q[lo:lo + GLA_SUB] * jnp.exp(g[lo:lo + GLA_SUB] - g_ref_row)).astype(BF16)
        k_a = (k[:n_keys] * jnp.exp(jnp.minimum(g_ref_row - g[:n_keys], 0.0))).astype(BF16)
        s_a = _dot_nt(q_a, k_a)
        col = lax.broadcasted_iota(jnp.int32, s_a.shape, 1)
        s_a = jnp.where(col < lo, s_a, 0.0)
        if n_keys < c_len:
            s_a = jnp.concatenate([s_a, jnp.zeros((GLA_SUB, c_len - n_keys), F32)], axis=1)
        score_rows.append(s_a)
    scores = jnp.concatenate(score_rows, axis=0)

    row = lax.broadcasted_iota(jnp.int32, (c_len, LANES), 0)
    lane = lax.broadcasted_iota(jnp.int32, (c_len, LANES), 1)
    delta = row % LANES - lane
    band_id = jnp.where((delta >= 0) & (delta <= row % GLA_SUB), delta, -1)
    band = jnp.zeros((c_len, LANES), F32)
    for d in range(GLA_SUB):
        k_d = k if d == 0 else pltpu.roll(k, d, 0)
        g_d = g if d == 0 else pltpu.roll(g, d, 0)
        e = jnp.exp(jnp.minimum(g - g_d, 0.0))
        diag = jnp.sum(q * k_d * e, axis=-1, keepdims=True)
        band = jnp.where(band_id == d, diag, band)
    zeros = jnp.zeros((LANES, LANES), F32)
    band_rows = []
    for t in range(c_len // LANES):
        tiles = [zeros] * (c_len // LANES)
        tiles[t] = band[t * LANES:(t + 1) * LANES]
        band_rows.append(jnp.concatenate(tiles, axis=1))
    return scores + jnp.concatenate(band_rows, axis=0)


def _gla_kernel(q_ref, k_ref, v_ref, r_ref, la_ref, tri_ref, gn_ref, o_ref, st_ref, g_ref,
                inter_ref, *, q_scale, dk, dv):
    c_len = q_ref.shape[0]

    @pl.when(pl.program_id(1) == 0)
    def _():
        st_ref[...] = jnp.zeros_like(st_ref)

    la = la_ref[...]
    tri = tri_ref[...]
    la_hi = la.astype(BF16)
    rem = la - la_hi.astype(F32)
    la_mid = rem.astype(BF16)
    la_lo = (rem - la_mid.astype(F32)).astype(BF16)
    g_all = _dot(tri, la_hi) + _dot(tri, la_mid) + _dot(tri, la_lo)
    g_ref[...] = g_all
    block_decay = [g_all[lo:lo + 1, :] - g_all[lo + GLA_BLOCK - 1:lo + GLA_BLOCK, :]
                   for lo in range(0, c_len, GLA_BLOCK)]
    worst_decay = jnp.max(jnp.concatenate(block_decay, axis=0))

    def head_operands(h):
        g = g_ref[:, h * dk:(h + 1) * dk]
        q = q_ref[:, h * dk:(h + 1) * dk].astype(F32) * q_scale
        k = k_ref[:, h * dk:(h + 1) * dk].astype(F32)
        return g, q, k, v_ref[:, h * dv:(h + 1) * dv]

    def finish_head(h, o):
        o = _rmsnorm(o, gn_ref[...])
        r = r_ref[:, h * dv:(h + 1) * dv].astype(F32)
        o_ref[:, h * dv:(h + 1) * dv] = (o * (r / (1.0 + jnp.exp(-r)))).astype(o_ref.dtype)

    for h in range(GLA_HEADS):
        g, q, k, v = head_operands(h)
        g_last = g[c_len - 1:c_len, :]
        st = st_ref[h]
        o_inter = _dot_nt((q * jnp.exp(g)).astype(BF16), st.astype(BF16))
        inter_ref[h] = o_inter
        k_dec = (k * jnp.exp(g_last - g)).astype(BF16)
        st_ref[h] = st * jnp.exp(g_last) + _dot_tn(v, k_dec)
        finish_head(h, o_inter + _dot(_gla_scores_factored(q, k, g).astype(BF16), v))

    @pl.when(worst_decay > GLA_MAX_BLOCK_DECAY)
    def _():
        for h in range(GLA_HEADS):
            g, q, k, v = head_operands(h)
            finish_head(h, inter_ref[h] + _dot(_gla_scores_pairwise(q, k, g).astype(BF16), v))


def _gla(proj, log_a, tri, norm_g, *, batch, seq, q_off, k_off, v_off, r_off, dk, dv):
    n = proj.shape[0]
    cpb = seq // GLA_CHUNK
    qk_w, vr_w = GLA_HEADS * dk, GLA_HEADS * dv
    row = lambda b, c: b * cpb + c
    kern = functools.partial(_gla_kernel, q_scale=float(dk) ** -0.5, dk=dk, dv=dv)
    return pl.pallas_call(
        kern,
        grid=(batch, cpb),
        in_specs=[
            pl.BlockSpec((GLA_CHUNK, qk_w), lambda b, c: (row(b, c), q_off // qk_w)),
            pl.BlockSpec((GLA_CHUNK, qk_w), lambda b, c: (row(b, c), k_off // qk_w)),
            pl.BlockSpec((GLA_CHUNK, vr_w), lambda b, c: (row(b, c), v_off // vr_w)),
            pl.BlockSpec((GLA_CHUNK, vr_w), lambda b, c: (row(b, c), r_off // vr_w)),
            pl.BlockSpec((GLA_CHUNK, qk_w), lambda b, c: (row(b, c), 0)),
            pl.BlockSpec((GLA_CHUNK, GLA_CHUNK), lambda b, c: (0, 0)),
            pl.BlockSpec((1, dv), lambda b, c: (0, 0)),
        ],
        out_specs=pl.BlockSpec((GLA_CHUNK, vr_w), lambda b, c: (row(b, c), 0)),
        out_shape=jax.ShapeDtypeStruct((n, vr_w), BF16),
        scratch_shapes=[pltpu.VMEM((GLA_HEADS, dv, dk), F32), pltpu.VMEM((GLA_CHUNK, qk_w), F32),
                        pltpu.VMEM((GLA_HEADS, GLA_CHUNK, dv), F32)],
        compiler_params=_params("parallel", "arbitrary"),
        name="gla",
    )(proj, proj, proj, proj, log_a, tri, norm_g)


def _attn_out_proj_kernel(a_ref, w_ref, x_ref, o_ref):
    tm, d = x_ref.shape
    acc = _dot(a_ref[...].reshape(tm, a_ref.shape[-1]), w_ref[...])
    acc = jnp.swapaxes(acc.reshape(DIL_MAX, tm // DIL_MAX, d), 0, 1).reshape(tm, d)
    o_ref[...] = x_ref[...] + acc


def _attn_out_proj_residual(attn, w, x, *, batch, seq, tm=512):
    n, d = x.shape
    per = tm // DIL_MAX
    tps = seq // tm
    attn = attn.reshape(batch, DIL_MAX, seq // DIL_MAX, attn.shape[1])
    return pl.pallas_call(
        _attn_out_proj_kernel,
        grid=(n // tm,),
        in_specs=[
            pl.BlockSpec((None, DIL_MAX, per, attn.shape[-1]), lambda i: (i // tps, 0, i % tps, 0)),
            pl.BlockSpec(w.shape, lambda i: (0, 0)),
            pl.BlockSpec((tm, d), lambda i: (i, 0)),
        ],
        out_specs=pl.BlockSpec((tm, d), lambda i: (i, 0)),
        out_shape=jax.ShapeDtypeStruct((n, d), F32),
        compiler_params=_params("parallel"),
        name="attn_out_proj",
    )(attn, w, x)


def _mlp_kernel(x_ref, g_ref, w1_ref, w2_ref, gf_ref, *rest, final_norm, n_cast):
    cast_in, o_ref = rest[:n_cast], rest[n_cast]
    cast_out, xn_ref = rest[n_cast + 1:2 * n_cast + 1], rest[2 * n_cast + 1]
    f = pl.program_id(1)

    @pl.when(f == 0)
    def _():
        x = x_ref[...]
        xn_ref[...] = _rmsnorm(x, g_ref[...]).astype(BF16)
        o_ref[...] = x

    for src, dst in zip(cast_in, cast_out):
        dst[...] = src[...].astype(dst.dtype)

    h = _dot(xn_ref[...], w1_ref[...])
    h = jnp.square(jnp.maximum(h, 0.0)).astype(BF16)
    o_ref[...] += _dot(h, w2_ref[...])

    if final_norm:
        @pl.when(f == pl.num_programs(1) - 1)
        def _():
            o_ref[...] = _rmsnorm(o_ref[...], gf_ref[...])


def _mlp_residual(x, gain, w1, w2, final_gain, *, final_norm, cast_layers=(), tm=512, tf=2048):
    n, d = x.shape
    d_ff = w1.shape[1]
    steps = (n // tm) * (d_ff // tf)
    kern = functools.partial(_mlp_kernel, final_norm=final_norm, n_cast=len(cast_layers))
    cast_in_specs, cast_out_specs, cast_shapes = [], [], []
    for arr, layer in cast_layers:
        rows, cols = arr.shape[1:]
        slab = rows // steps
        assert rows % steps == 0 and slab % 16 == 0
        cast_in_specs.append(pl.BlockSpec(
            (None, slab, cols), lambda i, f, layer=layer: (layer, i * (d_ff // tf) + f, 0)))
        cast_out_specs.append(pl.BlockSpec((slab, cols), lambda i, f: (i * (d_ff // tf) + f, 0)))
        cast_shapes.append(jax.ShapeDtypeStruct((rows, cols), BF16))
    vmem = (2 * (2 * _nbytes((tm, d), F32) + 2 * _nbytes((d, tf), BF16)) + _nbytes((tm, d), BF16)
            + _nbytes((tm, tf), F32) + _nbytes((tm, tf), BF16))
    vmem += sum(2 * (_nbytes(s.block_shape[1:], F32) + _nbytes(s.block_shape[1:], BF16))
                for s in cast_in_specs)
    vmem_limit = min(vmem + (4 << 20), V7X_VMEM_BYTES)
    out = pl.pallas_call(
        kern,
        grid=(n // tm, d_ff // tf),
        in_specs=[
            pl.BlockSpec((tm, d), lambda i, f: (i, 0)),
            pl.BlockSpec((1, d), lambda i, f: (0, 0)),
            pl.BlockSpec((d, tf), lambda i, f: (0, f)),
            pl.BlockSpec((tf, d), lambda i, f: (f, 0)),
            pl.BlockSpec((1, d), lambda i, f: (0, 0)),
        ] + cast_in_specs,
        out_specs=[pl.BlockSpec((tm, d), lambda i, f: (i, 0))] + cast_out_specs,
        out_shape=[jax.ShapeDtypeStruct((n, d), F32)] + cast_shapes,
        scratch_shapes=[pltpu.VMEM((tm, d), BF16)],
        compiler_params=_params("arbitrary", "arbitrary", vmem_limit_bytes=vmem_limit),
        name="mlp",
    )(x, gain, w1, w2, final_gain, *[arr for arr, _ in cast_layers])
    return out[0], out[1:]


def _attn_bias_tables(slopes, sub):
    def table(q_pos, k_pos, dilation):
        diff = q_pos[:, None] - k_pos[None, :]
        valid = jnp.asarray((diff >= 0) & (diff <= ATTN_STEPS))
        dist = jnp.asarray((dilation * diff).astype(np.float32))
        bias = -(slopes * LOG2_E)[:, None, None] * dist[None]
        return jnp.where(valid[None], bias, MASK_VALUE).astype(F32)

    q = np.arange(128)
    b16_first = table(q, q, 16)
    b16 = table(q + 128, np.arange(256), 16)

    q4 = 4 * (q % 32) + q // 32
    k = np.arange(256)
    k4 = 4 * (k % 64) + k // 64
    b4 = jnp.stack([table(q4, k4, 4), table(q4 + 128, k4, 4)], axis=1)

    q = np.arange(256)
    q1 = 16 * (q % 16) + q // 16
    k = np.arange(384)
    k1 = 16 * (k % 24) + k // 24
    b1 = jnp.stack([table(q1, k1, 1), table(q1 + 128, k1, 1)], axis=1)
    return b1, b4, b16_first, b16


def _attn_kernel(q_all, k_all, v_all, b1_all, b4_all, b16f_all, b16_all, o_all, *scratch, sub,
                 heads):
    def gather(ref, starts, size):
        return jnp.concatenate([ref[pl.ds(s, size), :] for s in starts], axis=0)

    def scatter(ref, starts, size, val):
        for c, s in enumerate(starts):
            ref[pl.ds(s, size), :] = val[c * size:(c + 1) * size]

    def lanes(x):
        return jnp.broadcast_to(x, (x.shape[0], LANES))

    def across_keys(x, n_keys):
        return jnp.concatenate([x] * (n_keys // LANES), axis=1)

    def weighted_values_and_sum(p, v):
        pv = _dot(p.astype(BF16), jnp.concatenate([v, jnp.ones_like(v)], axis=1))
        return pv[:, :LANES], pv[:, LANES:]

    def first_tile(q, k, v, bias):
        s = _dot_nt(q, k) + bias
        m = jnp.max(s, axis=-1, keepdims=True)
        acc, l = weighted_values_and_sum(jnp.exp2(s - m), v)
        return lanes(m), l, acc

    def next_tile(q, k, v, bias, m_old, l_old, acc_old):
        s = _dot_nt(q, k) + bias
        m_new = jnp.maximum(m_old, lanes(jnp.max(s, axis=-1, keepdims=True)))
        alpha = jnp.exp2(m_old - m_new)
        acc, l = weighted_values_and_sum(jnp.exp2(s - across_keys(m_new, s.shape[1])), v)
        return m_new, alpha * l_old + l, alpha * acc_old + acc

    dh = q_all.shape[1] // heads

    def loop(trips, body):
        if trips == 1:
            body(0, 0)
        else:
            lax.fori_loop(0, trips, body, 0)

    blocks_4 = min(BLOCKS_4, sub // 32)
    blocks_1 = min(BLOCKS_1, sub // 16)
    for group in (16, 4, 1):
        for h in range(heads):
            cols = slice(h * dh, (h + 1) * dh)
            head_refs = (q_all.at[:, cols], k_all.at[:, cols], v_all.at[:, cols], o_all.at[:, cols],
                         b1_all.at[h], b4_all.at[h], b16f_all.at[h], b16_all.at[h],
                         *scratch[3 * h:3 * h + 3])
            _attn_group(group, head_refs, sub, blocks_4, blocks_1, loop, first_tile, next_tile,
                        gather, scatter)


def _attn_group(group, head_refs, sub, blocks_4, blocks_1, loop, first_tile, next_tile, gather,
                scatter):
    (q_ref, k_ref, v_ref, o_ref, b1_ref, b4_ref, b16f_ref, b16_ref, m_ref, l_ref, acc_ref) = head_refs

    def group16(i, carry):
        bias_first, bias = b16f_ref[...], b16_ref[...]
        rows, res = [], []
        for u in range(TILES_16):
            base = (TILES_16 * i + u) * sub
            for j in range(sub // ATTN_STEPS):
                q_rows = pl.ds(pl.multiple_of(base + j * ATTN_STEPS, ATTN_STEPS), ATTN_STEPS)
                if j == 0:
                    k_rows, b = q_rows, bias_first
                else:
                    k_rows = pl.ds(pl.multiple_of(base + (j - 1) * ATTN_STEPS, ATTN_STEPS),
                                   2 * ATTN_STEPS)
                    b = bias
                rows.append(q_rows)
                res.append(first_tile(q_ref[q_rows, :], k_ref[k_rows, :], v_ref[k_rows, :], b))
        for r, (m, l, acc) in zip(rows, res):
            m_ref[r, :] = m
            l_ref[r, :] = l
            acc_ref[r, :] = acc
        return carry

    def group4(i, carry):
        starts, res = [], []
        for u in range(blocks_4):
            blk = blocks_4 * i + u
            bias = b4_ref[jnp.where(blk == 0, 0, 1)]
            k_blk = jnp.maximum(blk - 1, 0)
            for r4 in range(4):
                qs = [pl.multiple_of((r4 + 4 * c) * sub + 32 * blk, 32) for c in range(4)]
                ks = [pl.multiple_of((r4 + 4 * c) * sub + 32 * k_blk, 32) for c in range(4)]
                starts.append(qs)
                res.append(next_tile(
                    gather(q_ref, qs, 32), gather(k_ref, ks, 64), gather(v_ref, ks, 64), bias,
                    gather(m_ref, qs, 32), gather(l_ref, qs, 32), gather(acc_ref, qs, 32)))
        for qs, (m, l, acc) in zip(starts, res):
            scatter(m_ref, qs, 32, m)
            scatter(l_ref, qs, 32, l)
            scatter(acc_ref, qs, 32, acc)
        return carry

    def group1(i, carry):
        res = []
        for u in range(blocks_1):
            blk = blocks_1 * i + u
            bias = b1_ref[jnp.where(blk == 0, 0, 1)]
            k_lo = jnp.maximum(16 * blk - 8, 0)
            qs = [pl.multiple_of(r * sub + 16 * blk, 16) for r in range(DIL_MAX)]
            ks = [pl.multiple_of(r * sub + k_lo, 8) for r in range(DIL_MAX)]
            _, l, acc = next_tile(
                gather(q_ref, qs, 16), gather(k_ref, ks, 24), gather(v_ref, ks, 24), bias,
                gather(m_ref, qs, 16), gather(l_ref, qs, 16), gather(acc_ref, qs, 16))
            res.append((qs, (acc / l).astype(o_ref.dtype)))
        for qs, out in res:
            scatter(o_ref, qs, 16, out)
        return carry

    if group == 16:
        loop(DIL_MAX // TILES_16, group16)
    elif group == 4:
        loop(sub // (32 * blocks_4), group4)
    else:
        loop(sub // (16 * blocks_1), group1)


def _dilated_attention(qkv, slopes, *, batch, seq, dh):
    n = qkv.shape[0]
    sub = seq // DIL_MAX
    b1, b4, b16_first, b16 = _attn_bias_tables(slopes, sub)
    hp = ATTN_HEADS_PER_STEP
    groups = ATTN_HEADS // hp
    kern = functools.partial(_attn_kernel, sub=sub, heads=hp)
    return pl.pallas_call(
        kern,
        grid=(batch, groups),
        in_specs=[
            pl.BlockSpec((seq, hp * dh), lambda b, h: (b, h)),
            pl.BlockSpec((seq, hp * dh), lambda b, h: (b, groups + h)),
            pl.BlockSpec((seq, hp * dh), lambda b, h: (b, 2 * groups + h)),
            pl.BlockSpec((hp,) + b1.shape[1:], lambda b, h: (h, 0, 0, 0)),
            pl.BlockSpec((hp,) + b4.shape[1:], lambda b, h: (h, 0, 0, 0)),
            pl.BlockSpec((hp,) + b16_first.shape[1:], lambda b, h: (h, 0, 0)),
            pl.BlockSpec((hp,) + b16.shape[1:], lambda b, h: (h, 0, 0)),
        ],
        out_specs=pl.BlockSpec((seq, hp * dh), lambda b, h: (b, h)),
        out_shape=jax.ShapeDtypeStruct((n, ATTN_HEADS * dh), BF16),
        scratch_shapes=[pltpu.VMEM((seq, LANES), F32), pltpu.VMEM((seq, LANES), F32),
                        pltpu.VMEM((seq, dh), F32)] * hp,
        compiler_params=_params("parallel", "parallel"),
        name="dilated_attn",
    )(qkv, qkv, qkv, b1, b4, b16_first, b16)


def kernel(x, norm_mix_g, norm_mlp_g, final_norm_g, hyb_w_in, conv_w, gla_w_gate2, gla_b_gate,
           gla_norm_g, hyb_w_out, attn_w_qkv, attn_w_o, mlp_w1, mlp_w2):
    batch, seq, d = x.shape
    n = batch * seq
    depth = norm_mix_g.shape[0]
    conv_ch = conv_w.shape[-1]
    dv = gla_norm_g.shape[-1]
    dk = gla_w_gate2.shape[-1] // GLA_HEADS
    dh = d // ATTN_HEADS
    assert depth == 2 and seq % (DIL_MAX * ATTN_STEPS) == 0 and seq % GLA_CHUNK == 0
    assert dk == LANES and dh == LANES and dv % LANES == 0

    row = lambda v: v.reshape(1, -1).astype(F32)
    xf = x.reshape(n, d)

    main_cols = 3 * conv_ch + 2 * GLA_HEADS * dk + 2 * GLA_HEADS * dv
    w_in = hyb_w_in[0]
    w_low = jnp.pad(w_in[:, main_cols:], ((0, 0), (0, LANES - GLA_GATE_RANK))).astype(BF16)
    w_gate = jnp.pad(gla_w_gate2[0], ((0, LANES - GLA_GATE_RANK), (0, 0))).astype(BF16)
    proj, log_a, (w1_first, w2_first, w_out, w_qkv, w_o) = _in_proj_gla(
        xf, row(norm_mix_g[0]), w_in.astype(BF16), w_low, w_gate, row(gla_b_gate[0]),
        cols=main_cols,
        cast_layers=((mlp_w1, 0), (mlp_w2, 0), (hyb_w_out, 0), (attn_w_qkv, 0), (attn_w_o, 0)))

    q_off = 3 * conv_ch
    k_off = q_off + GLA_HEADS * dk
    v_off = k_off + GLA_HEADS * dk
    r_off = v_off + GLA_HEADS * dv
    tri = jnp.asarray(np.tril(np.ones((GLA_CHUNK, GLA_CHUNK), np.float32)), BF16)
    o_gla = _gla(proj, log_a, tri, row(gla_norm_g[0]), batch=batch, seq=seq,
                 q_off=q_off, k_off=k_off, v_off=v_off, r_off=r_off, dk=dk, dv=dv)
    xf = _conv_out_proj_residual(proj, conv_w[0].astype(F32), o_gla, w_out, xf, seq=seq)
    xf, (w1_next, w2_next) = _mlp_residual(
        xf, row(norm_mlp_g[0]), w1_first, w2_first, row(final_norm_g),
        final_norm=False, cast_layers=((mlp_w1, 1), (mlp_w2, 1)))

    col_scale = jnp.concatenate([jnp.full((1, d), float(dh) ** -0.5 * LOG2_E, F32),
                                 jnp.ones((1, 2 * d), F32)], axis=1)
    qkv = _qkv_proj_residue_major(xf, row(norm_mix_g[1]), w_qkv, col_scale, batch=batch, seq=seq)
    slopes = jnp.exp2(-8.0 * jnp.arange(1, ATTN_HEADS + 1, dtype=F32) / ATTN_HEADS)
    attn = _dilated_attention(qkv, slopes, batch=batch, seq=seq, dh=dh)
    xf = _attn_out_proj_residual(attn, w_o, xf, batch=batch, seq=seq)
    out, _ = _mlp_residual(xf, row(norm_mlp_g[1]), w1_next, w2_next, row(final_norm_g),
                           final_norm=True)
    return out.reshape(batch, seq, d)
```

```python
import functools

import jax
import jax.numpy as jnp
import numpy as np
from jax import lax
from jax.experimental import pallas as pl
from jax.experimental.pallas import tpu as pltpu

F32 = jnp.float32
BF16 = jnp.bfloat16

NORM_EPS = 1e-6
CONV_K = 3
GLA_HEADS = 4
GLA_GATE_RANK = 16
GLA_GATE_TEMP = 16.0
ATTN_HEADS = 16
ATTN_HEADS_PER_STEP = 2
DILATED_GROUPS = ((128, 1), (512, 4), (2048, 16))
DIL_MAX = 16
ATTN_STEPS = 128

LANES = 128
F32_SUBLANES = 8
BF16_SUBLANES = 16
GLA_CHUNK = 256
GLA_SUB = 16
GLA_BLOCK = 64
GLA_MAX_BLOCK_DECAY = 60.0
MASK_VALUE = -1e30
LOG2_E = 1.4426950408889634
TILES_16 = 16
BLOCKS_4 = 8
BLOCKS_1 = 16
V7X_VMEM_BYTES = 64 * 1024 * 1024
VMEM_LIMIT_BYTES = 56 * 1024 * 1024


def _params(*semantics, vmem_limit_bytes=VMEM_LIMIT_BYTES):
    return pltpu.CompilerParams(dimension_semantics=semantics, vmem_limit_bytes=vmem_limit_bytes)


def _nbytes(shape, dtype):
    return int(np.prod(shape)) * jnp.dtype(dtype).itemsize


def _dot(a, b):
    return jnp.dot(a, b, preferred_element_type=F32)


def _dot_nt(a, b):
    return lax.dot_general(a, b, (((1,), (1,)), ((), ())), preferred_element_type=F32)


def _dot_tn(a, b):
    return lax.dot_general(a, b, (((0,), (0,)), ((), ())), preferred_element_type=F32)


def _rmsnorm(x, g):
    return x * lax.rsqrt(jnp.mean(x * x, axis=-1, keepdims=True) + NORM_EPS) * g


def _tile_of_trip(i):
    return jnp.maximum(i - 1, 0)


def _chunk_index(i, j, n_tiles, chunks):
    return jnp.minimum(i, n_tiles - 1) * chunks + j


def _in_proj_gla_kernel(xc_ref, g_ref, w_ref, wlow_ref, wgate_ref, bgate_ref, *rest, n_cast):
    cast_in, (o_ref, loga_ref) = rest[:n_cast], rest[n_cast:n_cast + 2]
    cast_out, xn_ref = rest[n_cast + 2:2 * n_cast + 2], rest[2 * n_cast + 2]
    i, j = pl.program_id(0), pl.program_id(1)
    ch = xc_ref.shape[0]
    fill = i % 2
    rows = pl.ds(pl.multiple_of(j * ch, ch), ch)

    def norm_and_gate_chunk():
        xn = _rmsnorm(xc_ref[...], g_ref[...]).astype(BF16)
        xn_ref[fill, rows, :] = xn
        g_low = _dot(xn, wlow_ref[...]).astype(BF16)
        gate = _dot(g_low, wgate_ref[...]) + bgate_ref[...]
        log_sig = jnp.minimum(gate, 0.0) - jnp.log(1.0 + jnp.exp(-jnp.abs(gate)))
        loga_ref[rows, :] = log_sig * (1.0 / GLA_GATE_TEMP)
        for src, dst in zip(cast_in, cast_out):
            dst[...] = src[...].astype(dst.dtype)

    @pl.when(i == 0)
    def _():
        norm_and_gate_chunk()

    @pl.when(i > 0)
    def _():
        norm_and_gate_chunk()
        o_ref[...] = _dot(xn_ref[1 - fill], w_ref[...]).astype(o_ref.dtype)


def _in_proj_gla(x, gain, w, w_low, w_gate, b_gate, *, cols, cast_layers=(), tm=1024, tn=1536):
    n, d = x.shape
    n_gate = w_gate.shape[1]
    n_tiles, steps = n // tm, cols // tn
    ch = tm // steps
    cast_in_specs, cast_out_specs, cast_shapes = [], [], []
    for arr, layer in cast_layers:
        rows, width = arr.shape[1:]
        slab = rows // (n_tiles * steps)
        assert rows % (n_tiles * steps) == 0 and slab % BF16_SUBLANES == 0
        last = n_tiles * steps - 1
        cast_in_specs.append(pl.BlockSpec(
            (None, slab, width),
            lambda i, j, layer=layer, last=last: (layer, jnp.minimum(i * steps + j, last), 0)))
        cast_out_specs.append(pl.BlockSpec(
            (slab, width), lambda i, j, last=last: (jnp.minimum(i * steps + j, last), 0)))
        cast_shapes.append(jax.ShapeDtypeStruct((rows, width), BF16))
    kern = functools.partial(_in_proj_gla_kernel, n_cast=len(cast_layers))
    out = pl.pallas_call(
        kern,
        grid=(n_tiles + 1, steps),
        in_specs=[
            pl.BlockSpec((ch, d), lambda i, j: (_chunk_index(i, j, n_tiles, steps), 0)),
            pl.BlockSpec((1, d), lambda i, j: (0, 0)),
            pl.BlockSpec((d, tn), lambda i, j: (0, j)),
            pl.BlockSpec((d, LANES), lambda i, j: (0, 0)),
            pl.BlockSpec((LANES, n_gate), lambda i, j: (0, 0)),
            pl.BlockSpec((1, n_gate), lambda i, j: (0, 0)),
        ] + cast_in_specs,
        out_specs=[
            pl.BlockSpec((tm, tn), lambda i, j: (_tile_of_trip(i), j * jnp.minimum(i, 1))),
            pl.BlockSpec((tm, n_gate), lambda i, j: (jnp.minimum(i, n_tiles - 1), 0)),
        ] + cast_out_specs,
        out_shape=[
            jax.ShapeDtypeStruct((n, cols), BF16),
            jax.ShapeDtypeStruct((n, n_gate), F32),
        ] + cast_shapes,
        scratch_shapes=[pltpu.VMEM((2, tm, d), BF16)],
        compiler_params=_params("arbitrary", "arbitrary"),
        name="in_proj_gla",
    )(x, gain, w, w_low, w_gate, b_gate, *[arr for arr, _ in cast_layers])
    return out[0], out[1], out[2:]


def _qkv_proj_kernel(xc_ref, g_ref, w_ref, scale_ref, o_ref, xn_ref):
    i, j = pl.program_id(0), pl.program_id(1)
    ch, d = xc_ref.shape
    per = ch // DIL_MAX
    fill = i % 2

    def norm_chunk():
        xn = _rmsnorm(xc_ref[...], g_ref[...]).astype(BF16)
        xn_ref[fill, :, pl.ds(pl.multiple_of(j * per, per), per), :] = (
            jnp.swapaxes(xn.reshape(per, DIL_MAX, d), 0, 1))

    @pl.when(i == 0)
    def _():
        norm_chunk()

    @pl.when(i > 0)
    def _():
        norm_chunk()
        xn = xn_ref[1 - fill]
        acc = _dot(xn.reshape(xn.shape[0] * xn.shape[1], d), w_ref[...]) * scale_ref[...]
        o_ref[...] = acc.astype(o_ref.dtype).reshape(o_ref.shape)


def _qkv_proj_residue_major(x, gain, w, col_scale, *, batch, seq, tm=1024, tn=1536):
    n, d = x.shape
    cols = w.shape[1]
    sub = seq // DIL_MAX
    per = tm // DIL_MAX
    tiles_per_seq = seq // tm
    n_tiles, steps = n // tm, cols // tn
    ch = tm // steps
    assert ch % (DIL_MAX * BF16_SUBLANES) == 0

    def out_index(i, j):
        t = _tile_of_trip(i)
        return (t // tiles_per_seq, 0, t % tiles_per_seq, j * jnp.minimum(i, 1))

    out = pl.pallas_call(
        _qkv_proj_kernel,
        grid=(n_tiles + 1, steps),
        in_specs=[
            pl.BlockSpec((ch, d), lambda i, j: (_chunk_index(i, j, n_tiles, steps), 0)),
            pl.BlockSpec((1, d), lambda i, j: (0, 0)),
            pl.BlockSpec((d, tn), lambda i, j: (0, j)),
            pl.BlockSpec((1, tn), lambda i, j: (0, j)),
        ],
        out_specs=pl.BlockSpec((None, DIL_MAX, per, tn), out_index),
        out_shape=jax.ShapeDtypeStruct((batch, DIL_MAX, sub, cols), BF16),
        scratch_shapes=[pltpu.VMEM((2, DIL_MAX, per, d), BF16)],
        compiler_params=_params("arbitrary", "arbitrary"),
        name="qkv_proj",
    )(x, gain, w, col_scale)
    return out.reshape(n, cols)


def _conv_out_proj_kernel(ab_ref, ac_ref, ax_ref, cw_ref, gla_ref, wa_ref, wb_ref, x_ref, o_ref,
                          ubuf_ref, *, tiles_per_seq):
    tb = ab_ref.shape[0]
    pad = F32_SUBLANES

    @pl.when(pl.program_id(0) % tiles_per_seq == 0)
    def _():
        ubuf_ref[0:pad, :] = jnp.zeros((pad, ubuf_ref.shape[1]), F32)

    acc = x_ref[...] + _dot(gla_ref[...], wb_ref[...])
    u = ac_ref[...].astype(F32) * ax_ref[...].astype(F32)
    ubuf_ref[pad:pad + tb, :] = u
    u1 = ubuf_ref[pad - 1:pad - 1 + tb, :]
    u2 = ubuf_ref[pad - 2:pad - 2 + tb, :]
    cw = cw_ref[...]
    y = ab_ref[...].astype(F32) * (cw[0:1, :] * u2 + cw[1:2, :] * u1 + cw[2:3, :] * u)
    o_ref[...] = acc + _dot(y.astype(BF16), wa_ref[...])
    ubuf_ref[0:pad, :] = ubuf_ref[tb:tb + pad, :]


def _conv_out_proj_residual(proj, conv_w, o_gla, w_out, x, *, seq, tm=512):
    n, d = x.shape
    ch = conv_w.shape[1]
    assert w_out.shape[0] == 2 * ch and o_gla.shape[1] == ch
    kern = functools.partial(_conv_out_proj_kernel, tiles_per_seq=seq // tm)
    return pl.pallas_call(
        kern,
        grid=(n // tm,),
        in_specs=[
            pl.BlockSpec((tm, ch), lambda i: (i, 0)),
            pl.BlockSpec((tm, ch), lambda i: (i, 1)),
            pl.BlockSpec((tm, ch), lambda i: (i, 2)),
            pl.BlockSpec((CONV_K, ch), lambda i: (0, 0)),
            pl.BlockSpec((tm, o_gla.shape[1]), lambda i: (i, 0)),
            pl.BlockSpec((ch, d), lambda i: (0, 0)),
            pl.BlockSpec((ch, d), lambda i: (1, 0)),
            pl.BlockSpec((tm, d), lambda i: (i, 0)),
        ],
        out_specs=pl.BlockSpec((tm, d), lambda i: (i, 0)),
        out_shape=jax.ShapeDtypeStruct((n, d), F32),
        scratch_shapes=[pltpu.VMEM((tm + F32_SUBLANES, ch), F32)],
        compiler_params=_params("arbitrary"),
        name="conv_out_proj",
    )(proj, proj, proj, conv_w, o_gla, w_out, w_out, x)


def _gla_scores_factored(q, k, g):
    c_len = q.shape[0]
    score_rows = []
    for a in range(c_len // GLA_BLOCK):
        lo, hi = a * GLA_BLOCK, (a + 1) * GLA_BLOCK
        n_keys = LANES * -(-hi // LANES)
        g_first = g[lo:lo + 1, :]
        q_a = (q[lo:hi] * jnp.exp(g[lo:hi] - g_first)).astype(BF16)
        k_a = (k[:n_keys] * jnp.exp(jnp.minimum(g_first - g[:n_keys], GLA_MAX_BLOCK_DECAY)))
        s_a = _dot_nt(q_a, k_a.astype(BF16))
        row = lax.broadcasted_iota(jnp.int32, s_a.shape, 0) + lo
        col = lax.broadcasted_iota(jnp.int32, s_a.shape, 1)
        s_a = jnp.where(col <= row, s_a, 0.0)
        if n_keys < c_len:
            s_a = jnp.concatenate([s_a, jnp.zeros((GLA_BLOCK, c_len - n_keys), F32)], axis=1)
        score_rows.append(s_a)
    return jnp.concatenate(score_rows, axis=0)


def _gla_scores_pairwise(q, k, g):
    c_len = q.shape[0]
    n_sub = c_len // GLA_SUB
    score_rows = [jnp.zeros((GLA_SUB, c_len), F32)]
    for a in range(1, n_sub):
        lo = a * GLA_SUB
        n_keys = LANES * -(-lo // LANES)
        g_ref_row = g[lo:lo + 1, :]
        q_a = (q[lo:lo + GLA_SUB] * jnp.exp(g[lo:lo + GLA_SUB] - g_ref_row)).astype(BF16)
        k_a = (k[:n_keys] * jnp.exp(jnp.minimum(g_ref_row - g[:n_keys], 0.0))).astype(BF16)
        s_a = _dot_nt(q_a, k_a)
        col = lax.broadcasted_iota(jnp.int32, s_a.shape, 1)
        s_a = jnp.where(col < lo, s_a, 0.0)
        if n_keys < c_len:
            s_a = jnp.concatenate([s_a, jnp.zeros((GLA_SUB, c_len - n_keys), F32)], axis=1)
        score_rows.append(s_a)
    scores = jnp.concatenate(score_rows, axis=0)

    row = lax.broadcasted_iota(jnp.int32, (c_len, LANES), 0)
    lane = lax.broadcasted_iota(jnp.int32, (c_len, LANES), 1)
    delta = row % LANES - lane
    band_id = jnp.where((delta >= 0) & (delta <= row % GLA_SUB), delta, -1)
    band = jnp.zeros((c_len, LANES), F32)
    for d in range(GLA_SUB):
        k_d = k if d == 0 else pltpu.roll(k, d, 0)
        g_d = g if d == 0 else pltpu.roll(g, d, 0)
        e = jnp.exp(jnp.minimum(g - g_d, 0.0))
        diag = jnp.sum(q * k_d * e, axis=-1, keepdims=True)
        band = jnp.where(band_id == d, diag, band)
    zeros = jnp.zeros((LANES, LANES), F32)
    band_rows = []
    for t in range(c_len // LANES):
        tiles = [zeros] * (c_len // LANES)
        tiles[t] = band[t * LANES:(t + 1) * LANES]
        band_rows.append(jnp.concatenate(tiles, axis=1))
    return scores + jnp.concatenate(band_rows, axis=0)


def _gla_kernel(q_ref, k_ref, v_ref, r_ref, la_ref, tri_ref, gn_ref, o_ref, st_ref, g_ref,
                inter_ref, *, q_scale, dk, dv):
    c_len = q_ref.shape[0]

    @pl.when(pl.program_id(1) == 0)
    def _():
        st_ref[...] = jnp.zeros_like(st_ref)

    la = la_ref[...]
    tri = tri_ref[...]
    la_hi = la.astype(BF16)
    rem = la - la_hi.astype(F32)
    la_mid = rem.astype(BF16)
    la_lo = (rem - la_mid.astype(F32)).astype(BF16)
    g_all = _dot(tri, la_hi) + _dot(tri, la_mid) + _dot(tri, la_lo)
    g_ref[...] = g_all
    block_decay = [g_all[lo:lo + 1, :] - g_all[lo + GLA_BLOCK - 1:lo + GLA_BLOCK, :]
                   for lo in range(0, c_len, GLA_BLOCK)]
    worst_decay = jnp.max(jnp.concatenate(block_decay, axis=0))

    def head_operands(h):
        g = g_ref[:, h * dk:(h + 1) * dk]
        q = q_ref[:, h * dk:(h + 1) * dk].astype(F32) * q_scale
        k = k_ref[:, h * dk:(h + 1) * dk].astype(F32)
        return g, q, k, v_ref[:, h * dv:(h + 1) * dv]

    def finish_head(h, o):
        o = _rmsnorm(o, gn_ref[...])
        r = r_ref[:, h * dv:(h + 1) * dv].astype(F32)
        o_ref[:, h * dv:(h + 1) * dv] = (o * (r / (1.0 + jnp.exp(-r)))).astype(o_ref.dtype)

    for h in range(GLA_HEADS):
        g, q, k, v = head_operands(h)
        g_last = g[c_len - 1:c_len, :]
        st = st_ref[h]
        o_inter = _dot_nt((q * jnp.exp(g)).astype(BF16), st.astype(BF16))
        inter_ref[h] = o_inter
        k_dec = (k * jnp.exp(g_last - g)).astype(BF16)
        st_ref[h] = st * jnp.exp(g_last) + _dot_tn(v, k_dec)
        finish_head(h, o_inter + _dot(_gla_scores_factored(q, k, g).astype(BF16), v))

    @pl.when(worst_decay > GLA_MAX_BLOCK_DECAY)
    def _():
        for h in range(GLA_HEADS):
            g, q, k, v = head_operands(h)
            finish_head(h, inter_ref[h] + _dot(_gla_scores_pairwise(q, k, g).astype(BF16), v))


def _gla(proj, log_a, tri, norm_g, *, batch, seq, q_off, k_off, v_off, r_off, dk, dv):
    n = proj.shape[0]
    cpb = seq // GLA_CHUNK
    qk_w, vr_w = GLA_HEADS * dk, GLA_HEADS * dv
    row = lambda b, c: b * cpb + c
    kern = functools.partial(_gla_kernel, q_scale=float(dk) ** -0.5, dk=dk, dv=dv)
    return pl.pallas_call(
        kern,
        grid=(batch, cpb),
        in_specs=[
            pl.BlockSpec((GLA_CHUNK, qk_w), lambda b, c: (row(b, c), q_off // qk_w)),
            pl.BlockSpec((GLA_CHUNK, qk_w), lambda b, c: (row(b, c), k_off // qk_w)),
            pl.BlockSpec((GLA_CHUNK, vr_w), lambda b, c: (row(b, c), v_off // vr_w)),
            pl.BlockSpec((GLA_CHUNK, vr_w), lambda b, c: (row(b, c), r_off // vr_w)),
            pl.BlockSpec((GLA_CHUNK, qk_w), lambda b, c: (row(b, c), 0)),
            pl.BlockSpec((GLA_CHUNK, GLA_CHUNK), lambda b, c: (0, 0)),
            pl.BlockSpec((1, dv), lambda b, c: (0, 0)),
        ],
        out_specs=pl.BlockSpec((GLA_CHUNK, vr_w), lambda b, c: (row(b, c), 0)),
        out_shape=jax.ShapeDtypeStruct((n, vr_w), BF16),
        scratch_shapes=[pltpu.VMEM((GLA_HEADS, dv, dk), F32), pltpu.VMEM((GLA_CHUNK, qk_w), F32),
                        pltpu.VMEM((GLA_HEADS, GLA_CHUNK, dv), F32)],
        compiler_params=_params("parallel", "arbitrary"),
        name="gla",
    )(proj, proj, proj, proj, log_a, tri, norm_g)


def _attn_out_proj_kernel(a_ref, w_ref, x_ref, o_ref):
    tm, d = x_ref.shape
    acc = _dot(a_ref[...].reshape(tm, a_ref.shape[-1]), w_ref[...])
    acc = jnp.swapaxes(acc.reshape(DIL_MAX, tm // DIL_MAX, d), 0, 1).reshape(tm, d)
    o_ref[...] = x_ref[...] + acc


def _attn_out_proj_residual(attn, w, x, *, batch, seq, tm=512):
    n, d = x.shape
    per = tm // DIL_MAX
    tps = seq // tm
    attn = attn.reshape(batch, DIL_MAX, seq // DIL_MAX, attn.shape[1])
    return pl.pallas_call(
        _attn_out_proj_kernel,
        grid=(n // tm,),
        in_specs=[
            pl.BlockSpec((None, DIL_MAX, per, attn.shape[-1]), lambda i: (i // tps, 0, i % tps, 0)),
            pl.BlockSpec(w.shape, lambda i: (0, 0)),
            pl.BlockSpec((tm, d), lambda i: (i, 0)),
        ],
        out_specs=pl.BlockSpec((tm, d), lambda i: (i, 0)),
        out_shape=jax.ShapeDtypeStruct((n, d), F32),
        compiler_params=_params("parallel"),
        name="attn_out_proj",
    )(attn, w, x)


def _mlp_kernel(x_ref, g_ref, w1_ref, w2_ref, gf_ref, *rest, final_norm, n_cast):
    cast_in, o_ref = rest[:n_cast], rest[n_cast]
    cast_out, xn_ref = rest[n_cast + 1:2 * n_cast + 1], rest[2 * n_cast + 1]
    f = pl.program_id(1)

    @pl.when(f == 0)
    def _():
        x = x_ref[...]
        xn_ref[...] = _rmsnorm(x, g_ref[...]).astype(BF16)
        o_ref[...] = x

    for src, dst in zip(cast_in, cast_out):
        dst[...] = src[...].astype(dst.dtype)

    h = _dot(xn_ref[...], w1_ref[...])
    h = jnp.square(jnp.maximum(h, 0.0)).astype(BF16)
    o_ref[...] += _dot(h, w2_ref[...])

    if final_norm:
        @pl.when(f == pl.num_programs(1) - 1)
        def _():
            o_ref[...] = _rmsnorm(o_ref[...], gf_ref[...])


def _mlp_residual(x, gain, w1, w2, final_gain, *, final_norm, cast_layers=(), tm=512, tf=2048):
    n, d = x.shape
    d_ff = w1.shape[1]
    steps = (n // tm) * (d_ff // tf)
    kern = functools.partial(_mlp_kernel, final_norm=final_norm, n_cast=len(cast_layers))
    cast_in_specs, cast_out_specs, cast_shapes = [], [], []
    for arr, layer in cast_layers:
        rows, cols = arr.shape[1:]
        slab = rows // steps
        assert rows % steps == 0 and slab % BF16_SUBLANES == 0
        cast_in_specs.append(pl.BlockSpec(
            (None, slab, cols), lambda i, f, layer=layer: (layer, i * (d_ff // tf) + f, 0)))
        cast_out_specs.append(pl.BlockSpec((slab, cols), lambda i, f: (i * (d_ff // tf) + f, 0)))
        cast_shapes.append(jax.ShapeDtypeStruct((rows, cols), BF16))
    vmem = (2 * (2 * _nbytes((tm, d), F32) + 2 * _nbytes((d, tf), BF16)) + _nbytes((tm, d), BF16)
            + _nbytes((tm, tf), F32) + _nbytes((tm, tf), BF16))
    vmem += sum(2 * (_nbytes(s.block_shape[1:], F32) + _nbytes(s.block_shape[1:], BF16))
                for s in cast_in_specs)
    vmem_limit = min(vmem + (4 << 20), V7X_VMEM_BYTES)
    out = pl.pallas_call(
        kern,
        grid=(n // tm, d_ff // tf),
        in_specs=[
            pl.BlockSpec((tm, d), lambda i, f: (i, 0)),
            pl.BlockSpec((1, d), lambda i, f: (0, 0)),
            pl.BlockSpec((d, tf), lambda i, f: (0, f)),
            pl.BlockSpec((tf, d), lambda i, f: (f, 0)),
            pl.BlockSpec((1, d), lambda i, f: (0, 0)),
        ] + cast_in_specs,
        out_specs=[pl.BlockSpec((tm, d), lambda i, f: (i, 0))] + cast_out_specs,
        out_shape=[jax.ShapeDtypeStruct((n, d), F32)] + cast_shapes,
        scratch_shapes=[pltpu.VMEM((tm, d), BF16)],
        compiler_params=_params("arbitrary", "arbitrary", vmem_limit_bytes=vmem_limit),
        name="mlp",
    )(x, gain, w1, w2, final_gain, *[arr for arr, _ in cast_layers])
    return out[0], out[1:]


def _attn_bias_tables():
    slopes = np.exp2(-8.0 * np.arange(1, ATTN_HEADS + 1) / ATTN_HEADS)

    def table(q_pos, k_pos, dilation):
        diff = q_pos[:, None] - k_pos[None, :]
        valid = (diff >= 0) & (diff <= ATTN_STEPS)
        bias = -(slopes * LOG2_E)[:, None, None] * (dilation * diff)[None]
        return np.where(valid[None], bias, MASK_VALUE).astype(np.float32)

    q = np.arange(128)
    b16_first = table(q, q, 16)
    b16 = table(q + 128, np.arange(256), 16)

    q4 = 4 * (q % 32) + q // 32
    k = np.arange(256)
    k4 = 4 * (k % 64) + k // 64
    b4 = np.stack([table(q4, k4, 4), table(q4 + 128, k4, 4)], axis=1)

    q = np.arange(256)
    q1 = 16 * (q % 16) + q // 16
    k = np.arange(384)
    k1 = 16 * (k % 24) + k // 24
    b1 = np.stack([table(q1, k1, 1), table(q1 + 128, k1, 1)], axis=1)
    return tuple(jnp.asarray(t) for t in (b1, b4, b16_first, b16))


def _attn_kernel(q_all, k_all, v_all, b1_all, b4_all, b16f_all, b16_all, o_all, *scratch, sub,
                 heads):
    def gather(ref, starts, size):
        return jnp.concatenate([ref[pl.ds(s, size), :] for s in starts], axis=0)

    def scatter(ref, starts, size, val):
        for c, s in enumerate(starts):
            ref[pl.ds(s, size), :] = val[c * size:(c + 1) * size]

    def lanes(x):
        return jnp.broadcast_to(x, (x.shape[0], LANES))

    def across_keys(x, n_keys):
        return jnp.concatenate([x] * (n_keys // LANES), axis=1)

    def weighted_values_and_sum(p, v):
        pv = _dot(p.astype(BF16), jnp.concatenate([v, jnp.ones_like(v)], axis=1))
        return pv[:, :LANES], pv[:, LANES:]

    def first_tile(q, k, v, bias):
        s = _dot_nt(q, k) + bias
        m = jnp.max(s, axis=-1, keepdims=True)
        acc, l = weighted_values_and_sum(jnp.exp2(s - m), v)
        return lanes(m), l, acc

    def next_tile(q, k, v, bias, m_old, l_old, acc_old):
        s = _dot_nt(q, k) + bias
        m_new = jnp.maximum(m_old, lanes(jnp.max(s, axis=-1, keepdims=True)))
        alpha = jnp.exp2(m_old - m_new)
        acc, l = weighted_values_and_sum(jnp.exp2(s - across_keys(m_new, s.shape[1])), v)
        return m_new, alpha * l_old + l, alpha * acc_old + acc

    dh = q_all.shape[1] // heads

    def loop(trips, body):
        if trips == 1:
            body(0, 0)
        else:
            lax.fori_loop(0, trips, body, 0)

    blocks_4 = min(BLOCKS_4, sub // 32)
    blocks_1 = min(BLOCKS_1, sub // 16)
    for group in (16, 4, 1):
        for h in range(heads):
            cols = slice(h * dh, (h + 1) * dh)
            head_refs = (q_all.at[:, cols], k_all.at[:, cols], v_all.at[:, cols], o_all.at[:, cols],
                         b1_all.at[h], b4_all.at[h], b16f_all.at[h], b16_all.at[h],
                         *scratch[3 * h:3 * h + 3])
            _attn_group(group, head_refs, sub, blocks_4, blocks_1, loop, first_tile, next_tile,
                        gather, scatter)


def _attn_group(group, head_refs, sub, blocks_4, blocks_1, loop, first_tile, next_tile, gather,
                scatter):
    (q_ref, k_ref, v_ref, o_ref, b1_ref, b4_ref, b16f_ref, b16_ref, m_ref, l_ref, acc_ref) = head_refs

    def group16(i, carry):
        bias_first, bias = b16f_ref[...], b16_ref[...]
        rows, res = [], []
        for u in range(TILES_16):
            base = (TILES_16 * i + u) * sub
            for j in range(sub // ATTN_STEPS):
                q_rows = pl.ds(pl.multiple_of(base + j * ATTN_STEPS, ATTN_STEPS), ATTN_STEPS)
                if j == 0:
                    k_rows, b = q_rows, bias_first
                else:
                    k_rows = pl.ds(pl.multiple_of(base + (j - 1) * ATTN_STEPS, ATTN_STEPS),
                                   2 * ATTN_STEPS)
                    b = bias
                rows.append(q_rows)
                res.append(first_tile(q_ref[q_rows, :], k_ref[k_rows, :], v_ref[k_rows, :], b))
        for r, (m, l, acc) in zip(rows, res):
            m_ref[r, :] = m
            l_ref[r, :] = l
            acc_ref[r, :] = acc
        return carry

    def group4(i, carry):
        starts, res = [], []
        for u in range(blocks_4):
            blk = blocks_4 * i + u
            bias = b4_ref[jnp.where(blk == 0, 0, 1)]
            k_blk = jnp.maximum(blk - 1, 0)
            for r4 in range(4):
                qs = [pl.multiple_of((r4 + 4 * c) * sub + 32 * blk, 32) for c in range(4)]
                ks = [pl.multiple_of((r4 + 4 * c) * sub + 32 * k_blk, 32) for c in range(4)]
                starts.append(qs)
                res.append(next_tile(
                    gather(q_ref, qs, 32), gather(k_ref, ks, 64), gather(v_ref, ks, 64), bias,
                    gather(m_ref, qs, 32), gather(l_ref, qs, 32), gather(acc_ref, qs, 32)))
        for qs, (m, l, acc) in zip(starts, res):
            scatter(m_ref, qs, 32, m)
            scatter(l_ref, qs, 32, l)
            scatter(acc_ref, qs, 32, acc)
        return carry

    def group1(i, carry):
        res = []
        for u in range(blocks_1):
            blk = blocks_1 * i + u
            bias = b1_ref[jnp.where(blk == 0, 0, 1)]
            k_lo = jnp.maximum(16 * blk - 8, 0)
            qs = [pl.multiple_of(r * sub + 16 * blk, 16) for r in range(DIL_MAX)]
            ks = [pl.multiple_of(r * sub + k_lo, 8) for r in range(DIL_MAX)]
            _, l, acc = next_tile(
                gather(q_ref, qs, 16), gather(k_ref, ks, 24), gather(v_ref, ks, 24), bias,
                gather(m_ref, qs, 16), gather(l_ref, qs, 16), gather(acc_ref, qs, 16))
            res.append((qs, (acc / l).astype(o_ref.dtype)))
        for qs, out in res:
            scatter(o_ref, qs, 16, out)
        return carry

    if group == 16:
        loop(DIL_MAX // TILES_16, group16)
    elif group == 4:
        loop(sub // (32 * blocks_4), group4)
    else:
        loop(sub // (16 * blocks_1), group1)


def _dilated_attention(qkv, *, batch, seq, dh):
    n = qkv.shape[0]
    sub = seq // DIL_MAX
    b1, b4, b16_first, b16 = _attn_bias_tables()
    hp = ATTN_HEADS_PER_STEP
    groups = ATTN_HEADS // hp
    kern = functools.partial(_attn_kernel, sub=sub, heads=hp)
    return pl.pallas_call(
        kern,
        grid=(batch, groups),
        in_specs=[
            pl.BlockSpec((seq, hp * dh), lambda b, h: (b, h)),
            pl.BlockSpec((seq, hp * dh), lambda b, h: (b, groups + h)),
            pl.BlockSpec((seq, hp * dh), lambda b, h: (b, 2 * groups + h)),
            pl.BlockSpec((hp,) + b1.shape[1:], lambda b, h: (h, 0, 0, 0)),
            pl.BlockSpec((hp,) + b4.shape[1:], lambda b, h: (h, 0, 0, 0)),
            pl.BlockSpec((hp,) + b16_first.shape[1:], lambda b, h: (h, 0, 0)),
            pl.BlockSpec((hp,) + b16.shape[1:], lambda b, h: (h, 0, 0)),
        ],
        out_specs=pl.BlockSpec((seq, hp * dh), lambda b, h: (b, h)),
        out_shape=jax.ShapeDtypeStruct((n, ATTN_HEADS * dh), BF16),
        scratch_shapes=[pltpu.VMEM((seq, LANES), F32), pltpu.VMEM((seq, LANES), F32),
                        pltpu.VMEM((seq, dh), F32)] * hp,
        compiler_params=_params("parallel", "parallel"),
        name="dilated_attn",
    )(qkv, qkv, qkv, b1, b4, b16_first, b16)


def kernel(x, norm_mix_g, norm_mlp_g, final_norm_g, hyb_w_in, conv_w, gla_w_gate2, gla_b_gate,
           gla_norm_g, hyb_w_out, attn_w_qkv, attn_w_o, mlp_w1, mlp_w2):
    batch, seq, d = x.shape
    n = batch * seq
    depth = norm_mix_g.shape[0]
    conv_ch = conv_w.shape[-1]
    dv = gla_norm_g.shape[-1]
    dk = gla_w_gate2.shape[-1] // GLA_HEADS
    dh = d // ATTN_HEADS
    assert depth == 2 and seq % (DIL_MAX * ATTN_STEPS) == 0 and seq % GLA_CHUNK == 0
    assert dk == LANES and dh == LANES and dv % LANES == 0

    row = lambda v: v.reshape(1, -1).astype(F32)
    xf = x.reshape(n, d)

    main_cols = 3 * conv_ch + 2 * GLA_HEADS * dk + 2 * GLA_HEADS * dv
    w_in = hyb_w_in[0]
    w_low = jnp.pad(w_in[:, main_cols:], ((0, 0), (0, LANES - GLA_GATE_RANK))).astype(BF16)
    w_gate = jnp.pad(gla_w_gate2[0], ((0, LANES - GLA_GATE_RANK), (0, 0))).astype(BF16)
    proj, log_a, (w1_first, w2_first, w_out, w_qkv, w_o) = _in_proj_gla(
        xf, row(norm_mix_g[0]), w_in.astype(BF16), w_low, w_gate, row(gla_b_gate[0]),
        cols=main_cols,
        cast_layers=((mlp_w1, 0), (mlp_w2, 0), (hyb_w_out, 0), (attn_w_qkv, 0), (attn_w_o, 0)))

    q_off = 3 * conv_ch
    k_off = q_off + GLA_HEADS * dk
    v_off = k_off + GLA_HEADS * dk
    r_off = v_off + GLA_HEADS * dv
    tri = jnp.asarray(np.tril(np.ones((GLA_CHUNK, GLA_CHUNK), np.float32)), BF16)
    o_gla = _gla(proj, log_a, tri, row(gla_norm_g[0]), batch=batch, seq=seq,
                 q_off=q_off, k_off=k_off, v_off=v_off, r_off=r_off, dk=dk, dv=dv)
    xf = _conv_out_proj_residual(proj, conv_w[0].astype(F32), o_gla, w_out, xf, seq=seq)
    xf, (w1_next, w2_next) = _mlp_residual(
        xf, row(norm_mlp_g[0]), w1_first, w2_first, row(final_norm_g),
        final_norm=False, cast_layers=((mlp_w1, 1), (mlp_w2, 1)))

    col_scale = jnp.concatenate([jnp.full((1, d), float(dh) ** -0.5 * LOG2_E, F32),
                                 jnp.ones((1, 2 * d), F32)], axis=1)
    qkv = _qkv_proj_residue_major(xf, row(norm_mix_g[1]), w_qkv, col_scale, batch=batch, seq=seq)
    attn = _dilated_attention(qkv, batch=batch, seq=seq, dh=dh)
    xf = _attn_out_proj_residual(attn, w_o, xf, batch=batch, seq=seq)
    out, _ = _mlp_residual(xf, row(norm_mlp_g[1]), w1_next, w2_next, row(final_norm_g),
                           final_norm=True)
    return out.reshape(batch, seq, d)
```

```python
import functools

import jax
import jax.numpy as jnp
import numpy as np
from jax import lax
from jax.experimental import pallas as pl
from jax.experimental.pallas import tpu as pltpu

F32 = jnp.float32
BF16 = jnp.bfloat16

NORM_EPS = 1e-6
CONV_K = 3
GLA_HEADS = 4
GLA_GATE_RANK = 16
GLA_GATE_TEMP = 16.0
ATTN_HEADS = 16
ATTN_HEADS_PER_STEP = 2
DILATED_GROUPS = ((128, 1), (512, 4), (2048, 16))
DIL_MAX = 16
ATTN_STEPS = 128

LANES = 128
F32_SUBLANES = 8
BF16_SUBLANES = 16
GLA_CHUNK = 256
GLA_SUB = 16
GLA_BLOCK = 64
GLA_MAX_BLOCK_DECAY = 60.0
MASK_VALUE = -1e30
LOG2_E = 1.4426950408889634
TILES_16 = 16
BLOCKS_4 = 8
BLOCKS_1 = 16
V7X_VMEM_BYTES = 64 * 1024 * 1024
VMEM_LIMIT_BYTES = 56 * 1024 * 1024


def _params(*semantics, vmem_limit_bytes=VMEM_LIMIT_BYTES):
    return pltpu.CompilerParams(dimension_semantics=semantics, vmem_limit_bytes=vmem_limit_bytes)


def _nbytes(shape, dtype):
    return int(np.prod(shape)) * jnp.dtype(dtype).itemsize


def _dot(a, b):
    return jnp.dot(a, b, preferred_element_type=F32)


def _dot_nt(a, b):
    return lax.dot_general(a, b, (((1,), (1,)), ((), ())), preferred_element_type=F32)


def _dot_tn(a, b):
    return lax.dot_general(a, b, (((0,), (0,)), ((), ())), preferred_element_type=F32)


def _rmsnorm(x, g):
    return x * lax.rsqrt(jnp.mean(x * x, axis=-1, keepdims=True) + NORM_EPS) * g


def _tile_of_trip(i):
    return jnp.maximum(i - 1, 0)


def _chunk_index(i, j, n_tiles, chunks):
    return jnp.minimum(i, n_tiles - 1) * chunks + j


def _in_proj_gla_kernel(xc_ref, g_ref, w_ref, wlow_ref, wgate_ref, bgate_ref, *rest, n_cast):
    cast_in, (o_ref, loga_ref) = rest[:n_cast], rest[n_cast:n_cast + 2]
    cast_out, xn_refs = rest[n_cast + 2:2 * n_cast + 2], rest[2 * n_cast + 2:]
    i, j = pl.program_id(0), pl.program_id(1)
    ch = xc_ref.shape[0]
    rows = pl.ds(pl.multiple_of(j * ch, ch), ch)

    def norm_and_gate_chunk(fill_ref):
        xn = _rmsnorm(xc_ref[...], g_ref[...]).astype(BF16)
        fill_ref[rows, :] = xn
        g_low = _dot(xn, wlow_ref[...]).astype(BF16)
        gate = _dot(g_low, wgate_ref[...]) + bgate_ref[...]
        log_sig = jnp.minimum(gate, 0.0) - jnp.log(1.0 + jnp.exp(-jnp.abs(gate)))
        loga_ref[rows, :] = log_sig * (1.0 / GLA_GATE_TEMP)
        for src, dst in zip(cast_in, cast_out):
            dst[...] = src[...].astype(dst.dtype)

    @pl.when(i == 0)
    def _():
        norm_and_gate_chunk(xn_refs[0])

    for parity in range(2):
        @pl.when((i > 0) & (i % 2 == parity))
        def _():
            norm_and_gate_chunk(xn_refs[parity])
            o_ref[...] = _dot(xn_refs[1 - parity][...], w_ref[...]).astype(o_ref.dtype)


def _in_proj_gla(x, gain, w, w_low, w_gate, b_gate, *, cols, cast_layers=(), tm=1024, tn=1536):
    n, d = x.shape
    n_gate = w_gate.shape[1]
    n_tiles, steps = n // tm, cols // tn
    ch = tm // steps
    cast_in_specs, cast_out_specs, cast_shapes = [], [], []
    for arr, layer in cast_layers:
        rows, width = arr.shape[1:]
        slab = rows // (n_tiles * steps)
        assert rows % (n_tiles * steps) == 0 and slab % BF16_SUBLANES == 0
        last = n_tiles * steps - 1
        cast_in_specs.append(pl.BlockSpec(
            (None, slab, width),
            lambda i, j, layer=layer, last=last: (layer, jnp.minimum(i * steps + j, last), 0)))
        cast_out_specs.append(pl.BlockSpec(
            (slab, width), lambda i, j, last=last: (jnp.minimum(i * steps + j, last), 0)))
        cast_shapes.append(jax.ShapeDtypeStruct((rows, width), BF16))
    kern = functools.partial(_in_proj_gla_kernel, n_cast=len(cast_layers))
    out = pl.pallas_call(
        kern,
        grid=(n_tiles + 1, steps),
        in_specs=[
            pl.BlockSpec((ch, d), lambda i, j: (_chunk_index(i, j, n_tiles, steps), 0)),
            pl.BlockSpec((1, d), lambda i, j: (0, 0)),
            pl.BlockSpec((d, tn), lambda i, j: (0, j)),
            pl.BlockSpec((d, LANES), lambda i, j: (0, 0)),
            pl.BlockSpec((LANES, n_gate), lambda i, j: (0, 0)),
            pl.BlockSpec((1, n_gate), lambda i, j: (0, 0)),
        ] + cast_in_specs,
        out_specs=[
            pl.BlockSpec((tm, tn), lambda i, j: (_tile_of_trip(i), j * jnp.minimum(i, 1))),
            pl.BlockSpec((tm, n_gate), lambda i, j: (jnp.minimum(i, n_tiles - 1), 0)),
        ] + cast_out_specs,
        out_shape=[
            jax.ShapeDtypeStruct((n, cols), BF16),
            jax.ShapeDtypeStruct((n, n_gate), F32),
        ] + cast_shapes,
        scratch_shapes=[pltpu.VMEM((tm, d), BF16), pltpu.VMEM((tm, d), BF16)],
        compiler_params=_params("arbitrary", "arbitrary"),
        name="in_proj_gla",
    )(x, gain, w, w_low, w_gate, b_gate, *[arr for arr, _ in cast_layers])
    return out[0], out[1], out[2:]


def _qkv_proj_kernel(xc_ref, g_ref, w_ref, scale_ref, o_ref, *xn_refs):
    i, j = pl.program_id(0), pl.program_id(1)
    ch, d = xc_ref.shape
    per = ch // DIL_MAX

    def norm_chunk(fill_ref):
        xn = _rmsnorm(xc_ref[...], g_ref[...]).astype(BF16)
        fill_ref[:, pl.ds(pl.multiple_of(j * per, per), per), :] = (
            jnp.swapaxes(xn.reshape(per, DIL_MAX, d), 0, 1))

    @pl.when(i == 0)
    def _():
        norm_chunk(xn_refs[0])

    for parity in range(2):
        @pl.when((i > 0) & (i % 2 == parity))
        def _():
            norm_chunk(xn_refs[parity])
            xn = xn_refs[1 - parity][...]
            acc = _dot(xn.reshape(xn.shape[0] * xn.shape[1], d), w_ref[...]) * scale_ref[...]
            o_ref[...] = acc.astype(o_ref.dtype).reshape(o_ref.shape)


def _qkv_proj_residue_major(x, gain, w, col_scale, *, batch, seq, tm=1024, tn=1536):
    n, d = x.shape
    cols = w.shape[1]
    sub = seq // DIL_MAX
    per = tm // DIL_MAX
    tiles_per_seq = seq // tm
    n_tiles, steps = n // tm, cols // tn
    ch = tm // steps
    assert ch % (DIL_MAX * BF16_SUBLANES) == 0

    def out_index(i, j):
        t = _tile_of_trip(i)
        return (t // tiles_per_seq, 0, t % tiles_per_seq, j * jnp.minimum(i, 1))

    out = pl.pallas_call(
        _qkv_proj_kernel,
        grid=(n_tiles + 1, steps),
        in_specs=[
            pl.BlockSpec((ch, d), lambda i, j: (_chunk_index(i, j, n_tiles, steps), 0)),
            pl.BlockSpec((1, d), lambda i, j: (0, 0)),
            pl.BlockSpec((d, tn), lambda i, j: (0, j)),
            pl.BlockSpec((1, tn), lambda i, j: (0, j)),
        ],
        out_specs=pl.BlockSpec((None, DIL_MAX, per, tn), out_index),
        out_shape=jax.ShapeDtypeStruct((batch, DIL_MAX, sub, cols), BF16),
        scratch_shapes=[pltpu.VMEM((DIL_MAX, per, d), BF16), pltpu.VMEM((DIL_MAX, per, d), BF16)],
        compiler_params=_params("arbitrary", "arbitrary"),
        name="qkv_proj",
    )(x, gain, w, col_scale)
    return out.reshape(n, cols)


def _conv_out_proj_kernel(ab_ref, ac_ref, ax_ref, cw_ref, gla_ref, wa_ref, wb_ref, x_ref, o_ref,
                          ubuf_ref, *, tiles_per_seq):
    tb = ab_ref.shape[0]
    pad = F32_SUBLANES

    @pl.when(pl.program_id(0) % tiles_per_seq == 0)
    def _():
        ubuf_ref[0:pad, :] = jnp.zeros((pad, ubuf_ref.shape[1]), F32)

    acc = x_ref[...] + _dot(gla_ref[...], wb_ref[...])
    u = ac_ref[...].astype(F32) * ax_ref[...].astype(F32)
    ubuf_ref[pad:pad + tb, :] = u
    u1 = ubuf_ref[pad - 1:pad - 1 + tb, :]
    u2 = ubuf_ref[pad - 2:pad - 2 + tb, :]
    cw = cw_ref[...]
    y = ab_ref[...].astype(F32) * (cw[0:1, :] * u2 + cw[1:2, :] * u1 + cw[2:3, :] * u)
    o_ref[...] = acc + _dot(y.astype(BF16), wa_ref[...])
    ubuf_ref[0:pad, :] = ubuf_ref[tb:tb + pad, :]


def _conv_out_proj_residual(proj, conv_w, o_gla, w_out, x, *, seq, tm=512):
    n, d = x.shape
    ch = conv_w.shape[1]
    assert w_out.shape[0] == 2 * ch and o_gla.shape[1] == ch
    kern = functools.partial(_conv_out_proj_kernel, tiles_per_seq=seq // tm)
    return pl.pallas_call(
        kern,
        grid=(n // tm,),
        in_specs=[
            pl.BlockSpec((tm, ch), lambda i: (i, 0)),
            pl.BlockSpec((tm, ch), lambda i: (i, 1)),
            pl.BlockSpec((tm, ch), lambda i: (i, 2)),
            pl.BlockSpec((CONV_K, ch), lambda i: (0, 0)),
            pl.BlockSpec((tm, o_gla.shape[1]), lambda i: (i, 0)),
            pl.BlockSpec((ch, d), lambda i: (0, 0)),
            pl.BlockSpec((ch, d), lambda i: (1, 0)),
            pl.BlockSpec((tm, d), lambda i: (i, 0)),
        ],
        out_specs=pl.BlockSpec((tm, d), lambda i: (i, 0)),
        out_shape=jax.ShapeDtypeStruct((n, d), F32),
        scratch_shapes=[pltpu.VMEM((tm + F32_SUBLANES, ch), F32)],
        compiler_params=_params("arbitrary"),
        name="conv_out_proj",
    )(proj, proj, proj, conv_w, o_gla, w_out, w_out, x)


def _gla_scores_factored(q, k, g):
    c_len = q.shape[0]
    score_rows = []
    for a in range(c_len // GLA_BLOCK):
        lo, hi = a * GLA_BLOCK, (a + 1) * GLA_BLOCK
        n_keys = LANES * -(-hi // LANES)
        g_first = g[lo:lo + 1, :]
        q_a = (q[lo:hi] * jnp.exp(g[lo:hi] - g_first)).astype(BF16)
        k_a = (k[:n_keys] * jnp.exp(jnp.minimum(g_first - g[:n_keys], GLA_MAX_BLOCK_DECAY)))
        s_a = _dot_nt(q_a, k_a.astype(BF16))
        row = lax.broadcasted_iota(jnp.int32, s_a.shape, 0) + lo
        col = lax.broadcasted_iota(jnp.int32, s_a.shape, 1)
        s_a = jnp.where(col <= row, s_a, 0.0)
        if n_keys < c_len:
            s_a = jnp.concatenate([s_a, jnp.zeros((GLA_BLOCK, c_len - n_keys), F32)], axis=1)
        score_rows.append(s_a)
    return jnp.concatenate(score_rows, axis=0)


def _gla_scores_pairwise(q, k, g):
    c_len = q.shape[0]
    n_sub = c_len // GLA_SUB
    score_rows = [jnp.zeros((GLA_SUB, c_len), F32)]
    for a in range(1, n_sub):
        lo = a * GLA_SUB
        n_keys = LANES * -(-lo // LANES)
        g_ref_row = g[lo:lo + 1, :]
        q_a = (q[lo:lo + GLA_SUB] * jnp.exp(g[lo:lo + GLA_SUB] - g_ref_row)).astype(BF16)
        k_a = (k[:n_keys] * jnp.exp(jnp.minimum(g_ref_row - g[:n_keys], 0.0))).astype(BF16)
        s_a = _dot_nt(q_a, k_a)
        col = lax.broadcasted_iota(jnp.int32, s_a.shape, 1)
        s_a = jnp.where(col < lo, s_a, 0.0)
        if n_keys < c_len:
            s_a = jnp.concatenate([s_a, jnp.zeros((GLA_SUB, c_len - n_keys), F32)], axis=1)
        score_rows.append(s_a)
    scores = jnp.concatenate(score_rows, axis=0)

    row = lax.broadcasted_iota(jnp.int32, (c_len, LANES), 0)
    lane = lax.broadcasted_iota(jnp.int32, (c_len, LANES), 1)
    delta = row % LANES - lane
    band_id = jnp.where((delta >= 0) & (delta <= row % GLA_SUB), delta, -1)
    band = jnp.zeros((c_len, LANES), F32)
    for d in range(GLA_SUB):
        k_d = k if d == 0 else pltpu.roll(k, d, 0)
        g_d = g if d == 0 else pltpu.roll(g, d, 0)
        e = jnp.exp(jnp.minimum(g - g_d, 0.0))
        diag = jnp.sum(q * k_d * e, axis=-1, keepdims=True)
        band = jnp.where(band_id == d, diag, band)
    zeros = jnp.zeros((LANES, LANES), F32)
    band_rows = []
    for t in range(c_len // LANES):
        tiles = [zeros] * (c_len // LANES)
        tiles[t] = band[t * LANES:(t + 1) * LANES]
        band_rows.append(jnp.concatenate(tiles, axis=1))
    return scores + jnp.concatenate(band_rows, axis=0)


def _gla_kernel(q_ref, k_ref, v_ref, r_ref, la_ref, tri_ref, gn_ref, o_ref, st_ref, g_ref,
                inter_ref, *, q_scale, dk, dv):
    c_len = q_ref.shape[0]

    @pl.when(pl.program_id(1) == 0)
    def _():
        st_ref[...] = jnp.zeros_like(st_ref)

    la = la_ref[...]
    tri = tri_ref[...]
    la_hi = la.astype(BF16)
    rem = la - la_hi.astype(F32)
    la_mid = rem.astype(BF16)
    la_lo = (rem - la_mid.astype(F32)).astype(BF16)
    g_all = _dot(tri, la_hi) + _dot(tri, la_mid) + _dot(tri, la_lo)
    g_ref[...] = g_all
    block_decay = [g_all[lo:lo + 1, :] - g_all[lo + GLA_BLOCK - 1:lo + GLA_BLOCK, :]
                   for lo in range(0, c_len, GLA_BLOCK)]
    worst_decay = jnp.max(jnp.concatenate(block_decay, axis=0))

    def head_operands(h):
        g = g_ref[:, h * dk:(h + 1) * dk]
        q = q_ref[:, h * dk:(h + 1) * dk].astype(F32) * q_scale
        k = k_ref[:, h * dk:(h + 1) * dk].astype(F32)
        return g, q, k, v_ref[:, h * dv:(h + 1) * dv]

    def finish_head(h, o):
        o = _rmsnorm(o, gn_ref[...])
        r = r_ref[:, h * dv:(h + 1) * dv].astype(F32)
        o_ref[:, h * dv:(h + 1) * dv] = (o * (r / (1.0 + jnp.exp(-r)))).astype(o_ref.dtype)

    for h in range(GLA_HEADS):
        g, q, k, v = head_operands(h)
        g_last = g[c_len - 1:c_len, :]
        st = st_ref[h]
        o_inter = _dot_nt((q * jnp.exp(g)).astype(BF16), st.astype(BF16))
        inter_ref[h] = o_inter
        k_dec = (k * jnp.exp(g_last - g)).astype(BF16)
        st_ref[h] = st * jnp.exp(g_last) + _dot_tn(v, k_dec)
        finish_head(h, o_inter + _dot(_gla_scores_factored(q, k, g).astype(BF16), v))

    @pl.when(worst_decay > GLA_MAX_BLOCK_DECAY)
    def _():
        for h in range(GLA_HEADS):
            g, q, k, v = head_operands(h)
            finish_head(h, inter_ref[h] + _dot(_gla_scores_pairwise(q, k, g).astype(BF16), v))


def _gla(proj, log_a, tri, norm_g, *, batch, seq, q_off, k_off, v_off, r_off, dk, dv):
    n = proj.shape[0]
    cpb = seq // GLA_CHUNK
    qk_w, vr_w = GLA_HEADS * dk, GLA_HEADS * dv
    row = lambda b, c: b * cpb + c
    kern = functools.partial(_gla_kernel, q_scale=float(dk) ** -0.5, dk=dk, dv=dv)
    return pl.pallas_call(
        kern,
        grid=(batch, cpb),
        in_specs=[
            pl.BlockSpec((GLA_CHUNK, qk_w), lambda b, c: (row(b, c), q_off // qk_w)),
            pl.BlockSpec((GLA_CHUNK, qk_w), lambda b, c: (row(b, c), k_off // qk_w)),
            pl.BlockSpec((GLA_CHUNK, vr_w), lambda b, c: (row(b, c), v_off // vr_w)),
            pl.BlockSpec((GLA_CHUNK, vr_w), lambda b, c: (row(b, c), r_off // vr_w)),
            pl.BlockSpec((GLA_CHUNK, qk_w), lambda b, c: (row(b, c), 0)),
            pl.BlockSpec((GLA_CHUNK, GLA_CHUNK), lambda b, c: (0, 0)),
            pl.BlockSpec((1, dv), lambda b, c: (0, 0)),
        ],
        out_specs=pl.BlockSpec((GLA_CHUNK, vr_w), lambda b, c: (row(b, c), 0)),
        out_shape=jax.ShapeDtypeStruct((n, vr_w), BF16),
        scratch_shapes=[pltpu.VMEM((GLA_HEADS, dv, dk), F32), pltpu.VMEM((GLA_CHUNK, qk_w), F32),
                        pltpu.VMEM((GLA_HEADS, GLA_CHUNK, dv), F32)],
        compiler_params=_params("parallel", "arbitrary"),
        name="gla",
    )(proj, proj, proj, proj, log_a, tri, norm_g)


def _attn_out_proj_kernel(a_ref, w_ref, x_ref, o_ref):
    tm, d = x_ref.shape
    acc = _dot(a_ref[...].reshape(tm, a_ref.shape[-1]), w_ref[...])
    acc = jnp.swapaxes(acc.reshape(DIL_MAX, tm // DIL_MAX, d), 0, 1).reshape(tm, d)
    o_ref[...] = x_ref[...] + acc


def _attn_out_proj_residual(attn, w, x, *, batch, seq, tm=512):
    n, d = x.shape
    per = tm // DIL_MAX
    tps = seq // tm
    attn = attn.reshape(batch, DIL_MAX, seq // DIL_MAX, attn.shape[1])
    return pl.pallas_call(
        _attn_out_proj_kernel,
        grid=(n // tm,),
        in_specs=[
            pl.BlockSpec((None, DIL_MAX, per, attn.shape[-1]), lambda i: (i // tps, 0, i % tps, 0)),
            pl.BlockSpec(w.shape, lambda i: (0, 0)),
            pl.BlockSpec((tm, d), lambda i: (i, 0)),
        ],
        out_specs=pl.BlockSpec((tm, d), lambda i: (i, 0)),
        out_shape=jax.ShapeDtypeStruct((n, d), F32),
        compiler_params=_params("parallel"),
        name="attn_out_proj",
    )(attn, w, x)


def _mlp_kernel(x_ref, g_ref, w1_ref, w2_ref, gf_ref, *rest, final_norm, n_cast):
    cast_in, o_ref = rest[:n_cast], rest[n_cast]
    cast_out, xn_ref = rest[n_cast + 1:2 * n_cast + 1], rest[2 * n_cast + 1]
    f = pl.program_id(1)

    @pl.when(f == 0)
    def _():
        x = x_ref[...]
        xn_ref[...] = _rmsnorm(x, g_ref[...]).astype(BF16)
        o_ref[...] = x

    for src, dst in zip(cast_in, cast_out):
        dst[...] = src[...].astype(dst.dtype)

    h = _dot(xn_ref[...], w1_ref[...])
    h = jnp.square(jnp.maximum(h, 0.0)).astype(BF16)
    o_ref[...] += _dot(h, w2_ref[...])

    if final_norm:
        @pl.when(f == pl.num_programs(1) - 1)
        def _():
            o_ref[...] = _rmsnorm(o_ref[...], gf_ref[...])


def _mlp_residual(x, gain, w1, w2, final_gain, *, final_norm, cast_layers=(), tm=512, tf=2048):
    n, d = x.shape
    d_ff = w1.shape[1]
    steps = (n // tm) * (d_ff // tf)
    kern = functools.partial(_mlp_kernel, final_norm=final_norm, n_cast=len(cast_layers))
    cast_in_specs, cast_out_specs, cast_shapes = [], [], []
    for arr, layer in cast_layers:
        rows, cols = arr.shape[1:]
        slab = rows // steps
        assert rows % steps == 0 and slab % BF16_SUBLANES == 0
        cast_in_specs.append(pl.BlockSpec(
            (None, slab, cols), lambda i, f, layer=layer: (layer, i * (d_ff // tf) + f, 0)))
        cast_out_specs.append(pl.BlockSpec((slab, cols), lambda i, f: (i * (d_ff // tf) + f, 0)))
        cast_shapes.append(jax.ShapeDtypeStruct((rows, cols), BF16))
    vmem = (2 * (2 * _nbytes((tm, d), F32) + 2 * _nbytes((d, tf), BF16)) + _nbytes((tm, d), BF16)
            + _nbytes((tm, tf), F32) + _nbytes((tm, tf), BF16))
    vmem += sum(2 * (_nbytes(s.block_shape[1:], F32) + _nbytes(s.block_shape[1:], BF16))
                for s in cast_in_specs)
    vmem_limit = min(vmem + (4 << 20), V7X_VMEM_BYTES)
    out = pl.pallas_call(
        kern,
        grid=(n // tm, d_ff // tf),
        in_specs=[
            pl.BlockSpec((tm, d), lambda i, f: (i, 0)),
            pl.BlockSpec((1, d), lambda i, f: (0, 0)),
            pl.BlockSpec((d, tf), lambda i, f: (0, f)),
            pl.BlockSpec((tf, d), lambda i, f: (f, 0)),
            pl.BlockSpec((1, d), lambda i, f: (0, 0)),
        ] + cast_in_specs,
        out_specs=[pl.BlockSpec((tm, d), lambda i, f: (i, 0))] + cast_out_specs,
        out_shape=[jax.ShapeDtypeStruct((n, d), F32)] + cast_shapes,
        scratch_shapes=[pltpu.VMEM((tm, d), BF16)],
        compiler_params=_params("arbitrary", "arbitrary", vmem_limit_bytes=vmem_limit),
        name="mlp",
    )(x, gain, w1, w2, final_gain, *[arr for arr, _ in cast_layers])
    return out[0], out[1:]


def _attn_bias_tables():
    slopes = np.exp2(-8.0 * np.arange(1, ATTN_HEADS + 1) / ATTN_HEADS)

    def table(q_pos, k_pos, dilation):
        diff = q_pos[:, None] - k_pos[None, :]
        valid = (diff >= 0) & (diff <= ATTN_STEPS)
        bias = -(slopes * LOG2_E)[:, None, None] * (dilation * diff)[None]
        return np.where(valid[None], bias, MASK_VALUE).astype(np.float32)

    q = np.arange(128)
    b16_first = table(q, q, 16)
    b16 = table(q + 128, np.arange(256), 16)

    q4 = 4 * (q % 32) + q // 32
    k = np.arange(256)
    k4 = 4 * (k % 64) + k // 64
    b4 = np.stack([table(q4, k4, 4), table(q4 + 128, k4, 4)], axis=1)

    q = np.arange(256)
    q1 = 16 * (q % 16) + q // 16
    k = np.arange(384)
    k1 = 16 * (k % 24) + k // 24
    b1 = np.stack([table(q1, k1, 1), table(q1 + 128, k1, 1)], axis=1)
    return tuple(jnp.asarray(t) for t in (b1, b4, b16_first, b16))


def _attn_kernel(q_all, k_all, v_all, b1_all, b4_all, b16f_all, b16_all, o_all, *scratch, sub,
                 heads):
    def gather(ref, starts, size):
        return jnp.concatenate([ref[pl.ds(s, size), :] for s in starts], axis=0)

    def scatter(ref, starts, size, val):
        for c, s in enumerate(starts):
            ref[pl.ds(s, size), :] = val[c * size:(c + 1) * size]

    def lanes(x):
        return jnp.broadcast_to(x, (x.shape[0], LANES))

    def across_keys(x, n_keys):
        return jnp.concatenate([x] * (n_keys // LANES), axis=1)

    def weighted_values_and_sum(p, v):
        pv = _dot(p.astype(BF16), jnp.concatenate([v, jnp.ones_like(v)], axis=1))
        return pv[:, :LANES], pv[:, LANES:]

    def first_tile(q, k, v, bias):
        s = _dot_nt(q, k) + bias
        m = jnp.max(s, axis=-1, keepdims=True)
        acc, l = weighted_values_and_sum(jnp.exp2(s - m), v)
        return lanes(m), l, acc

    def next_tile(q, k, v, bias, m_old, l_old, acc_old):
        s = _dot_nt(q, k) + bias
        m_new = jnp.maximum(m_old, lanes(jnp.max(s, axis=-1, keepdims=True)))
        alpha = jnp.exp2(m_old - m_new)
        acc, l = weighted_values_and_sum(jnp.exp2(s - across_keys(m_new, s.shape[1])), v)
        return m_new, alpha * l_old + l, alpha * acc_old + acc

    dh = q_all.shape[1] // heads

    def loop(trips, body):
        if trips == 1:
            body(0, 0)
        else:
            lax.fori_loop(0, trips, body, 0)

    blocks_4 = min(BLOCKS_4, sub // 32)
    blocks_1 = min(BLOCKS_1, sub // 16)
    for group in (16, 4, 1):
        for h in range(heads):
            cols = slice(h * dh, (h + 1) * dh)
            head_refs = (q_all.at[:, cols], k_all.at[:, cols], v_all.at[:, cols], o_all.at[:, cols],
                         b1_all.at[h], b4_all.at[h], b16f_all.at[h], b16_all.at[h],
                         *scratch[3 * h:3 * h + 3])
            _attn_group(group, head_refs, sub, blocks_4, blocks_1, loop, first_tile, next_tile,
                        gather, scatter)


def _attn_group(group, head_refs, sub, blocks_4, blocks_1, loop, first_tile, next_tile, gather,
                scatter):
    (q_ref, k_ref, v_ref, o_ref, b1_ref, b4_ref, b16f_ref, b16_ref, m_ref, l_ref, acc_ref) = head_refs

    def group16(i, carry):
        bias_first, bias = b16f_ref[...], b16_ref[...]
        rows, res = [], []
        for u in range(TILES_16):
            base = (TILES_16 * i + u) * sub
            for j in range(sub // ATTN_STEPS):
                q_rows = pl.ds(pl.multiple_of(base + j * ATTN_STEPS, ATTN_STEPS), ATTN_STEPS)
                if j == 0:
                    k_rows, b = q_rows, bias_first
                else:
                    k_rows = pl.ds(pl.multiple_of(base + (j - 1) * ATTN_STEPS, ATTN_STEPS),
                                   2 * ATTN_STEPS)
                    b = bias
                rows.append(q_rows)
                res.append(first_tile(q_ref[q_rows, :], k_ref[k_rows, :], v_ref[k_rows, :], b))
        for r, (m, l, acc) in zip(rows, res):
            m_ref[r, :] = m
            l_ref[r, :] = l
            acc_ref[r, :] = acc
        return carry

    def group4(i, carry):
        starts, res = [], []
        for u in range(blocks_4):
            blk = blocks_4 * i + u
            bias = b4_ref[jnp.where(blk == 0, 0, 1)]
            k_blk = jnp.maximum(blk - 1, 0)
            for r4 in range(4):
                qs = [pl.multiple_of((r4 + 4 * c) * sub + 32 * blk, 32) for c in range(4)]
                ks = [pl.multiple_of((r4 + 4 * c) * sub + 32 * k_blk, 32) for c in range(4)]
                starts.append(qs)
                res.append(next_tile(
                    gather(q_ref, qs, 32), gather(k_ref, ks, 64), gather(v_ref, ks, 64), bias,
                    gather(m_ref, qs, 32), gather(l_ref, qs, 32), gather(acc_ref, qs, 32)))
        for qs, (m, l, acc) in zip(starts, res):
            scatter(m_ref, qs, 32, m)
            scatter(l_ref, qs, 32, l)
            scatter(acc_ref, qs, 32, acc)
        return carry

    def group1(i, carry):
        res = []
        for u in range(blocks_1):
            blk = blocks_1 * i + u
            bias = b1_ref[jnp.where(blk == 0, 0, 1)]
            k_lo = jnp.maximum(16 * blk - 8, 0)
            qs = [pl.multiple_of(r * sub + 16 * blk, 16) for r in range(DIL_MAX)]
            ks = [pl.multiple_of(r * sub + k_lo, 8) for r in range(DIL_MAX)]
            _, l, acc = next_tile(
                gather(q_ref, qs, 16), gather(k_ref, ks, 24), gather(v_ref, ks, 24), bias,
                gather(m_ref, qs, 16), gather(l_ref, qs, 16), gather(acc_ref, qs, 16))
            res.append((qs, (acc / l).astype(o_ref.dtype)))
        for qs, out in res:
            scatter(o_ref, qs, 16, out)
        return carry

    if group == 16:
        loop(DIL_MAX // TILES_16, group16)
    elif group == 4:
        loop(sub // (32 * blocks_4), group4)
    else:
        loop(sub // (16 * blocks_1), group1)


def _dilated_attention(qkv, *, batch, seq, dh):
    n = qkv.shape[0]
    sub = seq // DIL_MAX
    b1, b4, b16_first, b16 = _attn_bias_tables()
    hp = ATTN_HEADS_PER_STEP
    groups = ATTN_HEADS // hp
    kern = functools.partial(_attn_kernel, sub=sub, heads=hp)
    return pl.pallas_call(
        kern,
        grid=(batch, groups),
        in_specs=[
            pl.BlockSpec((seq, hp * dh), lambda b, h: (b, h)),
            pl.BlockSpec((seq, hp * dh), lambda b, h: (b, groups + h)),
            pl.BlockSpec((seq, hp * dh), lambda b, h: (b, 2 * groups + h)),
            pl.BlockSpec((hp,) + b1.shape[1:], lambda b, h: (h, 0, 0, 0)),
            pl.BlockSpec((hp,) + b4.shape[1:], lambda b, h: (h, 0, 0, 0)),
            pl.BlockSpec((hp,) + b16_first.shape[1:], lambda b, h: (h, 0, 0)),
            pl.BlockSpec((hp,) + b16.shape[1:], lambda b, h: (h, 0, 0)),
        ],
        out_specs=pl.BlockSpec((seq, hp * dh), lambda b, h: (b, h)),
        out_shape=jax.ShapeDtypeStruct((n, ATTN_HEADS * dh), BF16),
        scratch_shapes=[pltpu.VMEM((seq, LANES), F32), pltpu.VMEM((seq, LANES), F32),
                        pltpu.VMEM((seq, dh), F32)] * hp,
        compiler_params=_params("parallel", "parallel"),
        name="dilated_attn",
    )(qkv, qkv, qkv, b1, b4, b16_first, b16)


def kernel(x, norm_mix_g, norm_mlp_g, final_norm_g, hyb_w_in, conv_w, gla_w_gate2, gla_b_gate,
           gla_norm_g, hyb_w_out, attn_w_qkv, attn_w_o, mlp_w1, mlp_w2):
    batch, seq, d = x.shape
    n = batch * seq
    depth = norm_mix_g.shape[0]
    conv_ch = conv_w.shape[-1]
    dv = gla_norm_g.shape[-1]
    dk = gla_w_gate2.shape[-1] // GLA_HEADS
    dh = d // ATTN_HEADS
    assert depth == 2 and seq % (DIL_MAX * ATTN_STEPS) == 0 and seq % GLA_CHUNK == 0
    assert dk == LANES and dh == LANES and dv % LANES == 0

    row = lambda v: v.reshape(1, -1).astype(F32)
    xf = x.reshape(n, d)

    main_cols = 3 * conv_ch + 2 * GLA_HEADS * dk + 2 * GLA_HEADS * dv
    w_in = hyb_w_in[0]
    w_low = jnp.pad(w_in[:, main_cols:], ((0, 0), (0, LANES - GLA_GATE_RANK))).astype(BF16)
    w_gate = jnp.pad(gla_w_gate2[0], ((0, LANES - GLA_GATE_RANK), (0, 0))).astype(BF16)
    proj, log_a, (w1_first, w2_first, w_out, w_qkv, w_o) = _in_proj_gla(
        xf, row(norm_mix_g[0]), w_in.astype(BF16), w_low, w_gate, row(gla_b_gate[0]),
        cols=main_cols,
        cast_layers=((mlp_w1, 0), (mlp_w2, 0), (hyb_w_out, 0), (attn_w_qkv, 0), (attn_w_o, 0)))

    q_off = 3 * conv_ch
    k_off = q_off + GLA_HEADS * dk
    v_off = k_off + GLA_HEADS * dk
    r_off = v_off + GLA_HEADS * dv
    tri = jnp.asarray(np.tril(np.ones((GLA_CHUNK, GLA_CHUNK), np.float32)), BF16)
    o_gla = _gla(proj, log_a, tri, row(gla_norm_g[0]), batch=batch, seq=seq,
                 q_off=q_off, k_off=k_off, v_off=v_off, r_off=r_off, dk=dk, dv=dv)
    xf = _conv_out_proj_residual(proj, conv_w[0].astype(F32), o_gla, w_out, xf, seq=seq)
    xf, (w1_next, w2_next) = _mlp_residual(
        xf, row(norm_mlp_g[0]), w1_first, w2_first, row(final_norm_g),
        final_norm=False, cast_layers=((mlp_w1, 1), (mlp_w2, 1)))

    col_scale = jnp.concatenate([jnp.full((1, d), float(dh) ** -0.5 * LOG2_E, F32),
                                 jnp.ones((1, 2 * d), F32)], axis=1)
    qkv = _qkv_proj_residue_major(xf, row(norm_mix_g[1]), w_qkv, col_scale, batch=batch, seq=seq)
    attn = _dilated_attention(qkv, batch=batch, seq=seq, dh=dh)
    xf = _attn_out_proj_residual(attn, w_o, xf, batch=batch, seq=seq)
    out, _ = _mlp_residual(xf, row(norm_mlp_g[1]), w1_next, w2_next, row(final_norm_g),
                           final_norm=True)
    return out.reshape(batch, seq, d)
```

```python
import functools

import jax
import jax.numpy as jnp
import numpy as np
from jax import lax
from jax.experimental import pallas as pl
from jax.experimental.pallas import tpu as pltpu

F32 = jnp.float32
BF16 = jnp.bfloat16

NORM_EPS = 1e-6
CONV_K = 3
GLA_HEADS = 4
GLA_GATE_RANK = 16
GLA_GATE_TEMP = 16.0
ATTN_HEADS = 16
ATTN_HEADS_PER_STEP = 2
DILATED_GROUPS = ((128, 1), (512, 4), (2048, 16))
DIL_MAX = 16
ATTN_STEPS = 128

LANES = 128
F32_SUBLANES = 8
BF16_SUBLANES = 16
GLA_CHUNK = 256
GLA_SUB = 16
GLA_BLOCK = 64
GLA_MAX_BLOCK_DECAY = 60.0
MASK_VALUE = -1e30
LOG2_E = 1.4426950408889634
TILES_16 = 16
BLOCKS_4 = 8
BLOCKS_1 = 16
V7X_VMEM_BYTES = 64 * 1024 * 1024
VMEM_LIMIT_BYTES = 56 * 1024 * 1024


def _params(*semantics, vmem_limit_bytes=VMEM_LIMIT_BYTES):
    return pltpu.CompilerParams(dimension_semantics=semantics, vmem_limit_bytes=vmem_limit_bytes)


def _nbytes(shape, dtype):
    return int(np.prod(shape)) * jnp.dtype(dtype).itemsize


def _dot(a, b):
    return jnp.dot(a, b, preferred_element_type=F32)


def _dot_nt(a, b):
    return lax.dot_general(a, b, (((1,), (1,)), ((), ())), preferred_element_type=F32)


def _dot_tn(a, b):
    return lax.dot_general(a, b, (((0,), (0,)), ((), ())), preferred_element_type=F32)


def _rmsnorm(x, g):
    return x * lax.rsqrt(jnp.mean(x * x, axis=-1, keepdims=True) + NORM_EPS) * g


def _tile_of_trip(i):
    return jnp.maximum(i - 1, 0)


def _chunk_index(i, j, n_tiles, chunks):
    return jnp.minimum(i, n_tiles - 1) * chunks + j


def _in_proj_gla_kernel(xc_ref, g_ref, w_ref, wlow_ref, wgate_ref, bgate_ref, *rest, n_cast):
    cast_in, (o_ref, loga_ref) = rest[:n_cast], rest[n_cast:n_cast + 2]
    cast_out, xn_refs = rest[n_cast + 2:2 * n_cast + 2], rest[2 * n_cast + 2:]
    i, j = pl.program_id(0), pl.program_id(1)
    ch = xc_ref.shape[0]
    rows = pl.ds(pl.multiple_of(j * ch, ch), ch)

    def norm_and_gate_chunk(fill_ref):
        xn = _rmsnorm(xc_ref[...], g_ref[...]).astype(BF16)
        fill_ref[rows, :] = xn
        g_low = _dot(xn, wlow_ref[...]).astype(BF16)
        gate = _dot(g_low, wgate_ref[...]) + bgate_ref[...]
        log_sig = jnp.minimum(gate, 0.0) - jnp.log(1.0 + jnp.exp(-jnp.abs(gate)))
        loga_ref[rows, :] = log_sig * (1.0 / GLA_GATE_TEMP)
        for src, dst in zip(cast_in, cast_out):
            dst[...] = src[...].astype(dst.dtype)

    @pl.when(i == 0)
    def _():
        norm_and_gate_chunk(xn_refs[0])

    for parity in range(2):
        @pl.when((i > 0) & (i % 2 == parity))
        def _():
            norm_and_gate_chunk(xn_refs[parity])
            o_ref[...] = _dot(xn_refs[1 - parity][...], w_ref[...]).astype(o_ref.dtype)


def _in_proj_gla(x, gain, w, w_low, w_gate, b_gate, *, cols, cast_layers=(), tm=1024, tn=1536):
    n, d = x.shape
    n_gate = w_gate.shape[1]
    n_tiles, steps = n // tm, cols // tn
    ch = tm // steps
    cast_in_specs, cast_out_specs, cast_shapes = [], [], []
    for arr, layer in cast_layers:
        rows, width = arr.shape[1:]
        slab = rows // (n_tiles * steps)
        assert rows % (n_tiles * steps) == 0 and slab % BF16_SUBLANES == 0
        last = n_tiles * steps - 1
        cast_in_specs.append(pl.BlockSpec(
            (None, slab, width),
            lambda i, j, layer=layer, last=last: (layer, jnp.minimum(i * steps + j, last), 0)))
        cast_out_specs.append(pl.BlockSpec(
            (slab, width), lambda i, j, last=last: (jnp.minimum(i * steps + j, last), 0)))
        cast_shapes.append(jax.ShapeDtypeStruct((rows, width), BF16))
    kern = functools.partial(_in_proj_gla_kernel, n_cast=len(cast_layers))
    out = pl.pallas_call(
        kern,
        grid=(n_tiles + 1, steps),
        in_specs=[
            pl.BlockSpec((ch, d), lambda i, j: (_chunk_index(i, j, n_tiles, steps), 0)),
            pl.BlockSpec((1, d), lambda i, j: (0, 0)),
            pl.BlockSpec((d, tn), lambda i, j: (0, j)),
            pl.BlockSpec((d, LANES), lambda i, j: (0, 0)),
            pl.BlockSpec((LANES, n_gate), lambda i, j: (0, 0)),
            pl.BlockSpec((1, n_gate), lambda i, j: (0, 0)),
        ] + cast_in_specs,
        out_specs=[
            pl.BlockSpec((tm, tn), lambda i, j: (_tile_of_trip(i), j * jnp.minimum(i, 1))),
            pl.BlockSpec((tm, n_gate), lambda i, j: (jnp.minimum(i, n_tiles - 1), 0)),
        ] + cast_out_specs,
        out_shape=[
            jax.ShapeDtypeStruct((n, cols), BF16),
            jax.ShapeDtypeStruct((n, n_gate), F32),
        ] + cast_shapes,
        scratch_shapes=[pltpu.VMEM((tm, d), BF16), pltpu.VMEM((tm, d), BF16)],
        compiler_params=_params("arbitrary", "arbitrary"),
        name="in_proj_gla",
    )(x, gain, w, w_low, w_gate, b_gate, *[arr for arr, _ in cast_layers])
    return out[0], out[1], out[2:]


def _qkv_proj_kernel(xc_ref, g_ref, w_ref, scale_ref, o_ref, *xn_refs):
    i, j = pl.program_id(0), pl.program_id(1)
    ch, d = xc_ref.shape
    per = ch // DIL_MAX

    def norm_chunk(fill_ref):
        xn = _rmsnorm(xc_ref[...], g_ref[...]).astype(BF16)
        fill_ref[:, pl.ds(pl.multiple_of(j * per, per), per), :] = (
            jnp.swapaxes(xn.reshape(per, DIL_MAX, d), 0, 1))

    @pl.when(i == 0)
    def _():
        norm_chunk(xn_refs[0])

    for parity in range(2):
        @pl.when((i > 0) & (i % 2 == parity))
        def _():
            norm_chunk(xn_refs[parity])
            xn = xn_refs[1 - parity][...]
            acc = _dot(xn.reshape(xn.shape[0] * xn.shape[1], d), w_ref[...]) * scale_ref[...]
            o_ref[...] = acc.astype(o_ref.dtype).reshape(o_ref.shape)


def _qkv_proj_residue_major(x, gain, w, col_scale, *, batch, seq, tm=1024, tn=1536):
    n, d = x.shape
    cols = w.shape[1]
    sub = seq // DIL_MAX
    per = tm // DIL_MAX
    tiles_per_seq = seq // tm
    n_tiles, steps = n // tm, cols // tn
    ch = tm // steps
    assert ch % (DIL_MAX * BF16_SUBLANES) == 0

    def out_index(i, j):
        t = _tile_of_trip(i)
        return (t // tiles_per_seq, 0, t % tiles_per_seq, j * jnp.minimum(i, 1))

    out = pl.pallas_call(
        _qkv_proj_kernel,
        grid=(n_tiles + 1, steps),
        in_specs=[
            pl.BlockSpec((ch, d), lambda i, j: (_chunk_index(i, j, n_tiles, steps), 0)),
            pl.BlockSpec((1, d), lambda i, j: (0, 0)),
            pl.BlockSpec((d, tn), lambda i, j: (0, j)),
            pl.BlockSpec((1, tn), lambda i, j: (0, j)),
        ],
        out_specs=pl.BlockSpec((None, DIL_MAX, per, tn), out_index),
        out_shape=jax.ShapeDtypeStruct((batch, DIL_MAX, sub, cols), BF16),
        scratch_shapes=[pltpu.VMEM((DIL_MAX, per, d), BF16), pltpu.VMEM((DIL_MAX, per, d), BF16)],
        compiler_params=_params("arbitrary", "arbitrary"),
        name="qkv_proj",
    )(x, gain, w, col_scale)
    return out.reshape(n, cols)


def _conv_out_proj_kernel(ab_ref, ac_ref, ax_ref, cw_ref, gla_ref, wa_ref, wb_ref, x_ref, o_ref,
                          ubuf_ref, *, tiles_per_seq):
    tb = ab_ref.shape[0]
    pad = F32_SUBLANES

    @pl.when(pl.program_id(0) % tiles_per_seq == 0)
    def _():
        ubuf_ref[0:pad, :] = jnp.zeros((pad, ubuf_ref.shape[1]), F32)

    acc = x_ref[...] + _dot(gla_ref[...], wb_ref[...])
    u = ac_ref[...].astype(F32) * ax_ref[...].astype(F32)
    ubuf_ref[pad:pad + tb, :] = u
    u1 = ubuf_ref[pad - 1:pad - 1 + tb, :]
    u2 = ubuf_ref[pad - 2:pad - 2 + tb, :]
    cw = cw_ref[...]
    y = ab_ref[...].astype(F32) * (cw[0:1, :] * u2 + cw[1:2, :] * u1 + cw[2:3, :] * u)
    o_ref[...] = acc + _dot(y.astype(BF16), wa_ref[...])
    ubuf_ref[0:pad, :] = ubuf_ref[tb:tb + pad, :]


def _conv_out_proj_residual(proj, conv_w, o_gla, w_out, x, *, seq, tm=512):
    n, d = x.shape
    ch = conv_w.shape[1]
    assert w_out.shape[0] == 2 * ch and o_gla.shape[1] == ch
    kern = functools.partial(_conv_out_proj_kernel, tiles_per_seq=seq // tm)
    return pl.pallas_call(
        kern,
        grid=(n // tm,),
        in_specs=[
            pl.BlockSpec((tm, ch), lambda i: (i, 0)),
            pl.BlockSpec((tm, ch), lambda i: (i, 1)),
            pl.BlockSpec((tm, ch), lambda i: (i, 2)),
            pl.BlockSpec((CONV_K, ch), lambda i: (0, 0)),
            pl.BlockSpec((tm, o_gla.shape[1]), lambda i: (i, 0)),
            pl.BlockSpec((ch, d), lambda i: (0, 0)),
            pl.BlockSpec((ch, d), lambda i: (1, 0)),
            pl.BlockSpec((tm, d), lambda i: (i, 0)),
        ],
        out_specs=pl.BlockSpec((tm, d), lambda i: (i, 0)),
        out_shape=jax.ShapeDtypeStruct((n, d), F32),
        scratch_shapes=[pltpu.VMEM((tm + F32_SUBLANES, ch), F32)],
        compiler_params=_params("arbitrary"),
        name="conv_out_proj",
    )(proj, proj, proj, conv_w, o_gla, w_out, w_out, x)


def _gla_scores_factored(q, k, g):
    c_len = q.shape[0]
    score_rows = []
    for a in range(c_len // GLA_BLOCK):
        lo, hi = a * GLA_BLOCK, (a + 1) * GLA_BLOCK
        n_keys = LANES * -(-hi // LANES)
        g_first = g[lo:lo + 1, :]
        q_a = (q[lo:hi] * jnp.exp(g[lo:hi] - g_first)).astype(BF16)
        k_a = (k[:n_keys] * jnp.exp(jnp.minimum(g_first - g[:n_keys], GLA_MAX_BLOCK_DECAY)))
        s_a = _dot_nt(q_a, k_a.astype(BF16))
        row = lax.broadcasted_iota(jnp.int32, s_a.shape, 0) + lo
        col = lax.broadcasted_iota(jnp.int32, s_a.shape, 1)
        s_a = jnp.where(col <= row, s_a, 0.0)
        if n_keys < c_len:
            s_a = jnp.concatenate([s_a, jnp.zeros((GLA_BLOCK, c_len - n_keys), F32)], axis=1)
        score_rows.append(s_a)
    return jnp.concatenate(score_rows, axis=0)


def _gla_scores_pairwise(q, k, g):
    c_len = q.shape[0]
    n_sub = c_len // GLA_SUB
    score_rows = [jnp.zeros((GLA_SUB, c_len), F32)]
    for a in range(1, n_sub):
        lo = a * GLA_SUB
        n_keys = LANES * -(-lo // LANES)
        g_ref_row = g[lo:lo + 1, :]
        q_a = (q[lo:lo + GLA_SUB] * jnp.exp(g[lo:lo + GLA_SUB] - g_ref_row)).astype(BF16)
        k_a = (k[:n_keys] * jnp.exp(jnp.minimum(g_ref_row - g[:n_keys], 0.0))).astype(BF16)
        s_a = _dot_nt(q_a, k_a)
        col = lax.broadcasted_iota(jnp.int32, s_a.shape, 1)
        s_a = jnp.where(col < lo, s_a, 0.0)
        if n_keys < c_len:
            s_a = jnp.concatenate([s_a, jnp.zeros((GLA_SUB, c_len - n_keys), F32)], axis=1)
        score_rows.append(s_a)
    scores = jnp.concatenate(score_rows, axis=0)

    row = lax.broadcasted_iota(jnp.int32, (c_len, LANES), 0)
    lane = lax.broadcasted_iota(jnp.int32, (c_len, LANES), 1)
    delta = row % LANES - lane
    band_id = jnp.where((delta >= 0) & (delta <= row % GLA_SUB), delta, -1)
    band = jnp.zeros((c_len, LANES), F32)
    for d in range(GLA_SUB):
        k_d = k if d == 0 else pltpu.roll(k, d, 0)
        g_d = g if d == 0 else pltpu.roll(g, d, 0)
        e = jnp.exp(jnp.minimum(g - g_d, 0.0))
        diag = jnp.sum(q * k_d * e, axis=-1, keepdims=True)
        band = jnp.where(band_id == d, diag, band)
    zeros = jnp.zeros((LANES, LANES), F32)
    band_rows = []
    for t in range(c_len // LANES):
        tiles = [zeros] * (c_len // LANES)
        tiles[t] = band[t * LANES:(t + 1) * LANES]
        band_rows.append(jnp.concatenate(tiles, axis=1))
    return scores + jnp.concatenate(band_rows, axis=0)


def _gla_kernel(q_ref, k_ref, v_ref, r_ref, la_ref, tri_ref, gn_ref, o_ref, st_ref, g_ref,
                inter_ref, *, q_scale, dk, dv):
    c_len = q_ref.shape[0]

    @pl.when(pl.program_id(1) == 0)
    def _():
        st_ref[...] = jnp.zeros_like(st_ref)

    la = la_ref[...]
    tri = tri_ref[...]
    la_hi = la.astype(BF16)
    rem = la - la_hi.astype(F32)
    la_mid = rem.astype(BF16)
    la_lo = (rem - la_mid.astype(F32)).astype(BF16)
    g_all = _dot(tri, la_hi) + _dot(tri, la_mid) + _dot(tri, la_lo)
    g_ref[...] = g_all
    block_decay = [g_all[lo:lo + 1, :] - g_all[lo + GLA_BLOCK - 1:lo + GLA_BLOCK, :]
                   for lo in range(0, c_len, GLA_BLOCK)]
    worst_decay = jnp.max(jnp.concatenate(block_decay, axis=0))

    def head_operands(h):
        g = g_ref[:, h * dk:(h + 1) * dk]
        q = q_ref[:, h * dk:(h + 1) * dk].astype(F32) * q_scale
        k = k_ref[:, h * dk:(h + 1) * dk].astype(F32)
        return g, q, k, v_ref[:, h * dv:(h + 1) * dv]

    def finish_head(h, o):
        o = _rmsnorm(o, gn_ref[...])
        r = r_ref[:, h * dv:(h + 1) * dv].astype(F32)
        o_ref[:, h * dv:(h + 1) * dv] = (o * (r / (1.0 + jnp.exp(-r)))).astype(o_ref.dtype)

    for h in range(GLA_HEADS):
        g, q, k, v = head_operands(h)
        g_last = g[c_len - 1:c_len, :]
        st = st_ref[h]
        o_inter = _dot_nt((q * jnp.exp(g)).astype(BF16), st.astype(BF16))
        inter_ref[h] = o_inter
        k_dec = (k * jnp.exp(g_last - g)).astype(BF16)
        st_ref[h] = st * jnp.exp(g_last) + _dot_tn(v, k_dec)
        finish_head(h, o_inter + _dot(_gla_scores_factored(q, k, g).astype(BF16), v))

    @pl.when(worst_decay > GLA_MAX_BLOCK_DECAY)
    def _():
        for h in range(GLA_HEADS):
            g, q, k, v = head_operands(h)
            finish_head(h, inter_ref[h] + _dot(_gla_scores_pairwise(q, k, g).astype(BF16), v))


def _gla(proj, log_a, tri, norm_g, *, batch, seq, q_off, k_off, v_off, r_off, dk, dv):
    n = proj.shape[0]
    cpb = seq // GLA_CHUNK
    qk_w, vr_w = GLA_HEADS * dk, GLA_HEADS * dv
    row = lambda b, c: b * cpb + c
    kern = functools.partial(_gla_kernel, q_scale=float(dk) ** -0.5, dk=dk, dv=dv)
    return pl.pallas_call(
        kern,
        grid=(batch, cpb),
        in_specs=[
            pl.BlockSpec((GLA_CHUNK, qk_w), lambda b, c: (row(b, c), q_off // qk_w)),
            pl.BlockSpec((GLA_CHUNK, qk_w), lambda b, c: (row(b, c), k_off // qk_w)),
            pl.BlockSpec((GLA_CHUNK, vr_w), lambda b, c: (row(b, c), v_off // vr_w)),
            pl.BlockSpec((GLA_CHUNK, vr_w), lambda b, c: (row(b, c), r_off // vr_w)),
            pl.BlockSpec((GLA_CHUNK, qk_w), lambda b, c: (row(b, c), 0)),
            pl.BlockSpec((GLA_CHUNK, GLA_CHUNK), lambda b, c: (0, 0)),
            pl.BlockSpec((1, dv), lambda b, c: (0, 0)),
        ],
        out_specs=pl.BlockSpec((GLA_CHUNK, vr_w), lambda b, c: (row(b, c), 0)),
        out_shape=jax.ShapeDtypeStruct((n, vr_w), BF16),
        scratch_shapes=[pltpu.VMEM((GLA_HEADS, dv, dk), F32), pltpu.VMEM((GLA_CHUNK, qk_w), F32),
                        pltpu.VMEM((GLA_HEADS, GLA_CHUNK, dv), F32)],
        compiler_params=_params("parallel", "arbitrary"),
        name="gla",
    )(proj, proj, proj, proj, log_a, tri, norm_g)


def _attn_out_proj_kernel(a_ref, w_ref, x_ref, o_ref):
    tm, d = x_ref.shape
    acc = _dot(a_ref[...].reshape(tm, a_ref.shape[-1]), w_ref[...])
    acc = jnp.swapaxes(acc.reshape(DIL_MAX, tm // DIL_MAX, d), 0, 1).reshape(tm, d)
    o_ref[...] = x_ref[...] + acc


def _attn_out_proj_residual(attn, w, x, *, batch, seq, tm=512):
    n, d = x.shape
    per = tm // DIL_MAX
    tps = seq // tm
    attn = attn.reshape(batch, DIL_MAX, seq // DIL_MAX, attn.shape[1])
    return pl.pallas_call(
        _attn_out_proj_kernel,
        grid=(n // tm,),
        in_specs=[
            pl.BlockSpec((None, DIL_MAX, per, attn.shape[-1]), lambda i: (i // tps, 0, i % tps, 0)),
            pl.BlockSpec(w.shape, lambda i: (0, 0)),
            pl.BlockSpec((tm, d), lambda i: (i, 0)),
        ],
        out_specs=pl.BlockSpec((tm, d), lambda i: (i, 0)),
        out_shape=jax.ShapeDtypeStruct((n, d), F32),
        compiler_params=_params("parallel"),
        name="attn_out_proj",
    )(attn, w, x)


def _mlp_kernel(x_ref, g_ref, w1_ref, w2_ref, gf_ref, *rest, final_norm, n_cast):
    cast_in, o_ref = rest[:n_cast], rest[n_cast]
    cast_out, xn_ref = rest[n_cast + 1:2 * n_cast + 1], rest[2 * n_cast + 1]
    f = pl.program_id(1)

    def hidden_times_w2():
        for src, dst in zip(cast_in, cast_out):
            dst[...] = src[...].astype(dst.dtype)
        h = _dot(xn_ref[...], w1_ref[...])
        return _dot(jnp.square(jnp.maximum(h, 0.0)).astype(BF16), w2_ref[...])

    @pl.when(f == 0)
    def _():
        xn_ref[...] = _rmsnorm(x_ref[...], g_ref[...]).astype(BF16)
        o_ref[...] = x_ref[...] + hidden_times_w2()

    @pl.when(f > 0)
    def _():
        o_ref[...] += hidden_times_w2()

    if final_norm:
        @pl.when(f == pl.num_programs(1) - 1)
        def _():
            o_ref[...] = _rmsnorm(o_ref[...], gf_ref[...])


def _mlp_residual(x, gain, w1, w2, final_gain, *, final_norm, cast_layers=(), tm=512, tf=2048):
    n, d = x.shape
    d_ff = w1.shape[1]
    steps = (n // tm) * (d_ff // tf)
    kern = functools.partial(_mlp_kernel, final_norm=final_norm, n_cast=len(cast_layers))
    cast_in_specs, cast_out_specs, cast_shapes = [], [], []
    for arr, layer in cast_layers:
        rows, cols = arr.shape[1:]
        slab = rows // steps
        assert rows % steps == 0 and slab % BF16_SUBLANES == 0
        cast_in_specs.append(pl.BlockSpec(
            (None, slab, cols), lambda i, f, layer=layer: (layer, i * (d_ff // tf) + f, 0)))
        cast_out_specs.append(pl.BlockSpec((slab, cols), lambda i, f: (i * (d_ff // tf) + f, 0)))
        cast_shapes.append(jax.ShapeDtypeStruct((rows, cols), BF16))
    vmem = (2 * (2 * _nbytes((tm, d), F32) + 2 * _nbytes((d, tf), BF16)) + _nbytes((tm, d), BF16)
            + _nbytes((tm, tf), F32) + _nbytes((tm, tf), BF16))
    vmem += sum(2 * (_nbytes(s.block_shape[1:], F32) + _nbytes(s.block_shape[1:], BF16))
                for s in cast_in_specs)
    vmem_limit = min(vmem + (4 << 20), V7X_VMEM_BYTES)
    out = pl.pallas_call(
        kern,
        grid=(n // tm, d_ff // tf),
        in_specs=[
            pl.BlockSpec((tm, d), lambda i, f: (i, 0)),
            pl.BlockSpec((1, d), lambda i, f: (0, 0)),
            pl.BlockSpec((d, tf), lambda i, f: (0, f)),
            pl.BlockSpec((tf, d), lambda i, f: (f, 0)),
            pl.BlockSpec((1, d), lambda i, f: (0, 0)),
        ] + cast_in_specs,
        out_specs=[pl.BlockSpec((tm, d), lambda i, f: (i, 0))] + cast_out_specs,
        out_shape=[jax.ShapeDtypeStruct((n, d), F32)] + cast_shapes,
        scratch_shapes=[pltpu.VMEM((tm, d), BF16)],
        compiler_params=_params("arbitrary", "arbitrary", vmem_limit_bytes=vmem_limit),
        name="mlp",
    )(x, gain, w1, w2, final_gain, *[arr for arr, _ in cast_layers])
    return out[0], out[1:]


def _attn_bias_tables():
    slopes = np.exp2(-8.0 * np.arange(1, ATTN_HEADS + 1) / ATTN_HEADS)

    def table(q_pos, k_pos, dilation):
        diff = q_pos[:, None] - k_pos[None, :]
        valid = (diff >= 0) & (diff <= ATTN_STEPS)
        bias = -(slopes * LOG2_E)[:, None, None] * (dilation * diff)[None]
        return np.where(valid[None], bias, MASK_VALUE).astype(np.float32)

    q = np.arange(128)
    b16_first = table(q, q, 16)
    b16 = table(q + 128, np.arange(256), 16)

    q4 = 4 * (q % 32) + q // 32
    k = np.arange(256)
    k4 = 4 * (k % 64) + k // 64
    b4 = np.stack([table(q4, k4, 4), table(q4 + 128, k4, 4)], axis=1)

    q = np.arange(256)
    q1 = 16 * (q % 16) + q // 16
    k = np.arange(384)
    k1 = 16 * (k % 24) + k // 24
    b1 = np.stack([table(q1, k1, 1), table(q1 + 128, k1, 1)], axis=1)
    return tuple(jnp.asarray(t) for t in (b1, b4, b16_first, b16))


def _attn_kernel(q_all, k_all, v_all, b1_all, b4_all, b16f_all, b16_all, o_all, *scratch, sub,
                 heads):
    def gather(ref, starts, size):
        return jnp.concatenate([ref[pl.ds(s, size), :] for s in starts], axis=0)

    def scatter(ref, starts, size, val):
        for c, s in enumerate(starts):
            ref[pl.ds(s, size), :] = val[c * size:(c + 1) * size]

    def lanes(x):
        return jnp.broadcast_to(x, (x.shape[0], LANES))

    def across_keys(x, n_keys):
        return jnp.concatenate([x] * (n_keys // LANES), axis=1)

    def weighted_values_and_sum(p, v):
        pv = _dot(p.astype(BF16), jnp.concatenate([v, jnp.ones_like(v)], axis=1))
        return pv[:, :LANES], pv[:, LANES:]

    def first_tile(q, k, v, bias):
        s = _dot_nt(q, k) + bias
        m = jnp.max(s, axis=-1, keepdims=True)
        acc, l = weighted_values_and_sum(jnp.exp2(s - m), v)
        return lanes(m), l, acc

    def next_tile(q, k, v, bias, m_old, l_old, acc_old):
        s = _dot_nt(q, k) + bias
        m_new = jnp.maximum(m_old, lanes(jnp.max(s, axis=-1, keepdims=True)))
        alpha = jnp.exp2(m_old - m_new)
        acc, l = weighted_values_and_sum(jnp.exp2(s - across_keys(m_new, s.shape[1])), v)
        return m_new, alpha * l_old + l, alpha * acc_old + acc

    dh = q_all.shape[1] // heads

    def loop(trips, body):
        if trips == 1:
            body(0, 0)
        else:
            lax.fori_loop(0, trips, body, 0)

    blocks_4 = min(BLOCKS_4, sub // 32)
    blocks_1 = min(BLOCKS_1, sub // 16)
    for group in (16, 4, 1):
        for h in range(heads):
            cols = slice(h * dh, (h + 1) * dh)
            head_refs = (q_all.at[:, cols], k_all.at[:, cols], v_all.at[:, cols], o_all.at[:, cols],
                         b1_all.at[h], b4_all.at[h], b16f_all.at[h], b16_all.at[h],
                         *scratch[3 * h:3 * h + 3])
            _attn_group(group, head_refs, sub, blocks_4, blocks_1, loop, first_tile, next_tile,
                        gather, scatter)


def _attn_group(group, head_refs, sub, blocks_4, blocks_1, loop, first_tile, next_tile, gather,
                scatter):
    (q_ref, k_ref, v_ref, o_ref, b1_ref, b4_ref, b16f_ref, b16_ref, m_ref, l_ref, acc_ref) = head_refs

    def group16(i, carry):
        bias_first, bias = b16f_ref[...], b16_ref[...]
        rows, res = [], []
        for u in range(TILES_16):
            base = (TILES_16 * i + u) * sub
            for j in range(sub // ATTN_STEPS):
                q_rows = pl.ds(pl.multiple_of(base + j * ATTN_STEPS, ATTN_STEPS), ATTN_STEPS)
                if j == 0:
                    k_rows, b = q_rows, bias_first
                else:
                    k_rows = pl.ds(pl.multiple_of(base + (j - 1) * ATTN_STEPS, ATTN_STEPS),
                                   2 * ATTN_STEPS)
                    b = bias
                rows.append(q_rows)
                res.append(first_tile(q_ref[q_rows, :], k_ref[k_rows, :], v_ref[k_rows, :], b))
        for r, (m, l, acc) in zip(rows, res):
            m_ref[r, :] = m
            l_ref[r, :] = l
            acc_ref[r, :] = acc
        return carry

    def group4(i, carry):
        starts, res = [], []
        for u in range(blocks_4):
            blk = blocks_4 * i + u
            bias = b4_ref[jnp.where(blk == 0, 0, 1)]
            k_blk = jnp.maximum(blk - 1, 0)
            for r4 in range(4):
                qs = [pl.multiple_of((r4 + 4 * c) * sub + 32 * blk, 32) for c in range(4)]
                ks = [pl.multiple_of((r4 + 4 * c) * sub + 32 * k_blk, 32) for c in range(4)]
                starts.append(qs)
                res.append(next_tile(
                    gather(q_ref, qs, 32), gather(k_ref, ks, 64), gather(v_ref, ks, 64), bias,
                    gather(m_ref, qs, 32), gather(l_ref, qs, 32), gather(acc_ref, qs, 32)))
        for qs, (m, l, acc) in zip(starts, res):
            scatter(m_ref, qs, 32, m)
            scatter(l_ref, qs, 32, l)
            scatter(acc_ref, qs, 32, acc)
        return carry

    def group1(i, carry):
        res = []
        for u in range(blocks_1):
            blk = blocks_1 * i + u
            bias = b1_ref[jnp.where(blk == 0, 0, 1)]
            k_lo = jnp.maximum(16 * blk - 8, 0)
            qs = [pl.multiple_of(r * sub + 16 * blk, 16) for r in range(DIL_MAX)]
            ks = [pl.multiple_of(r * sub + k_lo, 8) for r in range(DIL_MAX)]
            _, l, acc = next_tile(
                gather(q_ref, qs, 16), gather(k_ref, ks, 24), gather(v_ref, ks, 24), bias,
                gather(m_ref, qs, 16), gather(l_ref, qs, 16), gather(acc_ref, qs, 16))
            res.append((qs, (acc / l).astype(o_ref.dtype)))
        for qs, out in res:
            scatter(o_ref, qs, 16, out)
        return carry

    if group == 16:
        loop(DIL_MAX // TILES_16, group16)
    elif group == 4:
        loop(sub // (32 * blocks_4), group4)
    else:
        loop(sub // (16 * blocks_1), group1)


def _dilated_attention(qkv, *, batch, seq, dh):
    n = qkv.shape[0]
    sub = seq // DIL_MAX
    b1, b4, b16_first, b16 = _attn_bias_tables()
    hp = ATTN_HEADS_PER_STEP
    groups = ATTN_HEADS // hp
    kern = functools.partial(_attn_kernel, sub=sub, heads=hp)
    return pl.pallas_call(
        kern,
        grid=(batch, groups),
        in_specs=[
            pl.BlockSpec((seq, hp * dh), lambda b, h: (b, h)),
            pl.BlockSpec((seq, hp * dh), lambda b, h: (b, groups + h)),
            pl.BlockSpec((seq, hp * dh), lambda b, h: (b, 2 * groups + h)),
            pl.BlockSpec((hp,) + b1.shape[1:], lambda b, h: (h, 0, 0, 0)),
            pl.BlockSpec((hp,) + b4.shape[1:], lambda b, h: (h, 0, 0, 0)),
            pl.BlockSpec((hp,) + b16_first.shape[1:], lambda b, h: (h, 0, 0)),
            pl.BlockSpec((hp,) + b16.shape[1:], lambda b, h: (h, 0, 0)),
        ],
        out_specs=pl.BlockSpec((seq, hp * dh), lambda b, h: (b, h)),
        out_shape=jax.ShapeDtypeStruct((n, ATTN_HEADS * dh), BF16),
        scratch_shapes=[pltpu.VMEM((seq, LANES), F32), pltpu.VMEM((seq, LANES), F32),
                        pltpu.VMEM((seq, dh), F32)] * hp,
        compiler_params=_params("parallel", "parallel"),
        name="dilated_attn",
    )(qkv, qkv, qkv, b1, b4, b16_first, b16)


def kernel(x, norm_mix_g, norm_mlp_g, final_norm_g, hyb_w_in, conv_w, gla_w_gate2, gla_b_gate,
           gla_norm_g, hyb_w_out, attn_w_qkv, attn_w_o, mlp_w1, mlp_w2):
    batch, seq, d = x.shape
    n = batch * seq
    depth = norm_mix_g.shape[0]
    conv_ch = conv_w.shape[-1]
    dv = gla_norm_g.shape[-1]
    dk = gla_w_gate2.shape[-1] // GLA_HEADS
    dh = d // ATTN_HEADS
    assert depth == 2 and seq % (DIL_MAX * ATTN_STEPS) == 0 and seq % GLA_CHUNK == 0
    assert dk == LANES and dh == LANES and dv % LANES == 0

    row = lambda v: v.reshape(1, -1).astype(F32)
    xf = x.reshape(n, d)

    main_cols = 3 * conv_ch + 2 * GLA_HEADS * dk + 2 * GLA_HEADS * dv
    w_in = hyb_w_in[0]
    w_low = jnp.pad(w_in[:, main_cols:], ((0, 0), (0, LANES - GLA_GATE_RANK))).astype(BF16)
    w_gate = jnp.pad(gla_w_gate2[0], ((0, LANES - GLA_GATE_RANK), (0, 0))).astype(BF16)
    proj, log_a, (w1_first, w2_first, w_out, w_qkv, w_o) = _in_proj_gla(
        xf, row(norm_mix_g[0]), w_in.astype(BF16), w_low, w_gate, row(gla_b_gate[0]),
        cols=main_cols,
        cast_layers=((mlp_w1, 0), (mlp_w2, 0), (hyb_w_out, 0), (attn_w_qkv, 0), (attn_w_o, 0)))

    q_off = 3 * conv_ch
    k_off = q_off + GLA_HEADS * dk
    v_off = k_off + GLA_HEADS * dk
    r_off = v_off + GLA_HEADS * dv
    tri = jnp.asarray(np.tril(np.ones((GLA_CHUNK, GLA_CHUNK), np.float32)), BF16)
    o_gla = _gla(proj, log_a, tri, row(gla_norm_g[0]), batch=batch, seq=seq,
                 q_off=q_off, k_off=k_off, v_off=v_off, r_off=r_off, dk=dk, dv=dv)
    xf = _conv_out_proj_residual(proj, conv_w[0].astype(F32), o_gla, w_out, xf, seq=seq)
    xf, (w1_next, w2_next) = _mlp_residual(
        xf, row(norm_mlp_g[0]), w1_first, w2_first, row(final_norm_g),
        final_norm=False, cast_layers=((mlp_w1, 1), (mlp_w2, 1)))

    col_scale = jnp.concatenate([jnp.full((1, d), float(dh) ** -0.5 * LOG2_E, F32),
                                 jnp.ones((1, 2 * d), F32)], axis=1)
    qkv = _qkv_proj_residue_major(xf, row(norm_mix_g[1]), w_qkv, col_scale, batch=batch, seq=seq)
    attn = _dilated_attention(qkv, batch=batch, seq=seq, dh=dh)
    xf = _attn_out_proj_residual(attn, w_o, xf, batch=batch, seq=seq)
    out, _ = _mlp_residual(xf, row(norm_mlp_g[1]), w1_next, w2_next, row(final_norm_g),
                           final_norm=True)
    return out.reshape(batch, seq, d)
```

```python
import functools

import jax
import jax.numpy as jnp
import numpy as np
from jax import lax
from jax.experimental import pallas as pl
from jax.experimental.pallas import tpu as pltpu

F32 = jnp.float32
BF16 = jnp.bfloat16

NORM_EPS = 1e-6
CONV_K = 3
GLA_HEADS = 4
GLA_GATE_RANK = 16
GLA_GATE_TEMP = 16.0
ATTN_HEADS = 16
ATTN_HEADS_PER_STEP = 2
DILATED_GROUPS = ((128, 1), (512, 4), (2048, 16))
DIL_MAX = 16
ATTN_STEPS = 128

LANES = 128
F32_SUBLANES = 8
BF16_SUBLANES = 16
GLA_CHUNK = 256
GLA_SUB = 16
GLA_BLOCK = 64
GLA_MAX_BLOCK_DECAY = 60.0
MASK_VALUE = -1e30
LOG2_E = 1.4426950408889634
TILES_16 = 16
BLOCKS_4 = 8
BLOCKS_1 = 16
V7X_VMEM_BYTES = 64 * 1024 * 1024
VMEM_LIMIT_BYTES = 56 * 1024 * 1024


def _params(*semantics, vmem_limit_bytes=VMEM_LIMIT_BYTES):
    return pltpu.CompilerParams(dimension_semantics=semantics, vmem_limit_bytes=vmem_limit_bytes)


def _nbytes(shape, dtype):
    return int(np.prod(shape)) * jnp.dtype(dtype).itemsize


def _dot(a, b):
    return jnp.dot(a, b, preferred_element_type=F32)


def _dot_nt(a, b):
    return lax.dot_general(a, b, (((1,), (1,)), ((), ())), preferred_element_type=F32)


def _dot_tn(a, b):
    return lax.dot_general(a, b, (((0,), (0,)), ((), ())), preferred_element_type=F32)


def _rmsnorm(x, g):
    return x * lax.rsqrt(jnp.mean(x * x, axis=-1, keepdims=True) + NORM_EPS) * g


def _tile_of_trip(i):
    return jnp.maximum(i - 1, 0)


def _chunk_index(i, j, n_tiles, chunks):
    return jnp.minimum(i, n_tiles - 1) * chunks + j


def _in_proj_gla_kernel(xc_ref, g_ref, w_ref, wlow_ref, wgate_ref, bgate_ref, *rest, n_cast):
    cast_in, (o_ref, loga_ref) = rest[:n_cast], rest[n_cast:n_cast + 2]
    cast_out, xn_refs = rest[n_cast + 2:2 * n_cast + 2], rest[2 * n_cast + 2:]
    i, j = pl.program_id(0), pl.program_id(1)
    ch = xc_ref.shape[0]
    rows = pl.ds(pl.multiple_of(j * ch, ch), ch)

    def norm_and_gate_chunk(fill_ref):
        xn = _rmsnorm(xc_ref[...], g_ref[...]).astype(BF16)
        fill_ref[rows, :] = xn
        g_low = _dot(xn, wlow_ref[...]).astype(BF16)
        gate = _dot(g_low, wgate_ref[...]) + bgate_ref[...]
        log_sig = jnp.minimum(gate, 0.0) - jnp.log(1.0 + jnp.exp(-jnp.abs(gate)))
        loga_ref[rows, :] = log_sig * (1.0 / GLA_GATE_TEMP)
        for src, dst in zip(cast_in, cast_out):
            dst[...] = src[...].astype(dst.dtype)

    @pl.when(i == 0)
    def _():
        norm_and_gate_chunk(xn_refs[0])

    for parity in range(2):
        @pl.when((i > 0) & (i % 2 == parity))
        def _():
            norm_and_gate_chunk(xn_refs[parity])
            o_ref[...] = _dot(xn_refs[1 - parity][...], w_ref[...]).astype(o_ref.dtype)


def _in_proj_gla(x, gain, w, w_low, w_gate, b_gate, *, cols, cast_layers=(), tm=1024, tn=1536):
    n, d = x.shape
    n_gate = w_gate.shape[1]
    n_tiles, steps = n // tm, cols // tn
    ch = tm // steps
    cast_in_specs, cast_out_specs, cast_shapes = [], [], []
    for arr, layer in cast_layers:
        rows, width = arr.shape[1:]
        slab = rows // (n_tiles * steps)
        assert rows % (n_tiles * steps) == 0 and slab % BF16_SUBLANES == 0
        last = n_tiles * steps - 1
        cast_in_specs.append(pl.BlockSpec(
            (None, slab, width),
            lambda i, j, layer=layer, last=last: (layer, jnp.minimum(i * steps + j, last), 0)))
        cast_out_specs.append(pl.BlockSpec(
            (slab, width), lambda i, j, last=last: (jnp.minimum(i * steps + j, last), 0)))
        cast_shapes.append(jax.ShapeDtypeStruct((rows, width), BF16))
    kern = functools.partial(_in_proj_gla_kernel, n_cast=len(cast_layers))
    out = pl.pallas_call(
        kern,
        grid=(n_tiles + 1, steps),
        in_specs=[
            pl.BlockSpec((ch, d), lambda i, j: (_chunk_index(i, j, n_tiles, steps), 0)),
            pl.BlockSpec((1, d), lambda i, j: (0, 0)),
            pl.BlockSpec((d, tn), lambda i, j: (0, j)),
            pl.BlockSpec((d, LANES), lambda i, j: (0, 0)),
            pl.BlockSpec((LANES, n_gate), lambda i, j: (0, 0)),
            pl.BlockSpec((1, n_gate), lambda i, j: (0, 0)),
        ] + cast_in_specs,
        out_specs=[
            pl.BlockSpec((tm, tn), lambda i, j: (_tile_of_trip(i), j * jnp.minimum(i, 1))),
            pl.BlockSpec((tm, n_gate), lambda i, j: (jnp.minimum(i, n_tiles - 1), 0)),
        ] + cast_out_specs,
        out_shape=[
            jax.ShapeDtypeStruct((n, cols), BF16),
            jax.ShapeDtypeStruct((n, n_gate), F32),
        ] + cast_shapes,
        scratch_shapes=[pltpu.VMEM((tm, d), BF16), pltpu.VMEM((tm, d), BF16)],
        compiler_params=_params("arbitrary", "arbitrary"),
        name="in_proj_gla",
    )(x, gain, w, w_low, w_gate, b_gate, *[arr for arr, _ in cast_layers])
    return out[0], out[1], out[2:]


def _qkv_proj_kernel(xc_ref, g_ref, w_ref, scale_ref, o_ref, *xn_refs):
    i, j = pl.program_id(0), pl.program_id(1)
    ch, d = xc_ref.shape
    per = ch // DIL_MAX

    def norm_chunk(fill_ref):
        xn = _rmsnorm(xc_ref[...], g_ref[...]).astype(BF16)
        fill_ref[:, pl.ds(pl.multiple_of(j * per, per), per), :] = (
            jnp.swapaxes(xn.reshape(per, DIL_MAX, d), 0, 1))

    @pl.when(i == 0)
    def _():
        norm_chunk(xn_refs[0])

    for parity in range(2):
        @pl.when((i > 0) & (i % 2 == parity))
        def _():
            norm_chunk(xn_refs[parity])
            xn = xn_refs[1 - parity][...]
            acc = _dot(xn.reshape(xn.shape[0] * xn.shape[1], d), w_ref[...]) * scale_ref[...]
            o_ref[...] = acc.astype(o_ref.dtype).reshape(o_ref.shape)


def _qkv_proj_residue_major(x, gain, w, col_scale, *, batch, seq, tm=1024, tn=1536):
    n, d = x.shape
    cols = w.shape[1]
    sub = seq // DIL_MAX
    per = tm // DIL_MAX
    tiles_per_seq = seq // tm
    n_tiles, steps = n // tm, cols // tn
    ch = tm // steps
    assert ch % (DIL_MAX * BF16_SUBLANES) == 0

    def out_index(i, j):
        t = _tile_of_trip(i)
        return (t // tiles_per_seq, 0, t % tiles_per_seq, j * jnp.minimum(i, 1))

    out = pl.pallas_call(
        _qkv_proj_kernel,
        grid=(n_tiles + 1, steps),
        in_specs=[
            pl.BlockSpec((ch, d), lambda i, j: (_chunk_index(i, j, n_tiles, steps), 0)),
            pl.BlockSpec((1, d), lambda i, j: (0, 0)),
            pl.BlockSpec((d, tn), lambda i, j: (0, j)),
            pl.BlockSpec((1, tn), lambda i, j: (0, j)),
        ],
        out_specs=pl.BlockSpec((None, DIL_MAX, per, tn), out_index),
        out_shape=jax.ShapeDtypeStruct((batch, DIL_MAX, sub, cols), BF16),
        scratch_shapes=[pltpu.VMEM((DIL_MAX, per, d), BF16), pltpu.VMEM((DIL_MAX, per, d), BF16)],
        compiler_params=_params("arbitrary", "arbitrary"),
        name="qkv_proj",
    )(x, gain, w, col_scale)
    return out.reshape(n, cols)


def _conv_out_proj_kernel(ab_ref, ac_ref, ax_ref, cw_ref, gla_ref, wa_ref, wb_ref, x_ref, o_ref,
                          ubuf_ref, *, tiles_per_seq):
    tb = ab_ref.shape[0]
    pad = F32_SUBLANES

    @pl.when(pl.program_id(0) % tiles_per_seq == 0)
    def _():
        ubuf_ref[0:pad, :] = jnp.zeros((pad, ubuf_ref.shape[1]), F32)

    acc = x_ref[...] + _dot(gla_ref[...], wb_ref[...])
    u = ac_ref[...].astype(F32) * ax_ref[...].astype(F32)
    ubuf_ref[pad:pad + tb, :] = u
    u1 = ubuf_ref[pad - 1:pad - 1 + tb, :]
    u2 = ubuf_ref[pad - 2:pad - 2 + tb, :]
    cw = cw_ref[...]
    y = ab_ref[...].astype(F32) * (cw[0:1, :] * u2 + cw[1:2, :] * u1 + cw[2:3, :] * u)
    o_ref[...] = acc + _dot(y.astype(BF16), wa_ref[...])
    ubuf_ref[0:pad, :] = ubuf_ref[tb:tb + pad, :]


def _conv_out_proj_residual(proj, conv_w, o_gla, w_out, x, *, seq, tm=512):
    n, d = x.shape
    ch = conv_w.shape[1]
    assert w_out.shape[0] == 2 * ch and o_gla.shape[1] == ch
    kern = functools.partial(_conv_out_proj_kernel, tiles_per_seq=seq // tm)
    return pl.pallas_call(
        kern,
        grid=(n // tm,),
        in_specs=[
            pl.BlockSpec((tm, ch), lambda i: (i, 0)),
            pl.BlockSpec((tm, ch), lambda i: (i, 1)),
            pl.BlockSpec((tm, ch), lambda i: (i, 2)),
            pl.BlockSpec((CONV_K, ch), lambda i: (0, 0)),
            pl.BlockSpec((tm, o_gla.shape[1]), lambda i: (i, 0)),
            pl.BlockSpec((ch, d), lambda i: (0, 0)),
            pl.BlockSpec((ch, d), lambda i: (1, 0)),
            pl.BlockSpec((tm, d), lambda i: (i, 0)),
        ],
        out_specs=pl.BlockSpec((tm, d), lambda i: (i, 0)),
        out_shape=jax.ShapeDtypeStruct((n, d), F32),
        scratch_shapes=[pltpu.VMEM((tm + F32_SUBLANES, ch), F32)],
        compiler_params=_params("arbitrary"),
        name="conv_out_proj",
    )(proj, proj, proj, conv_w, o_gla, w_out, w_out, x)


def _gla_scores_factored(q, k, g):
    c_len = q.shape[0]
    score_rows = []
    for a in range(c_len // GLA_BLOCK):
        lo, hi = a * GLA_BLOCK, (a + 1) * GLA_BLOCK
        n_keys = LANES * -(-hi // LANES)
        g_first = g[lo:lo + 1, :]
        q_a = (q[lo:hi] * jnp.exp(g[lo:hi] - g_first)).astype(BF16)
        k_a = (k[:n_keys] * jnp.exp(jnp.minimum(g_first - g[:n_keys], GLA_MAX_BLOCK_DECAY)))
        s_a = _dot_nt(q_a, k_a.astype(BF16))
        row = lax.broadcasted_iota(jnp.int32, s_a.shape, 0) + lo
        col = lax.broadcasted_iota(jnp.int32, s_a.shape, 1)
        s_a = jnp.where(col <= row, s_a, 0.0)
        if n_keys < c_len:
            s_a = jnp.concatenate([s_a, jnp.zeros((GLA_BLOCK, c_len - n_keys), F32)], axis=1)
        score_rows.append(s_a)
    return jnp.concatenate(score_rows, axis=0)


def _gla_scores_pairwise(q, k, g):
    c_len = q.shape[0]
    n_sub = c_len // GLA_SUB
    score_rows = [jnp.zeros((GLA_SUB, c_len), F32)]
    for a in range(1, n_sub):
        lo = a * GLA_SUB
        n_keys = LANES * -(-lo // LANES)
        g_ref_row = g[lo:lo + 1, :]
        q_a = (q[lo:lo + GLA_SUB] * jnp.exp(g[lo:lo + GLA_SUB] - g_ref_row)).astype(BF16)
        k_a = (k[:n_keys] * jnp.exp(jnp.minimum(g_ref_row - g[:n_keys], 0.0))).astype(BF16)
        s_a = _dot_nt(q_a, k_a)
        col = lax.broadcasted_iota(jnp.int32, s_a.shape, 1)
        s_a = jnp.where(col < lo, s_a, 0.0)
        if n_keys < c_len:
            s_a = jnp.concatenate([s_a, jnp.zeros((GLA_SUB, c_len - n_keys), F32)], axis=1)
        score_rows.append(s_a)
    scores = jnp.concatenate(score_rows, axis=0)

    row = lax.broadcasted_iota(jnp.int32, (c_len, LANES), 0)
    lane = lax.broadcasted_iota(jnp.int32, (c_len, LANES), 1)
    delta = row % LANES - lane
    band_id = jnp.where((delta >= 0) & (delta <= row % GLA_SUB), delta, -1)
    band = jnp.zeros((c_len, LANES), F32)
    for d in range(GLA_SUB):
        k_d = k if d == 0 else pltpu.roll(k, d, 0)
        g_d = g if d == 0 else pltpu.roll(g, d, 0)
        e = jnp.exp(jnp.minimum(g - g_d, 0.0))
        diag = jnp.sum(q * k_d * e, axis=-1, keepdims=True)
        band = jnp.where(band_id == d, diag, band)
    zeros = jnp.zeros((LANES, LANES), F32)
    band_rows = []
    for t in range(c_len // LANES):
        tiles = [zeros] * (c_len // LANES)
        tiles[t] = band[t * LANES:(t + 1) * LANES]
        band_rows.append(jnp.concatenate(tiles, axis=1))
    return scores + jnp.concatenate(band_rows, axis=0)


def _gla_kernel(q_ref, k_ref, v_ref, r_ref, la_ref, tri_ref, gn_ref, o_ref, st_ref, g_ref,
                inter_ref, *, q_scale, dk, dv):
    c_len = q_ref.shape[0]

    @pl.when(pl.program_id(1) == 0)
    def _():
        st_ref[...] = jnp.zeros_like(st_ref)

    la = la_ref[...]
    tri = tri_ref[...]
    la_hi = la.astype(BF16)
    rem = la - la_hi.astype(F32)
    la_mid = rem.astype(BF16)
    la_lo = (rem - la_mid.astype(F32)).astype(BF16)
    g_all = _dot(tri, la_hi) + _dot(tri, la_mid) + _dot(tri, la_lo)
    g_ref[...] = g_all
    block_decay = [g_all[lo:lo + 1, :] - g_all[lo + GLA_BLOCK - 1:lo + GLA_BLOCK, :]
                   for lo in range(0, c_len, GLA_BLOCK)]
    worst_decay = jnp.max(jnp.concatenate(block_decay, axis=0))

    def head_operands(h):
        g = g_ref[:, h * dk:(h + 1) * dk]
        q = q_ref[:, h * dk:(h + 1) * dk].astype(F32) * q_scale
        k = k_ref[:, h * dk:(h + 1) * dk].astype(F32)
        return g, q, k, v_ref[:, h * dv:(h + 1) * dv]

    def finish_head(h, o):
        o = _rmsnorm(o, gn_ref[...])
        r = r_ref[:, h * dv:(h + 1) * dv].astype(F32)
        o_ref[:, h * dv:(h + 1) * dv] = (o * (r / (1.0 + jnp.exp(-r)))).astype(o_ref.dtype)

    for h in range(GLA_HEADS):
        g, q, k, v = head_operands(h)
        g_last = g[c_len - 1:c_len, :]
        st = st_ref[h]
        o_inter = _dot_nt((q * jnp.exp(g)).astype(BF16), st.astype(BF16))
        inter_ref[h] = o_inter
        k_dec = (k * jnp.exp(g_last - g)).astype(BF16)
        st_ref[h] = st * jnp.exp(g_last) + _dot_tn(v, k_dec)
        finish_head(h, o_inter + _dot(_gla_scores_factored(q, k, g).astype(BF16), v))

    @pl.when(worst_decay > GLA_MAX_BLOCK_DECAY)
    def _():
        for h in range(GLA_HEADS):
            g, q, k, v = head_operands(h)
            finish_head(h, inter_ref[h] + _dot(_gla_scores_pairwise(q, k, g).astype(BF16), v))


def _gla(proj, log_a, tri, norm_g, *, batch, seq, q_off, k_off, v_off, r_off, dk, dv):
    n = proj.shape[0]
    cpb = seq // GLA_CHUNK
    qk_w, vr_w = GLA_HEADS * dk, GLA_HEADS * dv
    row = lambda b, c: b * cpb + c
    kern = functools.partial(_gla_kernel, q_scale=float(dk) ** -0.5, dk=dk, dv=dv)
    return pl.pallas_call(
        kern,
        grid=(batch, cpb),
        in_specs=[
            pl.BlockSpec((GLA_CHUNK, qk_w), lambda b, c: (row(b, c), q_off // qk_w)),
            pl.BlockSpec((GLA_CHUNK, qk_w), lambda b, c: (row(b, c), k_off // qk_w)),
            pl.BlockSpec((GLA_CHUNK, vr_w), lambda b, c: (row(b, c), v_off // vr_w)),
            pl.BlockSpec((GLA_CHUNK, vr_w), lambda b, c: (row(b, c), r_off // vr_w)),
            pl.BlockSpec((GLA_CHUNK, qk_w), lambda b, c: (row(b, c), 0)),
            pl.BlockSpec((GLA_CHUNK, GLA_CHUNK), lambda b, c: (0, 0)),
            pl.BlockSpec((1, dv), lambda b, c: (0, 0)),
        ],
        out_specs=pl.BlockSpec((GLA_CHUNK, vr_w), lambda b, c: (row(b, c), 0)),
        out_shape=jax.ShapeDtypeStruct((n, vr_w), BF16),
        scratch_shapes=[pltpu.VMEM((GLA_HEADS, dv, dk), F32), pltpu.VMEM((GLA_CHUNK, qk_w), F32),
                        pltpu.VMEM((GLA_HEADS, GLA_CHUNK, dv), F32)],
        compiler_params=_params("parallel", "arbitrary"),
        name="gla",
    )(proj, proj, proj, proj, log_a, tri, norm_g)


def _attn_out_proj_kernel(a_ref, w_ref, x_ref, o_ref):
    tm, d = x_ref.shape
    acc = _dot(a_ref[...].reshape(tm, a_ref.shape[-1]), w_ref[...])
    acc = jnp.swapaxes(acc.reshape(DIL_MAX, tm // DIL_MAX, d), 0, 1).reshape(tm, d)
    o_ref[...] = x_ref[...] + acc


def _attn_out_proj_residual(attn, w, x, *, batch, seq, tm=512):
    n, d = x.shape
    per = tm // DIL_MAX
    tps = seq // tm
    attn = attn.reshape(batch, DIL_MAX, seq // DIL_MAX, attn.shape[1])
    return pl.pallas_call(
        _attn_out_proj_kernel,
        grid=(n // tm,),
        in_specs=[
            pl.BlockSpec((None, DIL_MAX, per, attn.shape[-1]), lambda i: (i // tps, 0, i % tps, 0)),
            pl.BlockSpec(w.shape, lambda i: (0, 0)),
            pl.BlockSpec((tm, d), lambda i: (i, 0)),
        ],
        out_specs=pl.BlockSpec((tm, d), lambda i: (i, 0)),
        out_shape=jax.ShapeDtypeStruct((n, d), F32),
        compiler_params=_params("parallel"),
        name="attn_out_proj",
    )(attn, w, x)


def _mlp_kernel(x_ref, g_ref, w1_ref, w2_ref, gf_ref, *rest, final_norm, n_cast):
    cast_in, o_ref = rest[:n_cast], rest[n_cast]
    cast_out, xn_ref = rest[n_cast + 1:2 * n_cast + 1], rest[2 * n_cast + 1]
    f = pl.program_id(1)

    def hidden_times_w2():
        for src, dst in zip(cast_in, cast_out):
            dst[...] = src[...].astype(dst.dtype)
        h = _dot(xn_ref[...], w1_ref[...])
        return _dot(jnp.square(jnp.maximum(h, 0.0)).astype(BF16), w2_ref[...])

    @pl.when(f == 0)
    def _():
        xn_ref[...] = _rmsnorm(x_ref[...], g_ref[...]).astype(BF16)
        o_ref[...] = x_ref[...] + hidden_times_w2()

    last = pl.num_programs(1) - 1
    if not final_norm:
        @pl.when(f > 0)
        def _():
            o_ref[...] += hidden_times_w2()
    else:
        @pl.when((f > 0) & (f < last))
        def _():
            o_ref[...] += hidden_times_w2()

        @pl.when(f == last)
        def _():
            o_ref[...] = _rmsnorm(o_ref[...] + hidden_times_w2(), gf_ref[...])


def _mlp_residual(x, gain, w1, w2, final_gain, *, final_norm, cast_layers=(), tm=512, tf=2048):
    n, d = x.shape
    d_ff = w1.shape[1]
    assert d_ff // tf >= 2
    steps = (n // tm) * (d_ff // tf)
    kern = functools.partial(_mlp_kernel, final_norm=final_norm, n_cast=len(cast_layers))
    cast_in_specs, cast_out_specs, cast_shapes = [], [], []
    for arr, layer in cast_layers:
        rows, cols = arr.shape[1:]
        slab = rows // steps
        assert rows % steps == 0 and slab % BF16_SUBLANES == 0
        cast_in_specs.append(pl.BlockSpec(
            (None, slab, cols), lambda i, f, layer=layer: (layer, i * (d_ff // tf) + f, 0)))
        cast_out_specs.append(pl.BlockSpec((slab, cols), lambda i, f: (i * (d_ff // tf) + f, 0)))
        cast_shapes.append(jax.ShapeDtypeStruct((rows, cols), BF16))
    vmem = (2 * (2 * _nbytes((tm, d), F32) + 2 * _nbytes((d, tf), BF16)) + _nbytes((tm, d), BF16)
            + _nbytes((tm, tf), F32) + _nbytes((tm, tf), BF16))
    vmem += sum(2 * (_nbytes(s.block_shape[1:], F32) + _nbytes(s.block_shape[1:], BF16))
                for s in cast_in_specs)
    vmem_limit = min(vmem + (4 << 20), V7X_VMEM_BYTES)
    out = pl.pallas_call(
        kern,
        grid=(n // tm, d_ff // tf),
        in_specs=[
            pl.BlockSpec((tm, d), lambda i, f: (i, 0)),
            pl.BlockSpec((1, d), lambda i, f: (0, 0)),
            pl.BlockSpec((d, tf), lambda i, f: (0, f)),
            pl.BlockSpec((tf, d), lambda i, f: (f, 0)),
            pl.BlockSpec((1, d), lambda i, f: (0, 0)),
        ] + cast_in_specs,
        out_specs=[pl.BlockSpec((tm, d), lambda i, f: (i, 0))] + cast_out_specs,
        out_shape=[jax.ShapeDtypeStruct((n, d), F32)] + cast_shapes,
        scratch_shapes=[pltpu.VMEM((tm, d), BF16)],
        compiler_params=_params("arbitrary", "arbitrary", vmem_limit_bytes=vmem_limit),
        name="mlp",
    )(x, gain, w1, w2, final_gain, *[arr for arr, _ in cast_layers])
    return out[0], out[1:]


def _attn_bias_tables():
    slopes = np.exp2(-8.0 * np.arange(1, ATTN_HEADS + 1) / ATTN_HEADS)

    def table(q_pos, k_pos, dilation):
        diff = q_pos[:, None] - k_pos[None, :]
        valid = (diff >= 0) & (diff <= ATTN_STEPS)
        bias = -(slopes * LOG2_E)[:, None, None] * (dilation * diff)[None]
        return np.where(valid[None], bias, MASK_VALUE).astype(np.float32)

    q = np.arange(128)
    b16_first = table(q, q, 16)
    b16 = table(q + 128, np.arange(256), 16)

    q4 = 4 * (q % 32) + q // 32
    k = np.arange(256)
    k4 = 4 * (k % 64) + k // 64
    b4 = np.stack([table(q4, k4, 4), table(q4 + 128, k4, 4)], axis=1)

    q = np.arange(256)
    q1 = 16 * (q % 16) + q // 16
    k = np.arange(384)
    k1 = 16 * (k % 24) + k // 24
    b1 = np.stack([table(q1, k1, 1), table(q1 + 128, k1, 1)], axis=1)
    return tuple(jnp.asarray(t) for t in (b1, b4, b16_first, b16))


def _attn_kernel(q_all, k_all, v_all, b1_all, b4_all, b16f_all, b16_all, o_all, *scratch, sub,
                 heads):
    def gather(ref, starts, size):
        return jnp.concatenate([ref[pl.ds(s, size), :] for s in starts], axis=0)

    def scatter(ref, starts, size, val):
        for c, s in enumerate(starts):
            ref[pl.ds(s, size), :] = val[c * size:(c + 1) * size]

    def lanes(x):
        return jnp.broadcast_to(x, (x.shape[0], LANES))

    def across_keys(x, n_keys):
        return jnp.concatenate([x] * (n_keys // LANES), axis=1)

    def weighted_values_and_sum(p, v):
        pv = _dot(p.astype(BF16), jnp.concatenate([v, jnp.ones_like(v)], axis=1))
        return pv[:, :LANES], pv[:, LANES:]

    def first_tile(q, k, v, bias):
        s = _dot_nt(q, k) + bias
        m = jnp.max(s, axis=-1, keepdims=True)
        acc, l = weighted_values_and_sum(jnp.exp2(s - m), v)
        return lanes(m), l, acc

    def next_tile(q, k, v, bias, m_old, l_old, acc_old):
        s = _dot_nt(q, k) + bias
        m_new = jnp.maximum(m_old, lanes(jnp.max(s, axis=-1, keepdims=True)))
        alpha = jnp.exp2(m_old - m_new)
        acc, l = weighted_values_and_sum(jnp.exp2(s - across_keys(m_new, s.shape[1])), v)
        return m_new, alpha * l_old + l, alpha * acc_old + acc

    dh = q_all.shape[1] // heads

    def loop(trips, body):
        if trips == 1:
            body(0, 0)
        else:
            lax.fori_loop(0, trips, body, 0)

    blocks_4 = min(BLOCKS_4, sub // 32)
    blocks_1 = min(BLOCKS_1, sub // 16)
    for group in (16, 4, 1):
        for h in range(heads):
            cols = slice(h * dh, (h + 1) * dh)
            head_refs = (q_all.at[:, cols], k_all.at[:, cols], v_all.at[:, cols], o_all.at[:, cols],
                         b1_all.at[h], b4_all.at[h], b16f_all.at[h], b16_all.at[h],
                         *scratch[3 * h:3 * h + 3])
            _attn_group(group, head_refs, sub, blocks_4, blocks_1, loop, first_tile, next_tile,
                        gather, scatter)


def _attn_group(group, head_refs, sub, blocks_4, blocks_1, loop, first_tile, next_tile, gather,
                scatter):
    (q_ref, k_ref, v_ref, o_ref, b1_ref, b4_ref, b16f_ref, b16_ref, m_ref, l_ref, acc_ref) = head_refs

    def group16(i, carry):
        bias_first, bias = b16f_ref[...], b16_ref[...]
        rows, res = [], []
        for u in range(TILES_16):
            base = (TILES_16 * i + u) * sub
            for j in range(sub // ATTN_STEPS):
                q_rows = pl.ds(pl.multiple_of(base + j * ATTN_STEPS, ATTN_STEPS), ATTN_STEPS)
                if j == 0:
                    k_rows, b = q_rows, bias_first
                else:
                    k_rows = pl.ds(pl.multiple_of(base + (j - 1) * ATTN_STEPS, ATTN_STEPS),
                                   2 * ATTN_STEPS)
                    b = bias
                rows.append(q_rows)
                res.append(first_tile(q_ref[q_rows, :], k_ref[k_rows, :], v_ref[k_rows, :], b))
        for r, (m, l, acc) in zip(rows, res):
            m_ref[r, :] = m
            l_ref[r, :] = l
            acc_ref[r, :] = acc
        return carry

    def group4(i, carry):
        starts, res = [], []
        for u in range(blocks_4):
            blk = blocks_4 * i + u
            bias = b4_ref[jnp.where(blk == 0, 0, 1)]
            k_blk = jnp.maximum(blk - 1, 0)
            for r4 in range(4):
                qs = [pl.multiple_of((r4 + 4 * c) * sub + 32 * blk, 32) for c in range(4)]
                ks = [pl.multiple_of((r4 + 4 * c) * sub + 32 * k_blk, 32) for c in range(4)]
                starts.append(qs)
                res.append(next_tile(
                    gather(q_ref, qs, 32), gather(k_ref, ks, 64), gather(v_ref, ks, 64), bias,
                    gather(m_ref, qs, 32), gather(l_ref, qs, 32), gather(acc_ref, qs, 32)))
        for qs, (m, l, acc) in zip(starts, res):
            scatter(m_ref, qs, 32, m)
            scatter(l_ref, qs, 32, l)
            scatter(acc_ref, qs, 32, acc)
        return carry

    def group1(i, carry):
        res = []
        for u in range(blocks_1):
            blk = blocks_1 * i + u
            bias = b1_ref[jnp.where(blk == 0, 0, 1)]
            k_lo = jnp.maximum(16 * blk - 8, 0)
            qs = [pl.multiple_of(r * sub + 16 * blk, 16) for r in range(DIL_MAX)]
            ks = [pl.multiple_of(r * sub + k_lo, 8) for r in range(DIL_MAX)]
            _, l, acc = next_tile(
                gather(q_ref, qs, 16), gather(k_ref, ks, 24), gather(v_ref, ks, 24), bias,
                gather(m_ref, qs, 16), gather(l_ref, qs, 16), gather(acc_ref, qs, 16))
            res.append((qs, (acc / l).astype(o_ref.dtype)))
        for qs, out in res:
            scatter(o_ref, qs, 16, out)
        return carry

    if group == 16:
        loop(DIL_MAX // TILES_16, group16)
    elif group == 4:
        loop(sub // (32 * blocks_4), group4)
    else:
        loop(sub // (16 * blocks_1), group1)


def _dilated_attention(qkv, *, batch, seq, dh):
    n = qkv.shape[0]
    sub = seq // DIL_MAX
    b1, b4, b16_first, b16 = _attn_bias_tables()
    hp = ATTN_HEADS_PER_STEP
    groups = ATTN_HEADS // hp
    kern = functools.partial(_attn_kernel, sub=sub, heads=hp)
    return pl.pallas_call(
        kern,
        grid=(batch, groups),
        in_specs=[
            pl.BlockSpec((seq, hp * dh), lambda b, h: (b, h)),
            pl.BlockSpec((seq, hp * dh), lambda b, h: (b, groups + h)),
            pl.BlockSpec((seq, hp * dh), lambda b, h: (b, 2 * groups + h)),
            pl.BlockSpec((hp,) + b1.shape[1:], lambda b, h: (h, 0, 0, 0)),
            pl.BlockSpec((hp,) + b4.shape[1:], lambda b, h: (h, 0, 0, 0)),
            pl.BlockSpec((hp,) + b16_first.shape[1:], lambda b, h: (h, 0, 0)),
            pl.BlockSpec((hp,) + b16.shape[1:], lambda b, h: (h, 0, 0)),
        ],
        out_specs=pl.BlockSpec((seq, hp * dh), lambda b, h: (b, h)),
        out_shape=jax.ShapeDtypeStruct((n, ATTN_HEADS * dh), BF16),
        scratch_shapes=[pltpu.VMEM((seq, LANES), F32), pltpu.VMEM((seq, LANES), F32),
                        pltpu.VMEM((seq, dh), F32)] * hp,
        compiler_params=_params("parallel", "parallel"),
        name="dilated_attn",
    )(qkv, qkv, qkv, b1, b4, b16_first, b16)


def kernel(x, norm_mix_g, norm_mlp_g, final_norm_g, hyb_w_in, conv_w, gla_w_gate2, gla_b_gate,
           gla_norm_g, hyb_w_out, attn_w_qkv, attn_w_o, mlp_w1, mlp_w2):
    batch, seq, d = x.shape
    n = batch * seq
    depth = norm_mix_g.shape[0]
    conv_ch = conv_w.shape[-1]
    dv = gla_norm_g.shape[-1]
    dk = gla_w_gate2.shape[-1] // GLA_HEADS
    dh = d // ATTN_HEADS
    assert depth == 2 and seq % (DIL_MAX * ATTN_STEPS) == 0 and seq % GLA_CHUNK == 0
    assert dk == LANES and dh == LANES and dv % LANES == 0

    row = lambda v: v.reshape(1, -1).astype(F32)
    xf = x.reshape(n, d)

    main_cols = 3 * conv_ch + 2 * GLA_HEADS * dk + 2 * GLA_HEADS * dv
    w_in = hyb_w_in[0]
    w_low = jnp.pad(w_in[:, main_cols:], ((0, 0), (0, LANES - GLA_GATE_RANK))).astype(BF16)
    w_gate = jnp.pad(gla_w_gate2[0], ((0, LANES - GLA_GATE_RANK), (0, 0))).astype(BF16)
    proj, log_a, (w1_first, w2_first, w_out, w_qkv, w_o) = _in_proj_gla(
        xf, row(norm_mix_g[0]), w_in.astype(BF16), w_low, w_gate, row(gla_b_gate[0]),
        cols=main_cols,
        cast_layers=((mlp_w1, 0), (mlp_w2, 0), (hyb_w_out, 0), (attn_w_qkv, 0), (attn_w_o, 0)))

    q_off = 3 * conv_ch
    k_off = q_off + GLA_HEADS * dk
    v_off = k_off + GLA_HEADS * dk
    r_off = v_off + GLA_HEADS * dv
    tri = jnp.asarray(np.tril(np.ones((GLA_CHUNK, GLA_CHUNK), np.float32)), BF16)
    o_gla = _gla(proj, log_a, tri, row(gla_norm_g[0]), batch=batch, seq=seq,
                 q_off=q_off, k_off=k_off, v_off=v_off, r_off=r_off, dk=dk, dv=dv)
    xf = _conv_out_proj_residual(proj, conv_w[0].astype(F32), o_gla, w_out, xf, seq=seq)
    xf, (w1_next, w2_next) = _mlp_residual(
        xf, row(norm_mlp_g[0]), w1_first, w2_first, row(final_norm_g),
        final_norm=False, cast_layers=((mlp_w1, 1), (mlp_w2, 1)))

    col_scale = jnp.concatenate([jnp.full((1, d), float(dh) ** -0.5 * LOG2_E, F32),
                                 jnp.ones((1, 2 * d), F32)], axis=1)
    qkv = _qkv_proj_residue_major(xf, row(norm_mix_g[1]), w_qkv, col_scale, batch=batch, seq=seq)
    attn = _dilated_attention(qkv, batch=batch, seq=seq, dh=dh)
    xf = _attn_out_proj_residual(attn, w_o, xf, batch=batch, seq=seq)
    out, _ = _mlp_residual(xf, row(norm_mlp_g[1]), w1_next, w2_next, row(final_norm_g),
                           final_norm=True)
    return out.reshape(batch, seq, d)
```

```python
import functools

import jax
import jax.numpy as jnp
import numpy as np
from jax import lax
from jax.experimental import pallas as pl
from jax.experimental.pallas import tpu as pltpu

F32 = jnp.float32
BF16 = jnp.bfloat16

NORM_EPS = 1e-6
CONV_K = 3
GLA_HEADS = 4
GLA_GATE_RANK = 16
GLA_GATE_TEMP = 16.0
ATTN_HEADS = 16
ATTN_HEADS_PER_STEP = 2
DILATED_GROUPS = ((128, 1), (512, 4), (2048, 16))
DIL_MAX = 16
ATTN_STEPS = 128

LANES = 128
F32_SUBLANES = 8
BF16_SUBLANES = 16
GLA_CHUNK = 256
GLA_SUB = 16
GLA_BLOCK = 64
GLA_MAX_BLOCK_DECAY = 60.0
MASK_VALUE = -1e30
LOG2_E = 1.4426950408889634
TILES_16 = 16
BLOCKS_4 = 8
BLOCKS_1 = 16
V7X_VMEM_BYTES = 64 * 1024 * 1024
VMEM_LIMIT_BYTES = 56 * 1024 * 1024


def _params(*semantics, vmem_limit_bytes=VMEM_LIMIT_BYTES):
    return pltpu.CompilerParams(dimension_semantics=semantics, vmem_limit_bytes=vmem_limit_bytes)


def _nbytes(shape, dtype):
    return int(np.prod(shape)) * jnp.dtype(dtype).itemsize


def _dot(a, b):
    return jnp.dot(a, b, preferred_element_type=F32)


def _dot_nt(a, b):
    return lax.dot_general(a, b, (((1,), (1,)), ((), ())), preferred_element_type=F32)


def _dot_tn(a, b):
    return lax.dot_general(a, b, (((0,), (0,)), ((), ())), preferred_element_type=F32)


def _rmsnorm(x, g):
    return x * lax.rsqrt(jnp.mean(x * x, axis=-1, keepdims=True) + NORM_EPS) * g


def _tile_of_trip(i):
    return jnp.maximum(i - 1, 0)


def _chunk_index(i, j, n_tiles, chunks):
    return jnp.minimum(i, n_tiles - 1) * chunks + j


def _in_proj_gla_kernel(xc_ref, g_ref, w_ref, wlow_ref, wgate_ref, bgate_ref, *rest, n_cast):
    cast_in, (o_ref, loga_ref) = rest[:n_cast], rest[n_cast:n_cast + 2]
    cast_out, xn_refs = rest[n_cast + 2:2 * n_cast + 2], rest[2 * n_cast + 2:]
    i, j = pl.program_id(0), pl.program_id(1)
    ch = xc_ref.shape[0]
    rows = pl.ds(pl.multiple_of(j * ch, ch), ch)

    def norm_and_gate_chunk(fill_ref):
        xn = _rmsnorm(xc_ref[...], g_ref[...]).astype(BF16)
        fill_ref[rows, :] = xn
        g_low = _dot(xn, wlow_ref[...]).astype(BF16)
        gate = _dot(g_low, wgate_ref[...]) + bgate_ref[...]
        log_sig = jnp.minimum(gate, 0.0) - jnp.log(1.0 + jnp.exp(-jnp.abs(gate)))
        loga_ref[rows, :] = log_sig * (1.0 / GLA_GATE_TEMP)
        for src, dst in zip(cast_in, cast_out):
            dst[...] = src[...].astype(dst.dtype)

    @pl.when(i == 0)
    def _():
        norm_and_gate_chunk(xn_refs[0])

    for parity in range(2):
        @pl.when((i > 0) & (i % 2 == parity))
        def _():
            norm_and_gate_chunk(xn_refs[parity])
            o_ref[...] = _dot(xn_refs[1 - parity][...], w_ref[...]).astype(o_ref.dtype)


def _in_proj_gla(x, gain, w, w_low, w_gate, b_gate, *, cols, cast_layers=(), tm=1024, tn=1536):
    n, d = x.shape
    n_gate = w_gate.shape[1]
    n_tiles, steps = n // tm, cols // tn
    ch = tm // steps
    cast_in_specs, cast_out_specs, cast_shapes = [], [], []
    for arr, layer in cast_layers:
        rows, width = arr.shape[1:]
        slab = rows // (n_tiles * steps)
        assert rows % (n_tiles * steps) == 0 and slab % BF16_SUBLANES == 0
        last = n_tiles * steps - 1
        cast_in_specs.append(pl.BlockSpec(
            (None, slab, width),
            lambda i, j, layer=layer, last=last: (layer, jnp.minimum(i * steps + j, last), 0)))
        cast_out_specs.append(pl.BlockSpec(
            (slab, width), lambda i, j, last=last: (jnp.minimum(i * steps + j, last), 0)))
        cast_shapes.append(jax.ShapeDtypeStruct((rows, width), BF16))
    kern = functools.partial(_in_proj_gla_kernel, n_cast=len(cast_layers))
    out = pl.pallas_call(
        kern,
        grid=(n_tiles + 1, steps),
        in_specs=[
            pl.BlockSpec((ch, d), lambda i, j: (_chunk_index(i, j, n_tiles, steps), 0)),
            pl.BlockSpec((1, d), lambda i, j: (0, 0)),
            pl.BlockSpec((d, tn), lambda i, j: (0, j)),
            pl.BlockSpec((d, LANES), lambda i, j: (0, 0)),
            pl.BlockSpec((LANES, n_gate), lambda i, j: (0, 0)),
            pl.BlockSpec((1, n_gate), lambda i, j: (0, 0)),
        ] + cast_in_specs,
        out_specs=[
            pl.BlockSpec((tm, tn), lambda i, j: (_tile_of_trip(i), j * jnp.minimum(i, 1))),
            pl.BlockSpec((tm, n_gate), lambda i, j: (jnp.minimum(i, n_tiles - 1), 0)),
        ] + cast_out_specs,
        out_shape=[
            jax.ShapeDtypeStruct((n, cols), BF16),
            jax.ShapeDtypeStruct((n, n_gate), F32),
        ] + cast_shapes,
        scratch_shapes=[pltpu.VMEM((tm, d), BF16), pltpu.VMEM((tm, d), BF16)],
        compiler_params=_params("arbitrary", "arbitrary"),
        name="in_proj_gla",
    )(x, gain, w, w_low, w_gate, b_gate, *[arr for arr, _ in cast_layers])
    return out[0], out[1], out[2:]


def _qkv_proj_kernel(xc_ref, g_ref, w_ref, scale_ref, o_ref, *xn_refs):
    i, j = pl.program_id(0), pl.program_id(1)
    ch, d = xc_ref.shape
    per = ch // DIL_MAX

    def norm_chunk(fill_ref):
        xn = _rmsnorm(xc_ref[...], g_ref[...]).astype(BF16)
        fill_ref[:, pl.ds(pl.multiple_of(j * per, per), per), :] = (
            jnp.swapaxes(xn.reshape(per, DIL_MAX, d), 0, 1))

    @pl.when(i == 0)
    def _():
        norm_chunk(xn_refs[0])

    for parity in range(2):
        @pl.when((i > 0) & (i % 2 == parity))
        def _():
            norm_chunk(xn_refs[parity])
            xn = xn_refs[1 - parity][...]
            acc = _dot(xn.reshape(xn.shape[0] * xn.shape[1], d), w_ref[...]) * scale_ref[...]
            o_ref[...] = acc.astype(o_ref.dtype).reshape(o_ref.shape)


def _qkv_proj_residue_major(x, gain, w, col_scale, *, batch, seq, tm=1024, tn=1536):
    n, d = x.shape
    cols = w.shape[1]
    sub = seq // DIL_MAX
    per = tm // DIL_MAX
    tiles_per_seq = seq // tm
    n_tiles, steps = n // tm, cols // tn
    ch = tm // steps
    assert ch % (DIL_MAX * BF16_SUBLANES) == 0

    def out_index(i, j):
        t = _tile_of_trip(i)
        return (t // tiles_per_seq, 0, t % tiles_per_seq, j * jnp.minimum(i, 1))

    out = pl.pallas_call(
        _qkv_proj_kernel,
        grid=(n_tiles + 1, steps),
        in_specs=[
            pl.BlockSpec((ch, d), lambda i, j: (_chunk_index(i, j, n_tiles, steps), 0)),
            pl.BlockSpec((1, d), lambda i, j: (0, 0)),
            pl.BlockSpec((d, tn), lambda i, j: (0, j)),
            pl.BlockSpec((1, tn), lambda i, j: (0, j)),
        ],
        out_specs=pl.BlockSpec((None, DIL_MAX, per, tn), out_index),
        out_shape=jax.ShapeDtypeStruct((batch, DIL_MAX, sub, cols), BF16),
        scratch_shapes=[pltpu.VMEM((DIL_MAX, per, d), BF16), pltpu.VMEM((DIL_MAX, per, d), BF16)],
        compiler_params=_params("arbitrary", "arbitrary"),
        name="qkv_proj",
    )(x, gain, w, col_scale)
    return out.reshape(n, cols)


def _conv_out_proj_kernel(ab_ref, ac_ref, ax_ref, cw_ref, gla_ref, wa_ref, wb_ref, x_ref, o_ref,
                          ubuf_ref, *, tiles_per_seq):
    tb = ab_ref.shape[0]
    pad = F32_SUBLANES

    @pl.when(pl.program_id(0) % tiles_per_seq == 0)
    def _():
        ubuf_ref[0:pad, :] = jnp.zeros((pad, ubuf_ref.shape[1]), F32)

    acc = x_ref[...] + _dot(gla_ref[...], wb_ref[...])
    u = ac_ref[...].astype(F32) * ax_ref[...].astype(F32)
    ubuf_ref[pad:pad + tb, :] = u
    u1 = ubuf_ref[pad - 1:pad - 1 + tb, :]
    u2 = ubuf_ref[pad - 2:pad - 2 + tb, :]
    cw = cw_ref[...]
    y = ab_ref[...].astype(F32) * (cw[0:1, :] * u2 + cw[1:2, :] * u1 + cw[2:3, :] * u)
    o_ref[...] = acc + _dot(y.astype(BF16), wa_ref[...])
    ubuf_ref[0:pad, :] = ubuf_ref[tb:tb + pad, :]


def _conv_out_proj_residual(proj, conv_w, o_gla, w_out, x, *, seq, tm=512):
    n, d = x.shape
    ch = conv_w.shape[1]
    assert w_out.shape[0] == 2 * ch and o_gla.shape[1] == ch
    kern = functools.partial(_conv_out_proj_kernel, tiles_per_seq=seq // tm)
    return pl.pallas_call(
        kern,
        grid=(n // tm,),
        in_specs=[
            pl.BlockSpec((tm, ch), lambda i: (i, 0)),
            pl.BlockSpec((tm, ch), lambda i: (i, 1)),
            pl.BlockSpec((tm, ch), lambda i: (i, 2)),
            pl.BlockSpec((CONV_K, ch), lambda i: (0, 0)),
            pl.BlockSpec((tm, o_gla.shape[1]), lambda i: (i, 0)),
            pl.BlockSpec((ch, d), lambda i: (0, 0)),
            pl.BlockSpec((ch, d), lambda i: (1, 0)),
            pl.BlockSpec((tm, d), lambda i: (i, 0)),
        ],
        out_specs=pl.BlockSpec((tm, d), lambda i: (i, 0)),
        out_shape=jax.ShapeDtypeStruct((n, d), F32),
        scratch_shapes=[pltpu.VMEM((tm + F32_SUBLANES, ch), F32)],
        compiler_params=_params("arbitrary"),
        name="conv_out_proj",
    )(proj, proj, proj, conv_w, o_gla, w_out, w_out, x)


def _gla_scores_factored(q, k, g):
    c_len = q.shape[0]
    score_rows = []
    for a in range(c_len // GLA_BLOCK):
        lo, hi = a * GLA_BLOCK, (a + 1) * GLA_BLOCK
        n_keys = LANES * -(-hi // LANES)
        g_first = g[lo:lo + 1, :]
        q_a = (q[lo:hi] * jnp.exp(g[lo:hi] - g_first)).astype(BF16)
        k_a = (k[:n_keys] * jnp.exp(jnp.minimum(g_first - g[:n_keys], GLA_MAX_BLOCK_DECAY)))
        s_a = _dot_nt(q_a, k_a.astype(BF16))
        row = lax.broadcasted_iota(jnp.int32, s_a.shape, 0) + lo
        col = lax.broadcasted_iota(jnp.int32, s_a.shape, 1)
        s_a = jnp.where(col <= row, s_a, 0.0)
        if n_keys < c_len:
            s_a = jnp.concatenate([s_a, jnp.zeros((GLA_BLOCK, c_len - n_keys), F32)], axis=1)
        score_rows.append(s_a)
    return jnp.concatenate(score_rows, axis=0)


def _gla_scores_pairwise(q, k, g):
    c_len = q.shape[0]
    n_sub = c_len // GLA_SUB
    score_rows = [jnp.zeros((GLA_SUB, c_len), F32)]
    for a in range(1, n_sub):
        lo = a * GLA_SUB
        n_keys = LANES * -(-lo // LANES)
        g_ref_row = g[lo:lo + 1, :]
        q_a = (q[lo:lo + GLA_SUB] * jnp.exp(g[lo:lo + GLA_SUB] - g_ref_row)).astype(BF16)
        k_a = (k[:n_keys] * jnp.exp(jnp.minimum(g_ref_row - g[:n_keys], 0.0))).astype(BF16)
        s_a = _dot_nt(q_a, k_a)
        col = lax.broadcasted_iota(jnp.int32, s_a.shape, 1)
        s_a = jnp.where(col < lo, s_a, 0.0)
        if n_keys < c_len:
            s_a = jnp.concatenate([s_a, jnp.zeros((GLA_SUB, c_len - n_keys), F32)], axis=1)
        score_rows.append(s_a)
    scores = jnp.concatenate(score_rows, axis=0)

    row = lax.broadcasted_iota(jnp.int32, (c_len, LANES), 0)
    lane = lax.broadcasted_iota(jnp.int32, (c_len, LANES), 1)
    delta = row % LANES - lane
    band_id = jnp.where((delta >= 0) & (delta <= row % GLA_SUB), delta, -1)
    band = jnp.zeros((c_len, LANES), F32)
    for d in range(GLA_SUB):
        k_d = k if d == 0 else pltpu.roll(k, d, 0)
        g_d = g if d == 0 else pltpu.roll(g, d, 0)
        e = jnp.exp(jnp.minimum(g - g_d, 0.0))
        diag = jnp.sum(q * k_d * e, axis=-1, keepdims=True)
        band = jnp.where(band_id == d, diag, band)
    zeros = jnp.zeros((LANES, LANES), F32)
    band_rows = []
    for t in range(c_len // LANES):
        tiles = [zeros] * (c_len // LANES)
        tiles[t] = band[t * LANES:(t + 1) * LANES]
        band_rows.append(jnp.concatenate(tiles, axis=1))
    return scores + jnp.concatenate(band_rows, axis=0)


def _gla_kernel(q_ref, k_ref, v_ref, r_ref, la_ref, tri_ref, gn_ref, o_ref, st_ref, g_ref,
                inter_ref, *, q_scale, dk, dv):
    c_len = q_ref.shape[0]

    @pl.when(pl.program_id(1) == 0)
    def _():
        st_ref[...] = jnp.zeros_like(st_ref)

    la = la_ref[...]
    tri = tri_ref[...]
    la_hi = la.astype(BF16)
    rem = la - la_hi.astype(F32)
    la_mid = rem.astype(BF16)
    la_lo = (rem - la_mid.astype(F32)).astype(BF16)
    g_all = _dot(tri, la_hi) + _dot(tri, la_mid) + _dot(tri, la_lo)
    g_ref[...] = g_all
    block_decay = [g_all[lo:lo + 1, :] - g_all[lo + GLA_BLOCK - 1:lo + GLA_BLOCK, :]
                   for lo in range(0, c_len, GLA_BLOCK)]
    worst_decay = jnp.max(jnp.concatenate(block_decay, axis=0))

    def head_operands(h):
        g = g_ref[:, h * dk:(h + 1) * dk]
        q = q_ref[:, h * dk:(h + 1) * dk].astype(F32) * q_scale
        k = k_ref[:, h * dk:(h + 1) * dk].astype(F32)
        return g, q, k, v_ref[:, h * dv:(h + 1) * dv]

    def finish_head(h, o):
        o = _rmsnorm(o, gn_ref[...])
        r = r_ref[:, h * dv:(h + 1) * dv].astype(F32)
        o_ref[:, h * dv:(h + 1) * dv] = (o * (r / (1.0 + jnp.exp(-r)))).astype(o_ref.dtype)

    for h in range(GLA_HEADS):
        g, q, k, v = head_operands(h)
        g_last = g[c_len - 1:c_len, :]
        st = st_ref[h]
        o_inter = _dot_nt((q * jnp.exp(g)).astype(BF16), st.astype(BF16))
        inter_ref[h] = o_inter
        k_dec = (k * jnp.exp(g_last - g)).astype(BF16)
        st_ref[h] = st * jnp.exp(g_last) + _dot_tn(v, k_dec)
        finish_head(h, o_inter + _dot(_gla_scores_factored(q, k, g).astype(BF16), v))

    @pl.when(worst_decay > GLA_MAX_BLOCK_DECAY)
    def _():
        for h in range(GLA_HEADS):
            g, q, k, v = head_operands(h)
            finish_head(h, inter_ref[h] + _dot(_gla_scores_pairwise(q, k, g).astype(BF16), v))


def _gla(proj, log_a, tri, norm_g, *, batch, seq, q_off, k_off, v_off, r_off, dk, dv):
    n = proj.shape[0]
    cpb = seq // GLA_CHUNK
    qk_w, vr_w = GLA_HEADS * dk, GLA_HEADS * dv
    row = lambda b, c: b * cpb + c
    kern = functools.partial(_gla_kernel, q_scale=float(dk) ** -0.5, dk=dk, dv=dv)
    return pl.pallas_call(
        kern,
        grid=(batch, cpb),
        in_specs=[
            pl.BlockSpec((GLA_CHUNK, qk_w), lambda b, c: (row(b, c), q_off // qk_w)),
            pl.BlockSpec((GLA_CHUNK, qk_w), lambda b, c: (row(b, c), k_off // qk_w)),
            pl.BlockSpec((GLA_CHUNK, vr_w), lambda b, c: (row(b, c), v_off // vr_w)),
            pl.BlockSpec((GLA_CHUNK, vr_w), lambda b, c: (row(b, c), r_off // vr_w)),
            pl.BlockSpec((GLA_CHUNK, qk_w), lambda b, c: (row(b, c), 0)),
            pl.BlockSpec((GLA_CHUNK, GLA_CHUNK), lambda b, c: (0, 0)),
            pl.BlockSpec((1, dv), lambda b, c: (0, 0)),
        ],
        out_specs=pl.BlockSpec((GLA_CHUNK, vr_w), lambda b, c: (row(b, c), 0)),
        out_shape=jax.ShapeDtypeStruct((n, vr_w), BF16),
        scratch_shapes=[pltpu.VMEM((GLA_HEADS, dv, dk), F32), pltpu.VMEM((GLA_CHUNK, qk_w), F32),
                        pltpu.VMEM((GLA_HEADS, GLA_CHUNK, dv), F32)],
        compiler_params=_params("parallel", "arbitrary"),
        name="gla",
    )(proj, proj, proj, proj, log_a, tri, norm_g)


def _attn_out_proj_kernel(a_ref, w_ref, x_ref, o_ref):
    tm, d = x_ref.shape
    acc = _dot(a_ref[...].reshape(tm, a_ref.shape[-1]), w_ref[...])
    acc = jnp.swapaxes(acc.reshape(DIL_MAX, tm // DIL_MAX, d), 0, 1).reshape(tm, d)
    o_ref[...] = x_ref[...] + acc


def _attn_out_proj_residual(attn, w, x, *, batch, seq, tm=1024):
    n, d = x.shape
    per = tm // DIL_MAX
    tps = seq // tm
    attn = attn.reshape(batch, DIL_MAX, seq // DIL_MAX, attn.shape[1])
    vmem = (2 * (_nbytes((tm, attn.shape[-1]), BF16) + 2 * _nbytes((tm, d), F32))
            + _nbytes(w.shape, BF16) + _nbytes((tm, d), F32))
    vmem_limit = min(vmem + (4 << 20), V7X_VMEM_BYTES)
    return pl.pallas_call(
        _attn_out_proj_kernel,
        grid=(n // tm,),
        in_specs=[
            pl.BlockSpec((None, DIL_MAX, per, attn.shape[-1]), lambda i: (i // tps, 0, i % tps, 0)),
            pl.BlockSpec(w.shape, lambda i: (0, 0), pipeline_mode=pl.Buffered(1)),
            pl.BlockSpec((tm, d), lambda i: (i, 0)),
        ],
        out_specs=pl.BlockSpec((tm, d), lambda i: (i, 0)),
        out_shape=jax.ShapeDtypeStruct((n, d), F32),
        compiler_params=_params("parallel", vmem_limit_bytes=vmem_limit),
        name="attn_out_proj",
    )(attn, w, x)


def _mlp_kernel(x_ref, g_ref, w1_ref, w2_ref, gf_ref, *rest, final_norm, n_cast):
    cast_in, o_ref = rest[:n_cast], rest[n_cast]
    cast_out, xn_ref = rest[n_cast + 1:2 * n_cast + 1], rest[2 * n_cast + 1]
    f = pl.program_id(1)

    def hidden_times_w2():
        for src, dst in zip(cast_in, cast_out):
            dst[...] = src[...].astype(dst.dtype)
        h = _dot(xn_ref[...], w1_ref[...])
        return _dot(jnp.square(jnp.maximum(h, 0.0)).astype(BF16), w2_ref[...])

    @pl.when(f == 0)
    def _():
        xn_ref[...] = _rmsnorm(x_ref[...], g_ref[...]).astype(BF16)
        o_ref[...] = x_ref[...] + hidden_times_w2()

    @pl.when(f > 0)
    def _():
        o_ref[...] += hidden_times_w2()

    if final_norm:
        @pl.when(f == pl.num_programs(1) - 1)
        def _():
            o_ref[...] = _rmsnorm(o_ref[...], gf_ref[...])


def _mlp_residual(x, gain, w1, w2, final_gain, *, final_norm, cast_layers=(), tm=512, tf=2048):
    n, d = x.shape
    d_ff = w1.shape[1]
    steps = (n // tm) * (d_ff // tf)
    kern = functools.partial(_mlp_kernel, final_norm=final_norm, n_cast=len(cast_layers))
    cast_in_specs, cast_out_specs, cast_shapes = [], [], []
    for arr, layer in cast_layers:
        rows, cols = arr.shape[1:]
        slab = rows // steps
        assert rows % steps == 0 and slab % BF16_SUBLANES == 0
        cast_in_specs.append(pl.BlockSpec(
            (None, slab, cols), lambda i, f, layer=layer: (layer, i * (d_ff // tf) + f, 0)))
        cast_out_specs.append(pl.BlockSpec((slab, cols), lambda i, f: (i * (d_ff // tf) + f, 0)))
        cast_shapes.append(jax.ShapeDtypeStruct((rows, cols), BF16))
    vmem = (2 * (2 * _nbytes((tm, d), F32) + 2 * _nbytes((d, tf), BF16)) + _nbytes((tm, d), BF16)
            + _nbytes((tm, tf), F32) + _nbytes((tm, tf), BF16))
    vmem += sum(2 * (_nbytes(s.block_shape[1:], F32) + _nbytes(s.block_shape[1:], BF16))
                for s in cast_in_specs)
    vmem_limit = min(vmem + (4 << 20), V7X_VMEM_BYTES)
    out = pl.pallas_call(
        kern,
        grid=(n // tm, d_ff // tf),
        in_specs=[
            pl.BlockSpec((tm, d), lambda i, f: (i, 0)),
            pl.BlockSpec((1, d), lambda i, f: (0, 0)),
            pl.BlockSpec((d, tf), lambda i, f: (0, f)),
            pl.BlockSpec((tf, d), lambda i, f: (f, 0)),
            pl.BlockSpec((1, d), lambda i, f: (0, 0)),
        ] + cast_in_specs,
        out_specs=[pl.BlockSpec((tm, d), lambda i, f: (i, 0))] + cast_out_specs,
        out_shape=[jax.ShapeDtypeStruct((n, d), F32)] + cast_shapes,
        scratch_shapes=[pltpu.VMEM((tm, d), BF16)],
        compiler_params=_params("arbitrary", "arbitrary", vmem_limit_bytes=vmem_limit),
        name="mlp",
    )(x, gain, w1, w2, final_gain, *[arr for arr, _ in cast_layers])
    return out[0], out[1:]


def _attn_bias_tables():
    slopes = np.exp2(-8.0 * np.arange(1, ATTN_HEADS + 1) / ATTN_HEADS)

    def table(q_pos, k_pos, dilation):
        diff = q_pos[:, None] - k_pos[None, :]
        valid = (diff >= 0) & (diff <= ATTN_STEPS)
        bias = -(slopes * LOG2_E)[:, None, None] * (dilation * diff)[None]
        return np.where(valid[None], bias, MASK_VALUE).astype(np.float32)

    q = np.arange(128)
    b16_first = table(q, q, 16)
    b16 = table(q + 128, np.arange(256), 16)

    q4 = 4 * (q % 32) + q // 32
    k = np.arange(256)
    k4 = 4 * (k % 64) + k // 64
    b4 = np.stack([table(q4, k4, 4), table(q4 + 128, k4, 4)], axis=1)

    q = np.arange(256)
    q1 = 16 * (q % 16) + q // 16
    k = np.arange(384)
    k1 = 16 * (k % 24) + k // 24
    b1 = np.stack([table(q1, k1, 1), table(q1 + 128, k1, 1)], axis=1)
    return tuple(jnp.asarray(t) for t in (b1, b4, b16_first, b16))


def _attn_kernel(q_all, k_all, v_all, b1_all, b4_all, b16f_all, b16_all, o_all, *scratch, sub,
                 heads):
    def gather(ref, starts, size):
        return jnp.concatenate([ref[pl.ds(s, size), :] for s in starts], axis=0)

    def scatter(ref, starts, size, val):
        for c, s in enumerate(starts):
            ref[pl.ds(s, size), :] = val[c * size:(c + 1) * size]

    def lanes(x):
        return jnp.broadcast_to(x, (x.shape[0], LANES))

    def across_keys(x, n_keys):
        return jnp.concatenate([x] * (n_keys // LANES), axis=1)

    def weighted_values_and_sum(p, v):
        pv = _dot(p.astype(BF16), jnp.concatenate([v, jnp.ones_like(v)], axis=1))
        return pv[:, :LANES], pv[:, LANES:]

    def first_tile(q, k, v, bias):
        s = _dot_nt(q, k) + bias
        m = jnp.max(s, axis=-1, keepdims=True)
        acc, l = weighted_values_and_sum(jnp.exp2(s - m), v)
        return lanes(m), l, acc

    def next_tile(q, k, v, bias, m_old, l_old, acc_old):
        s = _dot_nt(q, k) + bias
        m_new = jnp.maximum(m_old, lanes(jnp.max(s, axis=-1, keepdims=True)))
        alpha = jnp.exp2(m_old - m_new)
        acc, l = weighted_values_and_sum(jnp.exp2(s - across_keys(m_new, s.shape[1])), v)
        return m_new, alpha * l_old + l, alpha * acc_old + acc

    dh = q_all.shape[1] // heads

    def loop(trips, body):
        if trips == 1:
            body(0, 0)
        else:
            lax.fori_loop(0, trips, body, 0)

    blocks_4 = min(BLOCKS_4, sub // 32)
    blocks_1 = min(BLOCKS_1, sub // 16)
    for group in (16, 4, 1):
        for h in range(heads):
            cols = slice(h * dh, (h + 1) * dh)
            head_refs = (q_all.at[:, cols], k_all.at[:, cols], v_all.at[:, cols], o_all.at[:, cols],
                         b1_all.at[h], b4_all.at[h], b16f_all.at[h], b16_all.at[h],
                         *scratch[3 * h:3 * h + 3])
            _attn_group(group, head_refs, sub, blocks_4, blocks_1, loop, first_tile, next_tile,
                        gather, scatter)


def _attn_group(group, head_refs, sub, blocks_4, blocks_1, loop, first_tile, next_tile, gather,
                scatter):
    (q_ref, k_ref, v_ref, o_ref, b1_ref, b4_ref, b16f_ref, b16_ref, m_ref, l_ref, acc_ref) = head_refs

    def group16(i, carry):
        bias_first, bias = b16f_ref[...], b16_ref[...]
        rows, res = [], []
        for u in range(TILES_16):
            base = (TILES_16 * i + u) * sub
            for j in range(sub // ATTN_STEPS):
                q_rows = pl.ds(pl.multiple_of(base + j * ATTN_STEPS, ATTN_STEPS), ATTN_STEPS)
                if j == 0:
                    k_rows, b = q_rows, bias_first
                else:
                    k_rows = pl.ds(pl.multiple_of(base + (j - 1) * ATTN_STEPS, ATTN_STEPS),
                                   2 * ATTN_STEPS)
                    b = bias
                rows.append(q_rows)
                res.append(first_tile(q_ref[q_rows, :], k_ref[k_rows, :], v_ref[k_rows, :], b))
        for r, (m, l, acc) in zip(rows, res):
            m_ref[r, :] = m
            l_ref[r, :] = l
            acc_ref[r, :] = acc
        return carry

    def group4(i, carry):
        starts, res = [], []
        for u in range(blocks_4):
            blk = blocks_4 * i + u
            bias = b4_ref[jnp.where(blk == 0, 0, 1)]
            k_blk = jnp.maximum(blk - 1, 0)
            for r4 in range(4):
                qs = [pl.multiple_of((r4 + 4 * c) * sub + 32 * blk, 32) for c in range(4)]
                ks = [pl.multiple_of((r4 + 4 * c) * sub + 32 * k_blk, 32) for c in range(4)]
                starts.append(qs)
                res.append(next_tile(
                    gather(q_ref, qs, 32), gather(k_ref, ks, 64), gather(v_ref, ks, 64), bias,
                    gather(m_ref, qs, 32), gather(l_ref, qs, 32), gather(acc_ref, qs, 32)))
        for qs, (m, l, acc) in zip(starts, res):
            scatter(m_ref, qs, 32, m)
            scatter(l_ref, qs, 32, l)
            scatter(acc_ref, qs, 32, acc)
        return carry

    def group1(i, carry):
        res = []
        for u in range(blocks_1):
            blk = blocks_1 * i + u
            bias = b1_ref[jnp.where(blk == 0, 0, 1)]
            k_lo = jnp.maximum(16 * blk - 8, 0)
            qs = [pl.multiple_of(r * sub + 16 * blk, 16) for r in range(DIL_MAX)]
            ks = [pl.multiple_of(r * sub + k_lo, 8) for r in range(DIL_MAX)]
            _, l, acc = next_tile(
                gather(q_ref, qs, 16), gather(k_ref, ks, 24), gather(v_ref, ks, 24), bias,
                gather(m_ref, qs, 16), gather(l_ref, qs, 16), gather(acc_ref, qs, 16))
            res.append((qs, (acc / l).astype(o_ref.dtype)))
        for qs, out in res:
            scatter(o_ref, qs, 16, out)
        return carry

    if group == 16:
        loop(DIL_MAX // TILES_16, group16)
    elif group == 4:
        loop(sub // (32 * blocks_4), group4)
    else:
        loop(sub // (16 * blocks_1), group1)


def _dilated_attention(qkv, *, batch, seq, dh):
    n = qkv.shape[0]
    sub = seq // DIL_MAX
    b1, b4, b16_first, b16 = _attn_bias_tables()
    hp = ATTN_HEADS_PER_STEP
    groups = ATTN_HEADS // hp
    kern = functools.partial(_attn_kernel, sub=sub, heads=hp)
    return pl.pallas_call(
        kern,
        grid=(batch, groups),
        in_specs=[
            pl.BlockSpec((seq, hp * dh), lambda b, h: (b, h)),
            pl.BlockSpec((seq, hp * dh), lambda b, h: (b, groups + h)),
            pl.BlockSpec((seq, hp * dh), lambda b, h: (b, 2 * groups + h)),
            pl.BlockSpec((hp,) + b1.shape[1:], lambda b, h: (h, 0, 0, 0)),
            pl.BlockSpec((hp,) + b4.shape[1:], lambda b, h: (h, 0, 0, 0)),
            pl.BlockSpec((hp,) + b16_first.shape[1:], lambda b, h: (h, 0, 0)),
            pl.BlockSpec((hp,) + b16.shape[1:], lambda b, h: (h, 0, 0)),
        ],
        out_specs=pl.BlockSpec((seq, hp * dh), lambda b, h: (b, h)),
        out_shape=jax.ShapeDtypeStruct((n, ATTN_HEADS * dh), BF16),
        scratch_shapes=[pltpu.VMEM((seq, LANES), F32), pltpu.VMEM((seq, LANES), F32),
                        pltpu.VMEM((seq, dh), F32)] * hp,
        compiler_params=_params("parallel", "parallel"),
        name="dilated_attn",
    )(qkv, qkv, qkv, b1, b4, b16_first, b16)


def kernel(x, norm_mix_g, norm_mlp_g, final_norm_g, hyb_w_in, conv_w, gla_w_gate2, gla_b_gate,
           gla_norm_g, hyb_w_out, attn_w_qkv, attn_w_o, mlp_w1, mlp_w2):
    batch, seq, d = x.shape
    n = batch * seq
    depth = norm_mix_g.shape[0]
    conv_ch = conv_w.shape[-1]
    dv = gla_norm_g.shape[-1]
    dk = gla_w_gate2.shape[-1] // GLA_HEADS
    dh = d // ATTN_HEADS
    assert depth == 2 and seq % (DIL_MAX * ATTN_STEPS) == 0 and seq % GLA_CHUNK == 0
    assert dk == LANES and dh == LANES and dv % LANES == 0

    row = lambda v: v.reshape(1, -1).astype(F32)
    xf = x.reshape(n, d)

    main_cols = 3 * conv_ch + 2 * GLA_HEADS * dk + 2 * GLA_HEADS * dv
    w_in = hyb_w_in[0]
    w_low = jnp.pad(w_in[:, main_cols:], ((0, 0), (0, LANES - GLA_GATE_RANK))).astype(BF16)
    w_gate = jnp.pad(gla_w_gate2[0], ((0, LANES - GLA_GATE_RANK), (0, 0))).astype(BF16)
    proj, log_a, (w1_first, w2_first, w_out, w_qkv, w_o) = _in_proj_gla(
        xf, row(norm_mix_g[0]), w_in.astype(BF16), w_low, w_gate, row(gla_b_gate[0]),
        cols=main_cols,
        cast_layers=((mlp_w1, 0), (mlp_w2, 0), (hyb_w_out, 0), (attn_w_qkv, 0), (attn_w_o, 0)))

    q_off = 3 * conv_ch
    k_off = q_off + GLA_HEADS * dk
    v_off = k_off + GLA_HEADS * dk
    r_off = v_off + GLA_HEADS * dv
    tri = jnp.asarray(np.tril(np.ones((GLA_CHUNK, GLA_CHUNK), np.float32)), BF16)
    o_gla = _gla(proj, log_a, tri, row(gla_norm_g[0]), batch=batch, seq=seq,
                 q_off=q_off, k_off=k_off, v_off=v_off, r_off=r_off, dk=dk, dv=dv)
    xf = _conv_out_proj_residual(proj, conv_w[0].astype(F32), o_gla, w_out, xf, seq=seq)
    xf, (w1_next, w2_next) = _mlp_residual(
        xf, row(norm_mlp_g[0]), w1_first, w2_first, row(final_norm_g),
        final_norm=False, cast_layers=((mlp_w1, 1), (mlp_w2, 1)))

    col_scale = jnp.concatenate([jnp.full((1, d), float(dh) ** -0.5 * LOG2_E, F32),
                                 jnp.ones((1, 2 * d), F32)], axis=1)
    qkv = _qkv_proj_residue_major(xf, row(norm_mix_g[1]), w_qkv, col_scale, batch=batch, seq=seq)
    attn = _dilated_attention(qkv, batch=batch, seq=seq, dh=dh)
    xf = _attn_out_proj_residual(attn, w_o, xf, batch=batch, seq=seq)
    out, _ = _mlp_residual(xf, row(norm_mlp_g[1]), w1_next, w2_next, row(final_norm_g),
                           final_norm=True)
    return out.reshape(batch, seq, d)
```
